```python
import jax, jax.numpy as jnp
from jax import lax
import numpy as np

D_MODEL = 1024
BATCH = 8
SEQ = 2048
DEPTH = 1
DEC_BATCH = 32
DEC_SEQ = 8
PAST_LEN = 16384
PAGE_SIZE = 128

N_META = 16
HEAD_DIM = 64
FOX_HEADS = 8
RWKV_HEADS = 8
FOX_WIDTH = FOX_HEADS * HEAD_DIM
RWKV_WIDTH = RWKV_HEADS * HEAD_DIM
MIX_WIDTH = FOX_WIDTH + RWKV_WIDTH
DECAY_LORA = 64
ICLR_LORA = 64
GATE_LORA = 128
D_FF = -(-8 * D_MODEL // (3 * 256)) * 256
Q_BLOCK = 128
RMS_EPS = 1e-6
GN_EPS = 64e-5
FOX_COLS = 3 * FOX_WIDTH + FOX_HEADS
RWKV_COLS = 3 * RWKV_WIDTH + DECAY_LORA + ICLR_LORA + GATE_LORA
IN_COLS = FOX_COLS + RWKV_COLS

kernel_name = "fox_rwkv7_hybrid_step"


def rms_norm(x, g):
    x32 = x.astype(jnp.float32)
    y = x32 * lax.rsqrt(jnp.mean(x32 * x32, axis=-1, keepdims=True) + RMS_EPS)
    return (y * g.astype(jnp.float32)).astype(x.dtype)


def fox_attend(q, c_q, segments):
    qf = q.astype(jnp.float32) * (HEAD_DIM ** -0.5)
    cq = jnp.swapaxes(c_q, 1, 2)[..., :, None]
    logits = []
    for k, _, c_k, mask in segments:
        s = jnp.einsum('bqhd,bkhd->bhqk', qf, k.astype(jnp.float32))
        s = s + cq - jnp.swapaxes(c_k, 1, 2)[..., None, :]
        logits.append(jnp.where(mask, s, -jnp.inf))
    p = jax.nn.softmax(jnp.concatenate(logits, axis=-1), axis=-1)
    out = None
    start = 0
    for k, v, _, _ in segments:
        n = k.shape[1]
        o = jnp.einsum('bhqk,bkhd->bqhd', p[..., start:start + n], v.astype(jnp.float32))
        out = o if out is None else out + o
        start += n
    return out


def prompt_fox(q, k, v, logf):
    B, T, H, dh = q.shape
    c = jnp.cumsum(logf, axis=1)
    pos_k = jnp.arange(T)
    meta_mask = pos_k[:N_META][None, :] <= jnp.arange(N_META)[:, None]
    out_meta = fox_attend(q[:, :N_META], c[:, :N_META],
                          ((k[:, :N_META], v[:, :N_META], c[:, :N_META], meta_mask),))

    def block(i):
        start = N_META + i * Q_BLOCK
        q_b = lax.dynamic_slice_in_dim(q, start, Q_BLOCK, axis=1)
        c_b = lax.dynamic_slice_in_dim(c, start, Q_BLOCK, axis=1)
        mask = pos_k[None, :] <= (start + jnp.arange(Q_BLOCK))[:, None]
        return fox_attend(q_b, c_b, ((k, v, c, mask),))

    n_blocks = (T - N_META) // Q_BLOCK
    out = lax.map(block, jnp.arange(n_blocks))
    out = jnp.moveaxis(out, 0, 1).reshape(B, n_blocks * Q_BLOCK, H, dh)
    return jnp.concatenate([out_meta, out], axis=1)


def make_sample_fox(k_past, v_past, logf_past):
    def attn(q, k, v, logf):
        L = q.shape[1]
        P = k_past.shape[1]
        lp = logf_past.astype(jnp.float32)
        r_past = lax.cumsum(lp, axis=1, reverse=True) - lp
        c_new = jnp.cumsum(logf, axis=1)
        past_mask = jnp.ones((L, P), dtype=bool)
        new_mask = jnp.tril(jnp.ones((L, L), dtype=bool))
        return fox_attend(q, c_new, ((k_past, v_past, -r_past, past_mask),
                                     (k, v, c_new, new_mask)))
    return attn


def wkv_step(S, inp):
    r_t, w_t, k_t, v_t, a_t, b_t = inp
    sa = jnp.einsum('bhij,bhj->bhi', S, a_t)
    S = S * w_t[:, :, None, :] + sa[..., None] * b_t[:, :, None, :] + v_t[..., None] * k_t[:, :, None, :]
    y = jnp.einsum('bhij,bhj->bhi', S, r_t)
    return S, y


def rwkv7_time_mix(r, k, v, wl, al, gl, S0, w0, w2, a0, a2, g2, k_k, k_a, r_k, gn_g, gn_b):
    B, T, _ = r.shape
    r, k, v, wl, al, gl = (t.astype(jnp.float32) for t in (r, k, v, wl, al, gl))
    w_log = -jax.nn.softplus(-(w0 + jnp.tanh(wl) @ w2)) - 0.5
    decay = jnp.exp(-jnp.exp(w_log))
    a = jax.nn.sigmoid(a0 + al @ a2)
    g = jax.nn.sigmoid(gl) @ g2
    hd = lambda t: t.reshape(B, T, RWKV_HEADS, HEAD_DIM)
    kk = hd(k * k_k)
    kk = kk / jnp.maximum(jnp.sqrt(jnp.sum(kk * kk, axis=-1, keepdims=True)), 1e-12)
    k = k * (1.0 + (a - 1.0) * k_a)
    r_h, k_h, v_h, w_h, a_h = hd(r), hd(k), hd(v), hd(decay), hd(a)
    xs = tuple(jnp.moveaxis(t, 1, 0) for t in (r_h, w_h, k_h, v_h, -kk, kk * a_h))
    S, y = lax.scan(wkv_step, S0.astype(jnp.float32), xs)
    y = jnp.moveaxis(y, 0, 1)
    mean = jnp.mean(y, axis=-1, keepdims=True)
    var = jnp.mean(jnp.square(y - mean), axis=-1, keepdims=True)
    yn = (y - mean) * lax.rsqrt(var + GN_EPS) * gn_g.reshape(RWKV_HEADS, HEAD_DIM) + gn_b.reshape(RWKV_HEADS, HEAD_DIM)
    bonus = jnp.sum(r_h * k_h * r_k, axis=-1, keepdims=True) * v_h
    out = (yn + bonus).reshape(B, T, RWKV_WIDTH) * g
    return out, S


def trunk_layer(x, shift_prev, wkv_prev, attn_fn, lp):
    (ln1_g, w_in, b_f, mu_shift, w0, w2, a0, a2, g2, k_k, k_a, r_k,
     gn_g, gn_b, w_out, ln2_g, w_gate, w_up, w_down) = lp
    B, T, _ = x.shape
    xn = rms_norm(x, ln1_g)
    x_ext = jnp.concatenate([shift_prev[:, None, :].astype(xn.dtype), xn], axis=1)
    proj = x_ext @ w_in
    fox = proj[:, 1:, :FOX_COLS]
    rw = proj[:, :, FOX_COLS:]
    rw = rw[:, 1:] + (rw[:, :-1] - rw[:, 1:]) * mu_shift
    q, k, v, f = jnp.split(fox, [FOX_WIDTH, 2 * FOX_WIDTH, 3 * FOX_WIDTH], axis=-1)
    logf = jax.nn.log_sigmoid((f + b_f).astype(jnp.float32))
    q = q.reshape(B, T, FOX_HEADS, HEAD_DIM)
    k = k.reshape(B, T, FOX_HEADS, HEAD_DIM)
    v = v.reshape(B, T, FOX_HEADS, HEAD_DIM)
    attn = attn_fn(q, k, v, logf)
    r, kr, vr, wl, al, gl = jnp.split(
        rw, [RWKV_WIDTH, 2 * RWKV_WIDTH, 3 * RWKV_WIDTH, 3 * RWKV_WIDTH + DECAY_LORA,
             3 * RWKV_WIDTH + DECAY_LORA + ICLR_LORA], axis=-1)
    rwkv_out, wkv_new = rwkv7_time_mix(r, kr, vr, wl, al, gl, wkv_prev, w0, w2, a0, a2, g2,
                                       k_k, k_a, r_k, gn_g, gn_b)
    mixed = jnp.concatenate([attn.reshape(B, T, FOX_WIDTH).astype(x.dtype),
                             rwkv_out.astype(x.dtype)], axis=-1) @ w_out
    h = x + mixed
    hn = rms_norm(h, ln2_g)
    h = h + (jax.nn.silu(hn @ w_gate) * (hn @ w_up)) @ w_down
    return h, (k, v, logf.astype(x.dtype), wkv_new.astype(x.dtype), xn[:, -1])


def setup_inputs(seed: int = 0) -> dict:
    key = jax.random.key(seed)
    ks = jax.random.split(key, 40)
    f32 = jnp.float32
    nrm = lambda k, shape, scale: jax.random.normal(k, shape, f32) * scale
    n_pages = PAST_LEN // PAGE_SIZE
    n_pool = (DEC_BATCH * n_pages * 5) // 4
    return {
        "x_prompt": nrm(ks[0], (BATCH, SEQ, D_MODEL), 1.0),
        "x_sample": nrm(ks[1], (DEC_BATCH, DEC_SEQ, D_MODEL), 1.0),
        "cache_k": nrm(ks[2], (DEPTH, n_pool, PAGE_SIZE, FOX_HEADS, HEAD_DIM), 1.0),
        "cache_v": nrm(ks[3], (DEPTH, n_pool, PAGE_SIZE, FOX_HEADS, HEAD_DIM), 1.0),
        "cache_logf": jax.nn.log_sigmoid(9.0 + nrm(ks[4], (DEPTH, n_pool, PAGE_SIZE, FOX_HEADS), 0.5)),
        "state_wkv": nrm(ks[5], (DEPTH, DEC_BATCH, RWKV_HEADS, HEAD_DIM, HEAD_DIM), 0.3),
        "state_shift": nrm(ks[6], (DEPTH, DEC_BATCH, D_MODEL), 1.0),
        "page_table": jax.random.permutation(ks[7], n_pool)[:DEC_BATCH * n_pages].reshape(DEC_BATCH, n_pages).astype(jnp.int32),
        "meta_tokens": nrm(ks[8], (N_META, D_MODEL), 1.0),
        "ln1_g": 1.0 + nrm(ks[9], (DEPTH, D_MODEL), 0.05),
        "w_in": nrm(ks[10], (DEPTH, D_MODEL, IN_COLS), D_MODEL ** -0.5),
        "b_f": 6.0 + nrm(ks[11], (DEPTH, FOX_HEADS), 0.5),
        "mu_shift": jax.random.uniform(ks[12], (DEPTH, RWKV_COLS), f32),
        "w0": jax.random.uniform(ks[13], (DEPTH, RWKV_WIDTH), f32, minval=-6.0, maxval=-1.0),
        "w2": nrm(ks[14], (DEPTH, DECAY_LORA, RWKV_WIDTH), DECAY_LORA ** -0.5),
        "a0": nrm(ks[15], (DEPTH, RWKV_WIDTH), 0.1),
        "a2": nrm(ks[16], (DEPTH, ICLR_LORA, RWKV_WIDTH), ICLR_LORA ** -0.5),
        "g2": nrm(ks[17], (DEPTH, GATE_LORA, RWKV_WIDTH), GATE_LORA ** -0.5),
        "k_k": 0.85 + nrm(ks[18], (DEPTH, RWKV_WIDTH), 0.05),
        "k_a": 1.0 + nrm(ks[19], (DEPTH, RWKV_WIDTH), 0.05),
        "r_k": nrm(ks[20], (DEPTH, RWKV_HEADS, HEAD_DIM), 0.1),
        "gn_g": 1.0 + nrm(ks[21], (DEPTH, RWKV_WIDTH), 0.05),
        "gn_b": nrm(ks[22], (DEPTH, RWKV_WIDTH), 0.02),
        "w_out": nrm(ks[23], (DEPTH, MIX_WIDTH, D_MODEL), MIX_WIDTH ** -0.5),
        "ln2_g": 1.0 + nrm(ks[24], (DEPTH, D_MODEL), 0.05),
        "w_gate": nrm(ks[25], (DEPTH, D_MODEL, D_FF), D_MODEL ** -0.5),
        "w_up": nrm(ks[26], (DEPTH, D_MODEL, D_FF), D_MODEL ** -0.5),
        "w_down": nrm(ks[27], (DEPTH, D_FF, D_MODEL), D_FF ** -0.5),
        "lnf_g": 1.0 + nrm(ks[28], (D_MODEL,), 0.05),
    }


def reference(x_prompt, x_sample, cache_k, cache_v, cache_logf, state_wkv, state_shift, page_table,
              meta_tokens, ln1_g, w_in, b_f, mu_shift, w0, w2, a0, a2, g2, k_k, k_a, r_k,
              gn_g, gn_b, w_out, ln2_g, w_gate, w_up, w_down, lnf_g):
    B = x_prompt.shape[0]
    DB = x_sample.shape[0]
    n_pages = PAST_LEN // PAGE_SIZE
    xp = jnp.concatenate([jnp.broadcast_to(meta_tokens[None].astype(x_prompt.dtype), (B, N_META, D_MODEL)),
                          x_prompt], axis=1)
    xs = x_sample
    p_states = []
    s_states = []
    for l in range(DEPTH):
        lp = (ln1_g[l], w_in[l], b_f[l], mu_shift[l], w0[l], w2[l], a0[l], a2[l], g2[l], k_k[l], k_a[l],
              r_k[l], gn_g[l], gn_b[l], w_out[l], ln2_g[l], w_gate[l], w_up[l], w_down[l])
        xp, st_p = trunk_layer(xp, jnp.zeros((B, D_MODEL), xp.dtype),
                               jnp.zeros((B, RWKV_HEADS, HEAD_DIM, HEAD_DIM), jnp.float32),
                               prompt_fox, lp)
        k_past = cache_k[l][page_table].reshape(DB, n_pages * PAGE_SIZE, FOX_HEADS, HEAD_DIM)
        v_past = cache_v[l][page_table].reshape(DB, n_pages * PAGE_SIZE, FOX_HEADS, HEAD_DIM)
        logf_past = cache_logf[l][page_table].reshape(DB, n_pages * PAGE_SIZE, FOX_HEADS)
        xs, st_s = trunk_layer(xs, state_shift[l], state_wkv[l],
                               make_sample_fox(k_past, v_past, logf_past), lp)
        p_states.append(st_p)
        s_states.append(st_s)
    y_prompt = rms_norm(xp[:, N_META:], lnf_g)
    y_sample = rms_norm(xs, lnf_g)
    k_prompt = jnp.stack([s[0] for s in p_states])
    v_prompt = jnp.stack([s[1] for s in p_states])
    logf_prompt = jnp.stack([s[2] for s in p_states])
    wkv_prompt = jnp.stack([s[3] for s in p_states])
    shift_prompt = jnp.stack([s[4] for s in p_states])
    k_sample = jnp.stack([s[0] for s in s_states])
    v_sample = jnp.stack([s[1] for s in s_states])
    logf_sample = jnp.stack([s[2] for s in s_states])
    wkv_sample = jnp.stack([s[3] for s in s_states])
    shift_sample = jnp.stack([s[4] for s in s_states])
    return (y_prompt, y_sample, k_prompt, v_prompt, logf_prompt, wkv_prompt, shift_prompt,
            k_sample, v_sample, logf_sample, wkv_sample, shift_sample)
```

```python
import functools

import jax
import jax.numpy as jnp
from jax import lax
from jax.experimental import pallas as pl
from jax.experimental.pallas import tpu as pltpu

D_MODEL = 1024
N_META = 16
HEAD_DIM = 64
N_HEADS = 8
WIDTH = N_HEADS * HEAD_DIM
PAGE_SIZE = 128
DECAY_LORA = 64
ICLR_LORA = 64
GATE_LORA = 128
RWKV_COLS = 3 * WIDTH + DECAY_LORA + ICLR_LORA + GATE_LORA
RMS_EPS = 1e-6
GN_EPS = 64e-5
NEG_BIG = -1e30

LANES = 128
BIAS_GROUP = 16
HEAD_SHIFT = HEAD_DIM.bit_length() - 1
GROUP_SHIFT = BIAS_GROUP.bit_length() - 1
VMEM_LIMIT = 56 * 1024 * 1024

ROW_TILE = 256
ATT_TILE = 256
WKV_CHUNK = 64
PAGES_PER_STEP = 8
FF_CHUNK = 256

BF16 = jnp.bfloat16
F32 = jnp.float32
HIGHEST = lax.Precision.HIGHEST


def _dot(a, b, precision=None):
    return jnp.dot(a, b, preferred_element_type=F32, precision=precision)


def _dot_nt(a, b, precision=None):
    return lax.dot_general(a, b, (((1,), (1,)), ((), ())), preferred_element_type=F32,
                           precision=precision)


def _dot_tn(a, b, precision=None):
    return lax.dot_general(a, b, (((0,), (0,)), ((), ())), preferred_element_type=F32,
                           precision=precision)


def _rms(x, g):
    return x * lax.rsqrt(jnp.mean(x * x, axis=-1, keepdims=True) + RMS_EPS) * g


def _softplus(z):
    return jnp.maximum(z, 0.0) + jnp.log1p(jnp.exp(-jnp.abs(z)))


def _sigmoid(z):
    return 1.0 / (1.0 + jnp.exp(-z))


def _mask_bf16(cond):
    return cond.astype(F32).astype(BF16)


def _split3(c):
    hi = c.astype(BF16)
    r1 = c - hi.astype(F32)
    mid = r1.astype(BF16)
    lo = (r1 - mid.astype(F32)).astype(BF16)
    return hi, mid, lo


def _inproj_outputs(xn, qkv, logf, c, p, p_prev, w, outs):
    (mu_ref, sel_ref, ones_ref, w2_ref, a2_ref, g2_ref, w0_ref, a0_ref) = w
    (q_ref, k_ref, kb_ref, v_ref, vb_ref, lf_ref, cq_ref, ck_ref,
     r_ref, lw_ref, kr_ref, vr_ref, a_ref, g_ref) = outs
    q_ref[0] = (qkv[:, :WIDTH] * (HEAD_DIM ** -0.5)).astype(BF16)
    k = qkv[:, WIDTH:2 * WIDTH]
    k_ref[0] = k
    kb_ref[0] = k.astype(BF16)
    v = qkv[:, 2 * WIDTH:]
    v_ref[0] = v
    vb_ref[0] = v.astype(BF16)
    lf_ref[0] = logf[:, :N_HEADS]
    cparts = jnp.concatenate(_split3(c), axis=1)
    cc = _dot(cparts, sel_ref[...]) + ones_ref[...]
    cq_ref[0] = cc[:, :LANES].astype(BF16)
    ck_ref[0] = cc[:, LANES:].astype(BF16)
    rw = p + (p_prev - p) * mu_ref[...]
    r_ref[0] = rw[:, :WIDTH]
    kr_ref[0] = rw[:, WIDTH:2 * WIDTH]
    vr_ref[0] = rw[:, 2 * WIDTH:3 * WIDTH]
    z = rw[:, 3 * WIDTH:3 * WIDTH + LANES]
    gl = rw[:, 3 * WIDTH + LANES:]
    w_log = -_softplus(-(w0_ref[...] + _dot(jnp.tanh(z).astype(BF16), w2_ref[...]))) - 0.5
    lw_ref[0] = -jnp.exp(w_log)
    a_ref[0] = _sigmoid(a0_ref[...] + _dot(z.astype(BF16), a2_ref[...]))
    g_ref[0] = _dot(_sigmoid(gl).astype(BF16), g2_ref[...])


def _inproj_long_kernel(x_ref, prow_ref, c0_ref, ln_ref, wqkv_ref, wf_ref, bf_ref, wrw_ref,
                        mu_ref, sel_ref, ones_ref, w2_ref, a2_ref, g2_ref, w0_ref, a0_ref,
                        q_ref, k_ref, kb_ref, v_ref, vb_ref, lf_ref, cq_ref, ck_ref,
                        r_ref, lw_ref, kr_ref, vr_ref, a_ref, g_ref, xl_ref, pl_ref, cl_ref,
                        pcar_ref, ccar_ref):
    @pl.when(pl.program_id(1) == 0)
    def _():
        pcar_ref[...] = prow_ref[0]
        ccar_ref[...] = c0_ref[0]

    rows = x_ref.shape[1]
    xn = _rms(x_ref[0], ln_ref[...])
    xb = xn.astype(BF16)
    qkv = _dot(xb, wqkv_ref[...])
    logf = -_softplus(-(_dot(xb, wf_ref[...]) + bf_ref[...]))
    ri = lax.broadcasted_iota(jnp.int32, (rows, rows), 0)
    ci = lax.broadcasted_iota(jnp.int32, (rows, rows), 1)
    c = _dot((ri >= ci).astype(F32), logf, HIGHEST) + ccar_ref[...]
    ccar_ref[...] = c[rows - 1:rows, :]
    p = _dot(xb, wrw_ref[...])
    first = lax.broadcasted_iota(jnp.int32, p.shape, 0) == 0
    p_prev = jnp.where(first, pcar_ref[...], pltpu.roll(p, 1, 0))
    pcar_ref[...] = p[rows - 1:rows, :]
    xl_ref[0] = xn[rows - 1:rows, :]
    pl_ref[0] = p[rows - 1:rows, :]
    cl_ref[0] = c[rows - 1:rows, :]
    _inproj_outputs(xn, qkv, logf, c, p, p_prev,
                    (mu_ref, sel_ref, ones_ref, w2_ref, a2_ref, g2_ref, w0_ref, a0_ref),
                    (q_ref, k_ref, kb_ref, v_ref, vb_ref, lf_ref, cq_ref, ck_ref,
                     r_ref, lw_ref, kr_ref, vr_ref, a_ref, g_ref))


def _inproj_packed_kernel(x_ref, xprev_ref, ln_ref, wqkv_ref, wf_ref, bf_ref, wrw_ref,
                          mu_ref, sel_ref, ones_ref, w2_ref, a2_ref, g2_ref, w0_ref, a0_ref,
                          q_ref, k_ref, kb_ref, v_ref, vb_ref, lf_ref, cq_ref, ck_ref,
                          r_ref, lw_ref, kr_ref, vr_ref, a_ref, g_ref, xn_ref, *, seq_len):
    rows = x_ref.shape[1]
    shift = seq_len.bit_length() - 1
    xn = _rms(x_ref[0], ln_ref[...])
    xn_ref[0] = xn
    xb = xn.astype(BF16)
    qkv = _dot(xb, wqkv_ref[...])
    logf = -_softplus(-(_dot(xb, wf_ref[...]) + bf_ref[...]))
    ri = lax.broadcasted_iota(jnp.int32, (rows, rows), 0)
    ci = lax.broadcasted_iota(jnp.int32, (rows, rows), 1)
    same_seq = (ri >> shift) == (ci >> shift)
    c = _dot(((ri >= ci) & same_seq).astype(F32), logf, HIGHEST)
    p = _dot(xb, wrw_ref[...])
    p_first = _dot(xprev_ref[0].astype(BF16), wrw_ref[...])
    first = (lax.broadcasted_iota(jnp.int32, p.shape, 0) & (seq_len - 1)) == 0
    p_prev = jnp.where(first, p_first, pltpu.roll(p, 1, 0))
    _inproj_outputs(xn, qkv, logf, c, p, p_prev,
                    (mu_ref, sel_ref, ones_ref, w2_ref, a2_ref, g2_ref, w0_ref, a0_ref),
                    (q_ref, k_ref, kb_ref, v_ref, vb_ref, lf_ref, cq_ref, ck_ref,
                     r_ref, lw_ref, kr_ref, vr_ref, a_ref, g_ref))


def _const_spec(arr):
    return pl.BlockSpec(arr.shape, lambda *_: (0,) * arr.ndim)


def _inproj_out_shapes(batch, rows_total):
    def s(width, dtype):
        return jax.ShapeDtypeStruct((batch, rows_total, width), dtype)
    return [s(WIDTH, BF16), s(WIDTH, F32), s(WIDTH, BF16), s(WIDTH, F32), s(WIDTH, BF16),
            s(N_HEADS, F32), s(LANES, BF16), s(LANES, BF16)] + [s(WIDTH, F32)] * 6


def _inproj_out_specs(tile):
    def s(width):
        return pl.BlockSpec((1, tile, width), lambda b, i: (b, i, 0))
    return [s(WIDTH)] * 5 + [s(N_HEADS), s(LANES), s(LANES)] + [s(WIDTH)] * 6


def _inproj_long(x, prow, c0, weights, tile):
    batch, seq, _ = x.shape
    n_tiles = seq // tile
    row = lambda width: pl.BlockSpec((1, 1, width), lambda b, i: (b, 0, 0))
    shapes = _inproj_out_shapes(batch, seq) + [
        jax.ShapeDtypeStruct((batch, 1, D_MODEL), F32),
        jax.ShapeDtypeStruct((batch, 1, RWKV_COLS), F32),
        jax.ShapeDtypeStruct((batch, 1, LANES), F32)]
    specs = _inproj_out_specs(tile) + [row(D_MODEL), row(RWKV_COLS), row(LANES)]
    return pl.pallas_call(
        _inproj_long_kernel,
        grid=(batch, n_tiles),
        in_specs=[pl.BlockSpec((1, tile, D_MODEL), lambda b, i: (b, i, 0)),
                  _const_spec(prow), _const_spec(c0)] + [_const_spec(w) for w in weights],
        out_specs=specs,
        out_shape=shapes,
        scratch_shapes=[pltpu.VMEM((1, RWKV_COLS), F32), pltpu.VMEM((1, LANES), F32)],
        compiler_params=pltpu.CompilerParams(
            dimension_semantics=("arbitrary", "arbitrary"), vmem_limit_bytes=VMEM_LIMIT),
        name="inproj_long",
    )(x, prow, c0, *weights)


def _inproj_packed(x, xprev, weights, seq_len):
    _, rows, _ = x.shape
    full = pl.BlockSpec((1, rows, D_MODEL), lambda b, i: (0, 0, 0))
    return pl.pallas_call(
        functools.partial(_inproj_packed_kernel, seq_len=seq_len),
        grid=(1, 1),
        in_specs=[full, full] + [_const_spec(w) for w in weights],
        out_specs=_inproj_out_specs(rows) + [full],
        out_shape=_inproj_out_shapes(1, rows) + [jax.ShapeDtypeStruct((1, rows, D_MODEL), F32)],
        compiler_params=pltpu.CompilerParams(
            dimension_semantics=("arbitrary", "arbitrary"), vmem_limit_bytes=VMEM_LIMIT),
        name="inproj_packed",
    )(x, xprev, *weights)


def _fox_prompt_kernel(q_ref, cq_ref, km_ref, ckm_ref, vm_ref, k_ref, ck_ref, v_ref, o_ref):
    pair = pl.program_id(1)
    qi = pl.program_id(2)
    tile = q_ref.shape[1]
    lane = lax.broadcasted_iota(jnp.int32, (1, LANES), 1)
    q2 = q_ref[0]
    cq = cq_ref[0]
    lhs = []
    for hh in range(2):
        head_lanes = (lane >= hh * HEAD_DIM) & (lane < (hh + 1) * HEAD_DIM)
        g0 = (2 * pair + hh) * BIAS_GROUP
        group_lanes = (lane >= g0) & (lane < g0 + BIAS_GROUP)
        lhs.append(jnp.concatenate([q2 * _mask_bf16(head_lanes),
                                    cq * _mask_bf16(group_lanes)], axis=1))

    def block(carry, kk, vv, mask):
        out = []
        for hh in range(2):
            m, l, acc = carry[hh]
            s = _dot_nt(lhs[hh], kk)
            if mask is not None:
                s = jnp.where(mask, s, NEG_BIG)
            m_new = jnp.maximum(m, jnp.max(s, axis=-1, keepdims=True))
            alpha = jnp.exp(m - m_new)
            pe = jnp.exp(s - m_new)
            l_new = alpha * l + jnp.sum(pe, axis=-1, keepdims=True)
            acc_new = alpha * acc + _dot(pe.astype(BF16), vv)
            out.append((m_new, l_new, acc_new))
        return tuple(out)

    init = tuple((jnp.full((tile, 1), NEG_BIG, F32), jnp.zeros((tile, 1), F32),
                  jnp.zeros((tile, LANES), F32)) for _ in range(2))
    carry = block(init, jnp.concatenate([km_ref[0], ckm_ref[0]], axis=1), vm_ref[0], None)

    def body(j, carry):
        start = pl.multiple_of(j * tile, tile)
        kk = jnp.concatenate([k_ref[0, pl.ds(start, tile), :], ck_ref[0, pl.ds(start, tile), :]],
                             axis=1)
        return block(carry, kk, v_ref[0, pl.ds(start, tile), :], None)

    carry = lax.fori_loop(0, qi, body, carry)
    start = pl.multiple_of(qi * tile, tile)
    kk = jnp.concatenate([k_ref[0, pl.ds(start, tile), :], ck_ref[0, pl.ds(start, tile), :]], axis=1)
    causal = (lax.broadcasted_iota(jnp.int32, (tile, tile), 0)
              >= lax.broadcasted_iota(jnp.int32, (tile, tile), 1))
    carry = block(carry, kk, v_ref[0, pl.ds(start, tile), :], causal)
    o0 = carry[0][2] / carry[0][1]
    o1 = carry[1][2] / carry[1][1]
    first_head = lax.broadcasted_iota(jnp.int32, o0.shape, 1) < HEAD_DIM
    o_ref[0] = jnp.where(first_head, o0, o1).astype(BF16)


def _fox_prompt(q, cq, k_meta, ck_meta, v_meta, k, ck, v):
    batch, seq, _ = q.shape
    n_pairs = N_HEADS // 2
    return pl.pallas_call(
        _fox_prompt_kernel,
        grid=(batch, n_pairs, seq // ATT_TILE),
        in_specs=[
            pl.BlockSpec((1, ATT_TILE, LANES), lambda b, p, i: (b, i, p)),
            pl.BlockSpec((1, ATT_TILE, LANES), lambda b, p, i: (b, i, 0)),
            pl.BlockSpec((1, N_META, LANES), lambda b, p, i: (0, 0, p)),
            pl.BlockSpec((1, N_META, LANES), lambda b, p, i: (0, 0, 0)),
            pl.BlockSpec((1, N_META, LANES), lambda b, p, i: (0, 0, p)),
            pl.BlockSpec((1, seq, LANES), lambda b, p, i: (b, 0, p)),
            pl.BlockSpec((1, seq, LANES), lambda b, p, i: (b, 0, 0)),
            pl.BlockSpec((1, seq, LANES), lambda b, p, i: (b, 0, p)),
        ],
        out_specs=pl.BlockSpec((1, ATT_TILE, LANES), lambda b, p, i: (b, i, p)),
        out_shape=jax.ShapeDtypeStruct((batch, seq, WIDTH), BF16),
        compiler_params=pltpu.CompilerParams(
            dimension_semantics=("arbitrary", "arbitrary", "arbitrary"),
            vmem_limit_bytes=VMEM_LIMIT),
        name="fox_prompt",
    )(q, cq, k_meta, ck_meta, v_meta, k, ck, v)


def _fox_sample_kernel(pt_ref, q_ref, cq_ref, kn_ref, ckn_ref, vn_ref, *rest, n_new):
    del pt_ref
    g_pages = PAGES_PER_STEP
    k_refs, v_refs, lf_refs = rest[:g_pages], rest[g_pages:2 * g_pages], rest[2 * g_pages:3 * g_pages]
    o_ref, qbd_ref, cqm_ref, roff_ref, m_ref, l_ref, acc_ref = rest[3 * g_pages:]
    step = pl.program_id(1)
    rows = N_HEADS * n_new
    shift = n_new.bit_length() - 1
    row_head = lax.broadcasted_iota(jnp.int32, (rows, 1), 0) >> shift

    @pl.when(step == 0)
    def _():
        q = jnp.concatenate([q_ref[0].astype(F32)] * N_HEADS, axis=0)
        lane_head = lax.broadcasted_iota(jnp.int32, (1, WIDTH), 1) >> HEAD_SHIFT
        qbd_ref[...] = jnp.where(row_head == lane_head, q, 0.0).astype(BF16)
        cq = jnp.concatenate([cq_ref[0]] * N_HEADS, axis=0).astype(F32)
        lane = lax.broadcasted_iota(jnp.int32, (1, LANES), 1)
        cqm = jnp.where((lane >> GROUP_SHIFT) == row_head, cq, 0.0)
        cqm_ref[...] = cqm.astype(BF16)
        slot = lax.broadcasted_iota(jnp.int32, cqm.shape, 1) & (BIAS_GROUP - 1)
        roff_ref[...] = jnp.sum(jnp.where(slot < 3, cqm, 0.0), axis=-1, keepdims=True)
        m_ref[...] = jnp.full(m_ref.shape, NEG_BIG, F32)
        l_ref[...] = jnp.zeros(l_ref.shape, F32)
        acc_ref[...] = jnp.zeros(acc_ref.shape, F32)

    qbd = qbd_ref[...]
    ri = lax.broadcasted_iota(jnp.int32, (2 * PAGE_SIZE, PAGE_SIZE), 0)
    ci = lax.broadcasted_iota(jnp.int32, (2 * PAGE_SIZE, PAGE_SIZE), 1)
    suffix = jnp.where(ri < PAGE_SIZE, (ci > ri).astype(F32), (ci == ri - PAGE_SIZE).astype(F32))
    expand = (row_head == lax.broadcasted_iota(jnp.int32, (1, N_HEADS), 1)).astype(F32)

    carry = roff_ref[...]
    scores = []
    for g in range(g_pages):
        y = _dot(suffix, lf_refs[g][0], HIGHEST)
        bb = _dot_nt(expand, y, HIGHEST)
        later, own = bb[:, :PAGE_SIZE], bb[:, PAGE_SIZE:]
        scores.append(_dot_nt(qbd, k_refs[g][0].astype(BF16)) + later + carry)
        carry = carry + later[:, 0:1] + own[:, 0:1]
    roff_ref[...] = carry

    m = m_ref[...]
    m_new = m
    for s in scores:
        m_new = jnp.maximum(m_new, jnp.max(s, axis=-1, keepdims=True))
    alpha = jnp.exp(m - m_new)
    l_new = alpha * l_ref[...]
    acc = alpha * acc_ref[...]
    for g, s in enumerate(scores):
        pe = jnp.exp(s - m_new)
        l_new = l_new + jnp.sum(pe, axis=-1, keepdims=True)
        acc = acc + _dot(pe.astype(BF16), v_refs[g][0].astype(BF16))
    m_ref[...] = m_new
    l_ref[...] = l_new
    acc_ref[...] = acc

    @pl.when(step == pl.num_programs(1) - 1)
    def _():
        s = _dot_nt(qbd, kn_ref[0]) + _dot_nt(cqm_ref[...], ckn_ref[0])
        row_tok = lax.broadcasted_iota(jnp.int32, (rows, 1), 0) & (n_new - 1)
        s = jnp.where(lax.broadcasted_iota(jnp.int32, (1, n_new), 1) <= row_tok, s, NEG_BIG)
        m_fin = jnp.maximum(m_new, jnp.max(s, axis=-1, keepdims=True))
        a_fin = jnp.exp(m_new - m_fin)
        pe = jnp.exp(s - m_fin)
        l_fin = a_fin * l_new + jnp.sum(pe, axis=-1, keepdims=True)
        out = (a_fin * acc + _dot(pe.astype(BF16), vn_ref[0])) / l_fin
        lane_head = lax.broadcasted_iota(jnp.int32, (n_new, WIDTH), 1) >> HEAD_SHIFT
        res = jnp.zeros((n_new, WIDTH), F32)
        for h in range(N_HEADS):
            res = res + jnp.where(lane_head == h, out[h * n_new:(h + 1) * n_new, :], 0.0)
        o_ref[0] = res.astype(BF16)


def _fox_sample(page_table, q, cq, k_new, ck_new, v_new, cache_k, cache_v, cache_logf):
    n_seq, n_new, _ = q.shape
    n_pages = page_table.shape[1]
    g_pages = PAGES_PER_STEP
    rows = N_HEADS * n_new

    def per_seq(width):
        return pl.BlockSpec((1, n_new, width), lambda b, j, pt: (b, 0, 0))

    def page_spec(g, width):
        return pl.BlockSpec((1, PAGE_SIZE, width),
                            lambda b, j, pt: (pt[b, n_pages - 1 - (j * g_pages + g)], 0, 0))

    grid_spec = pltpu.PrefetchScalarGridSpec(
        num_scalar_prefetch=1,
        grid=(n_seq, n_pages // g_pages),
        in_specs=([per_seq(WIDTH), per_seq(LANES), per_seq(WIDTH), per_seq(LANES), per_seq(WIDTH)]
                  + [page_spec(g, WIDTH) for g in range(g_pages)]
                  + [page_spec(g, WIDTH) for g in range(g_pages)]
                  + [page_spec(g, N_HEADS) for g in range(g_pages)]),
        out_specs=pl.BlockSpec((1, n_new, WIDTH), lambda b, j, pt: (b, 0, 0)),
        scratch_shapes=[pltpu.VMEM((rows, WIDTH), BF16), pltpu.VMEM((rows, LANES), BF16),
                        pltpu.VMEM((rows, 1), F32), pltpu.VMEM((rows, 1), F32),
                        pltpu.VMEM((rows, 1), F32), pltpu.VMEM((rows, WIDTH), F32)],
    )
    return pl.pallas_call(
        functools.partial(_fox_sample_kernel, n_new=n_new),
        grid_spec=grid_spec,
        out_shape=jax.ShapeDtypeStruct((n_seq, n_new, WIDTH), BF16),
        compiler_params=pltpu.CompilerParams(
            dimension_semantics=("arbitrary", "arbitrary"), vmem_limit_bytes=VMEM_LIMIT),
        name="fox_sample",
    )(page_table, q, cq, k_new, ck_new, v_new,
      *([cache_k] * g_pages), *([cache_v] * g_pages), *([cache_logf] * g_pages))


def _wkv_kernel(r_ref, lw_ref, kr_ref, vr_ref, a_ref, g_ref, s0_ref,
                kk_ref, ka_ref, rk_ref, gg_ref, gb_ref, o_ref, so_ref, s_sc):
    @pl.when(pl.program_id(1) == 0)
    def _():
        s_sc[...] = s0_ref[0]

    C = r_ref.shape[1]
    ri = lax.broadcasted_iota(jnp.int32, (C, C), 0)
    ci = lax.broadcasted_iota(jnp.int32, (C, C), 1)
    incl = ri >= ci
    strict = ri > ci
    eye = (ri == ci).astype(F32)
    lcum = _dot(incl.astype(F32), lw_ref[0], HIGHEST)
    levels = max(1, (C - 1).bit_length())

    for pair in range(N_HEADS // 2):
        lanes = slice(pair * LANES, (pair + 1) * LANES)
        blocks = [ref[0, :, lanes] for ref in (r_ref, lw_ref, kr_ref, vr_ref, a_ref, g_ref)]
        blocks.append(lcum[:, lanes])
        outs = []
        for hh in range(2):
            h = 2 * pair + hh
            if hh == 0:
                rh, lwh, krh, vh, ah, gh, lc = [x[:, :HEAD_DIM] for x in blocks]
            else:
                rh, lwh, krh, vh, ah, gh, lc = [pltpu.roll(x, HEAD_DIM, 1)[:, :HEAD_DIM] for x in blocks]
            kkr = krh * kk_ref[h:h + 1, :]
            norm = jnp.sqrt(jnp.sum(kkr * kkr, axis=-1, keepdims=True))
            kk = kkr / jnp.maximum(norm, 1e-12)
            kp = krh * (1.0 + (ah - 1.0) * ka_ref[h:h + 1, :])
            bv = kk * ah
            e_neg = jnp.exp(-lc)
            l_end = lc[C - 1:C, :]
            e_rem = jnp.exp(l_end - lc)
            rt = rh * jnp.exp(lc)
            at = -kk * jnp.exp(lc - lwh)
            x = jnp.concatenate([at, rt], axis=0).astype(BF16)
            sc_b = _dot_nt(x, (bv * e_neg).astype(BF16))
            sc_k = _dot_nt(x, (kp * e_neg).astype(BF16))
            a_ab = jnp.where(strict, sc_b[:C], 0.0)
            a_ak = jnp.where(strict, sc_k[:C], 0.0)
            a_rb = jnp.where(incl, sc_b[C:], 0.0)
            a_rk = jnp.where(incl, sc_k[C:], 0.0)
            tinv = eye + a_ab
            lpow = a_ab
            for _ in range(levels - 1):
                lpow = _dot(lpow, lpow, HIGHEST)
                tinv = tinv + _dot(tinv, lpow, HIGHEST)
            s_old = s_sc[h]
            ps = _dot_nt(x, s_old.astype(BF16))
            vb = vh.astype(BF16)
            u = _dot(tinv, ps[:C] + _dot(a_ak.astype(BF16), vb), HIGHEST)
            ub = u.astype(BF16)
            y = ps[C:] + _dot(a_rb.astype(BF16), ub) + _dot(a_rk.astype(BF16), vb)
            s_sc[h] = (s_old * jnp.exp(l_end)
                       + _dot_tn(ub, (bv * e_rem).astype(BF16))
                       + _dot_tn(vb, (kp * e_rem).astype(BF16)))
            mean = jnp.mean(y, axis=-1, keepdims=True)
            yc = y - mean
            var = jnp.mean(yc * yc, axis=-1, keepdims=True)
            yn = yc * lax.rsqrt(var + GN_EPS) * gg_ref[h:h + 1, :] + gb_ref[h:h + 1, :]
            bonus = jnp.sum(rh * kp * rk_ref[h:h + 1, :], axis=-1, keepdims=True) * vh
            outs.append((yn + bonus) * gh)
        o_ref[0, :, lanes] = jnp.concatenate(outs, axis=1).astype(BF16)

    @pl.when(pl.program_id(1) == pl.num_programs(1) - 1)
    def _():
        so_ref[0] = s_sc[...]


def _wkv(r, lw, kr, vr, a, g, s0, params, chunk):
    batch, seq, _ = r.shape
    shared_state = s0.shape[0] == 1
    tok = pl.BlockSpec((1, chunk, WIDTH), lambda b, c: (b, c, 0))
    state_in = pl.BlockSpec((1, N_HEADS, HEAD_DIM, HEAD_DIM),
                            (lambda b, c: (0, 0, 0, 0)) if shared_state else (lambda b, c: (b, 0, 0, 0)))
    state_out = pl.BlockSpec((1, N_HEADS, HEAD_DIM, HEAD_DIM), lambda b, c: (b, 0, 0, 0))
    return pl.pallas_call(
        _wkv_kernel,
        grid=(batch, seq // chunk),
        in_specs=[tok] * 6 + [state_in] + [_const_spec(p) for p in params],
        out_specs=[tok, state_out],
        out_shape=[jax.ShapeDtypeStruct((batch, seq, WIDTH), BF16),
                   jax.ShapeDtypeStruct((batch, N_HEADS, HEAD_DIM, HEAD_DIM), F32)],
        scratch_shapes=[pltpu.VMEM((N_HEADS, HEAD_DIM, HEAD_DIM), F32)],
        compiler_params=pltpu.CompilerParams(
            dimension_semantics=("arbitrary", "arbitrary"), vmem_limit_bytes=VMEM_LIMIT),
        name="wkv",
    )(r, lw, kr, vr, a, g, s0, *params)


def _out_ffn_kernel(x_ref, att_ref, rw_ref, woa_ref, wor_ref, ln2_ref, wg_ref, wu_ref, wd_ref,
                    lnf_ref, y_ref):
    h = x_ref[...] + _dot(att_ref[...], woa_ref[...]) + _dot(rw_ref[...], wor_ref[...])
    hb = _rms(h, ln2_ref[...]).astype(BF16)
    ffn = jnp.zeros(h.shape, F32)
    d_ff = wg_ref.shape[1]
    for c0 in range(0, d_ff, FF_CHUNK):
        gate = _dot(hb, wg_ref[:, c0:c0 + FF_CHUNK])
        up = _dot(hb, wu_ref[:, c0:c0 + FF_CHUNK])
        act = gate * _sigmoid(gate) * up
        ffn = ffn + _dot(act.astype(BF16), wd_ref[c0:c0 + FF_CHUNK, :])
    y_ref[...] = _rms(h + ffn, lnf_ref[...])


def _out_ffn(x, att, rw, weights):
    rows = x.shape[0]
    tile = min(ROW_TILE, rows)
    tok = lambda width: pl.BlockSpec((tile, width), lambda i: (i, 0))
    return pl.pallas_call(
        _out_ffn_kernel,
        grid=(rows // tile,),
        in_specs=[tok(D_MODEL), tok(WIDTH), tok(WIDTH)] + [_const_spec(w) for w in weights],
        out_specs=tok(D_MODEL),
        out_shape=jax.ShapeDtypeStruct((rows, D_MODEL), F32),
        compiler_params=pltpu.CompilerParams(
            dimension_semantics=("arbitrary",), vmem_limit_bytes=VMEM_LIMIT),
        name="out_ffn",
    )(x, att, rw, *weights)


def _bias_selectors():
    part = jnp.arange(3)[:, None, None]
    src = jnp.arange(LANES)[None, :, None]
    dst = jnp.arange(LANES)[None, None, :]
    is_head = src < N_HEADS
    q_sel = (is_head & (dst == src * BIAS_GROUP + part)).astype(F32)
    k_sel = -(is_head & (dst == src * BIAS_GROUP + 3 + part)).astype(F32)
    sel = jnp.concatenate([q_sel, k_sel], axis=2).reshape(3 * LANES, 2 * LANES)
    slot = jnp.arange(LANES) % BIAS_GROUP
    ones = jnp.concatenate([(slot >= 3) & (slot < 6), slot < 3]).astype(F32)[None, :]
    return sel.astype(BF16), ones


def kernel(x_prompt, x_sample, cache_k, cache_v, cache_logf, state_wkv, state_shift, page_table,
           meta_tokens, ln1_g, w_in, b_f, mu_shift, w0, w2, a0, a2, g2, k_k, k_a, r_k,
           gn_g, gn_b, w_out, ln2_g, w_gate, w_up, w_down, lnf_g):
    batch, seq, _ = x_prompt.shape
    n_seq, n_new, _ = x_sample.shape
    n_pool = cache_k.shape[1]
    fox_cols = 3 * WIDTH + N_HEADS

    w_in0 = w_in[0]
    wqkv = w_in0[:, :3 * WIDTH].astype(BF16)
    wf = jnp.pad(w_in0[:, 3 * WIDTH:fox_cols], ((0, 0), (0, LANES - N_HEADS))).astype(BF16)
    bf = jnp.pad(b_f[0], (0, LANES - N_HEADS))[None, :]
    wrw = w_in0[:, fox_cols:].astype(BF16)
    sel, ones = _bias_selectors()
    w2p = jnp.concatenate([w2[0], jnp.zeros((ICLR_LORA, WIDTH), F32)], axis=0).astype(BF16)
    a2p = jnp.concatenate([jnp.zeros((DECAY_LORA, WIDTH), F32), a2[0]], axis=0).astype(BF16)
    inproj_w = (ln1_g[0][None, :], wqkv, wf, bf, wrw, mu_shift[0][None, :], sel, ones,
                w2p, a2p, g2[0].astype(BF16), w0[0][None, :], a0[0][None, :])
    per_head = lambda t: t.reshape(N_HEADS, HEAD_DIM)
    wkv_p = (per_head(k_k[0]), per_head(k_a[0]), r_k[0], per_head(gn_g[0]), per_head(gn_b[0]))
    ffn_w = (w_out[0][:WIDTH].astype(BF16), w_out[0][WIDTH:].astype(BF16), ln2_g[0][None, :],
             w_gate[0].astype(BF16), w_up[0].astype(BF16), w_down[0].astype(BF16), lnf_g[None, :])

    zrow = jnp.zeros((1, 1, RWKV_COLS), F32)
    zc = jnp.zeros((1, 1, LANES), F32)
    mo = _inproj_long(meta_tokens[None], zrow, zc, inproj_w, N_META)
    (_, mk, mkb, mv, mvb, mlf, _, mck, mr, mlw, mkr, mvr, ma, mg, _, mprow, mclast) = mo
    zstate = jnp.zeros((1, N_HEADS, HEAD_DIM, HEAD_DIM), F32)
    _, s_meta = _wkv(mr, mlw, mkr, mvr, ma, mg, zstate, wkv_p, N_META)

    po = _inproj_long(x_prompt, mprow, mclast, inproj_w, ROW_TILE)
    (pq, pk, pkb, pv, pvb, plf, pcq, pck, pr, plw, pkr, pvr, pa, pg, pxl, _, _) = po
    att_p = _fox_prompt(pq, pcq, mkb, mck, mvb, pkb, pck, pvb)
    rw_p, wkv_prompt = _wkv(pr, plw, pkr, pvr, pa, pg, s_meta, wkv_p, WKV_CHUNK)
    y_prompt = _out_ffn(x_prompt.reshape(batch * seq, D_MODEL), att_p.reshape(batch * seq, WIDTH),
                        rw_p.reshape(batch * seq, WIDTH), ffn_w).reshape(batch, seq, D_MODEL)

    rows_s = n_seq * n_new
    xprev = jnp.repeat(state_shift[0], n_new, axis=0)[None]
    so = _inproj_packed(x_sample.reshape(1, rows_s, D_MODEL), xprev, inproj_w, n_new)
    (sq, sk, skb, sv, svb, slf, scq, sck, sr, slw, skr, svr, sa, sg, sxn) = so
    seqs = lambda t: t.reshape(n_seq, n_new, t.shape[-1])
    att_s = _fox_sample(page_table, seqs(sq), seqs(scq), seqs(skb), seqs(sck), seqs(svb),
                        cache_k[0].reshape(n_pool, PAGE_SIZE, WIDTH),
                        cache_v[0].reshape(n_pool, PAGE_SIZE, WIDTH), cache_logf[0])
    rw_s, wkv_sample = _wkv(seqs(sr), seqs(slw), seqs(skr), seqs(svr), seqs(sa), seqs(sg),
                            state_wkv[0], wkv_p, n_new)
    y_sample = _out_ffn(x_sample.reshape(rows_s, D_MODEL), att_s.reshape(rows_s, WIDTH),
                        rw_s.reshape(rows_s, WIDTH), ffn_w).reshape(n_seq, n_new, D_MODEL)

    def with_meta(meta, main):
        return jnp.concatenate([jnp.broadcast_to(meta, (batch,) + meta.shape[1:]), main], axis=1)

    heads = lambda t: t.reshape(t.shape[:-1] + (N_HEADS, HEAD_DIM))
    k_prompt = heads(with_meta(mk, pk))[None]
    v_prompt = heads(with_meta(mv, pv))[None]
    logf_prompt = with_meta(mlf, plf)[None]
    shift_prompt = pxl.reshape(1, batch, D_MODEL)
    k_sample = heads(seqs(sk))[None]
    v_sample = heads(seqs(sv))[None]
    logf_sample = seqs(slf)[None]
    shift_sample = seqs(sxn)[:, -1][None]
    return (y_prompt, y_sample, k_prompt, v_prompt, logf_prompt, wkv_prompt[None], shift_prompt,
            k_sample, v_sample, logf_sample, wkv_sample[None], shift_sample)
```

```python
import functools

import jax
import jax.numpy as jnp
from jax import lax
from jax.experimental import pallas as pl
from jax.experimental.pallas import tpu as pltpu

D_MODEL = 1024
N_META = 16
HEAD_DIM = 64
N_HEADS = 8
WIDTH = N_HEADS * HEAD_DIM
PAGE_SIZE = 128
DECAY_LORA = 64
ICLR_LORA = 64
GATE_LORA = 128
RWKV_COLS = 3 * WIDTH + DECAY_LORA + ICLR_LORA + GATE_LORA
RMS_EPS = 1e-6
GN_EPS = 64e-5
NEG_BIG = -1e30

LANES = 128
BIAS_GROUP = 16
HEAD_SHIFT = HEAD_DIM.bit_length() - 1
GROUP_SHIFT = BIAS_GROUP.bit_length() - 1
VMEM_LIMIT = 56 * 1024 * 1024

ROW_TILE = 256
ATT_TILE = 512
ATT_K_TILE = 256
WKV_CHUNK = 64
PAGES_PER_STEP = 8
FF_CHUNK = 256

BF16 = jnp.bfloat16
F32 = jnp.float32
HIGHEST = lax.Precision.HIGHEST


def _dot(a, b, precision=None):
    return jnp.dot(a, b, preferred_element_type=F32, precision=precision)


def _dot_nt(a, b, precision=None):
    return lax.dot_general(a, b, (((1,), (1,)), ((), ())), preferred_element_type=F32,
                           precision=precision)


def _dot_tn(a, b, precision=None):
    return lax.dot_general(a, b, (((0,), (0,)), ((), ())), preferred_element_type=F32,
                           precision=precision)


def _rms(x, g):
    return x * lax.rsqrt(jnp.mean(x * x, axis=-1, keepdims=True) + RMS_EPS) * g


def _softplus(z):
    return jnp.maximum(z, 0.0) + jnp.log1p(jnp.exp(-jnp.abs(z)))


def _sigmoid(z):
    return 1.0 / (1.0 + jnp.exp(-z))


def _mask_bf16(cond):
    return cond.astype(F32).astype(BF16)


def _split3(c):
    hi = c.astype(BF16)
    r1 = c - hi.astype(F32)
    mid = r1.astype(BF16)
    lo = (r1 - mid.astype(F32)).astype(BF16)
    return hi, mid, lo


def _inproj_outputs(xn, qkv, logf, c, p, p_prev, w, outs):
    (mu_ref, sel_ref, ones_ref, w2_ref, a2_ref, g2_ref, w0_ref, a0_ref) = w
    (q_ref, k_ref, kb_ref, v_ref, vb_ref, lf_ref, cq_ref, ck_ref,
     r_ref, lw_ref, kr_ref, vr_ref, a_ref, g_ref) = outs
    q_ref[0] = (qkv[:, :WIDTH] * (HEAD_DIM ** -0.5)).astype(BF16)
    k = qkv[:, WIDTH:2 * WIDTH]
    k_ref[0] = k
    kb_ref[0] = k.astype(BF16)
    v = qkv[:, 2 * WIDTH:]
    v_ref[0] = v
    vb_ref[0] = v.astype(BF16)
    lf_ref[0] = logf[:, :N_HEADS]
    cparts = jnp.concatenate(_split3(c), axis=1)
    cc = _dot(cparts, sel_ref[...]) + ones_ref[...]
    cq_ref[0] = cc[:, :LANES].astype(BF16)
    ck_ref[0] = cc[:, LANES:].astype(BF16)
    rw = p + (p_prev - p) * mu_ref[...]
    r_ref[0] = rw[:, :WIDTH]
    kr_ref[0] = rw[:, WIDTH:2 * WIDTH]
    vr_ref[0] = rw[:, 2 * WIDTH:3 * WIDTH]
    z = rw[:, 3 * WIDTH:3 * WIDTH + LANES]
    gl = rw[:, 3 * WIDTH + LANES:]
    w_log = -_softplus(-(w0_ref[...] + _dot(jnp.tanh(z).astype(BF16), w2_ref[...]))) - 0.5
    lw_ref[0] = -jnp.exp(w_log)
    a_ref[0] = _sigmoid(a0_ref[...] + _dot(z.astype(BF16), a2_ref[...]))
    g_ref[0] = _dot(_sigmoid(gl).astype(BF16), g2_ref[...])


def _inproj_long_kernel(x_ref, prow_ref, c0_ref, ln_ref, wqkv_ref, wf_ref, bf_ref, wrw_ref,
                        mu_ref, sel_ref, ones_ref, w2_ref, a2_ref, g2_ref, w0_ref, a0_ref,
                        q_ref, k_ref, kb_ref, v_ref, vb_ref, lf_ref, cq_ref, ck_ref,
                        r_ref, lw_ref, kr_ref, vr_ref, a_ref, g_ref, xl_ref, pl_ref, cl_ref,
                        pcar_ref, ccar_ref):
    @pl.when(pl.program_id(1) == 0)
    def _():
        pcar_ref[...] = prow_ref[0]
        ccar_ref[...] = c0_ref[0]

    rows = x_ref.shape[1]
    xn = _rms(x_ref[0], ln_ref[...])
    xb = xn.astype(BF16)
    qkv = _dot(xb, wqkv_ref[...])
    logf = -_softplus(-(_dot(xb, wf_ref[...]) + bf_ref[...]))
    ri = lax.broadcasted_iota(jnp.int32, (rows, rows), 0)
    ci = lax.broadcasted_iota(jnp.int32, (rows, rows), 1)
    c = _dot((ri >= ci).astype(F32), logf, HIGHEST) + ccar_ref[...]
    ccar_ref[...] = c[rows - 1:rows, :]
    p = _dot(xb, wrw_ref[...])
    first = lax.broadcasted_iota(jnp.int32, p.shape, 0) == 0
    p_prev = jnp.where(first, pcar_ref[...], pltpu.roll(p, 1, 0))
    pcar_ref[...] = p[rows - 1:rows, :]
    xl_ref[0] = xn[rows - 1:rows, :]
    pl_ref[0] = p[rows - 1:rows, :]
    cl_ref[0] = c[rows - 1:rows, :]
    _inproj_outputs(xn, qkv, logf, c, p, p_prev,
                    (mu_ref, sel_ref, ones_ref, w2_ref, a2_ref, g2_ref, w0_ref, a0_ref),
                    (q_ref, k_ref, kb_ref, v_ref, vb_ref, lf_ref, cq_ref, ck_ref,
                     r_ref, lw_ref, kr_ref, vr_ref, a_ref, g_ref))


def _inproj_packed_kernel(x_ref, xprev_ref, ln_ref, wqkv_ref, wf_ref, bf_ref, wrw_ref,
                          mu_ref, sel_ref, ones_ref, w2_ref, a2_ref, g2_ref, w0_ref, a0_ref,
                          q_ref, k_ref, kb_ref, v_ref, vb_ref, lf_ref, cq_ref, ck_ref,
                          r_ref, lw_ref, kr_ref, vr_ref, a_ref, g_ref, xn_ref, *, seq_len):
    rows = x_ref.shape[1]
    shift = seq_len.bit_length() - 1
    xn = _rms(x_ref[0], ln_ref[...])
    xn_ref[0] = xn
    xb = xn.astype(BF16)
    qkv = _dot(xb, wqkv_ref[...])
    logf = -_softplus(-(_dot(xb, wf_ref[...]) + bf_ref[...]))
    ri = lax.broadcasted_iota(jnp.int32, (rows, rows), 0)
    ci = lax.broadcasted_iota(jnp.int32, (rows, rows), 1)
    same_seq = (ri >> shift) == (ci >> shift)
    c = _dot(((ri >= ci) & same_seq).astype(F32), logf, HIGHEST)
    p = _dot(xb, wrw_ref[...])
    p_first = _dot(xprev_ref[0].astype(BF16), wrw_ref[...])
    first = (lax.broadcasted_iota(jnp.int32, p.shape, 0) & (seq_len - 1)) == 0
    p_prev = jnp.where(first, p_first, pltpu.roll(p, 1, 0))
    _inproj_outputs(xn, qkv, logf, c, p, p_prev,
                    (mu_ref, sel_ref, ones_ref, w2_ref, a2_ref, g2_ref, w0_ref, a0_ref),
                    (q_ref, k_ref, kb_ref, v_ref, vb_ref, lf_ref, cq_ref, ck_ref,
                     r_ref, lw_ref, kr_ref, vr_ref, a_ref, g_ref))


def _const_spec(arr):
    return pl.BlockSpec(arr.shape, lambda *_: (0,) * arr.ndim)


def _inproj_out_shapes(batch, rows_total):
    def s(width, dtype):
        return jax.ShapeDtypeStruct((batch, rows_total, width), dtype)
    return [s(WIDTH, BF16), s(WIDTH, F32), s(WIDTH, BF16), s(WIDTH, F32), s(WIDTH, BF16),
            s(N_HEADS, F32), s(LANES, BF16), s(LANES, BF16)] + [s(WIDTH, F32)] * 6


def _inproj_out_specs(tile):
    def s(width):
        return pl.BlockSpec((1, tile, width), lambda b, i: (b, i, 0))
    return [s(WIDTH)] * 5 + [s(N_HEADS), s(LANES), s(LANES)] + [s(WIDTH)] * 6


def _inproj_long(x, prow, c0, weights, tile):
    batch, seq, _ = x.shape
    n_tiles = seq // tile
    row = lambda width: pl.BlockSpec((1, 1, width), lambda b, i: (b, 0, 0))
    shapes = _inproj_out_shapes(batch, seq) + [
        jax.ShapeDtypeStruct((batch, 1, D_MODEL), F32),
        jax.ShapeDtypeStruct((batch, 1, RWKV_COLS), F32),
        jax.ShapeDtypeStruct((batch, 1, LANES), F32)]
    specs = _inproj_out_specs(tile) + [row(D_MODEL), row(RWKV_COLS), row(LANES)]
    return pl.pallas_call(
        _inproj_long_kernel,
        grid=(batch, n_tiles),
        in_specs=[pl.BlockSpec((1, tile, D_MODEL), lambda b, i: (b, i, 0)),
                  _const_spec(prow), _const_spec(c0)] + [_const_spec(w) for w in weights],
        out_specs=specs,
        out_shape=shapes,
        scratch_shapes=[pltpu.VMEM((1, RWKV_COLS), F32), pltpu.VMEM((1, LANES), F32)],
        compiler_params=pltpu.CompilerParams(
            dimension_semantics=("arbitrary", "arbitrary"), vmem_limit_bytes=VMEM_LIMIT),
        name="inproj_long",
    )(x, prow, c0, *weights)


def _inproj_packed(x, xprev, weights, seq_len):
    _, rows, _ = x.shape
    full = pl.BlockSpec((1, rows, D_MODEL), lambda b, i: (0, 0, 0))
    return pl.pallas_call(
        functools.partial(_inproj_packed_kernel, seq_len=seq_len),
        grid=(1, 1),
        in_specs=[full, full] + [_const_spec(w) for w in weights],
        out_specs=_inproj_out_specs(rows) + [full],
        out_shape=_inproj_out_shapes(1, rows) + [jax.ShapeDtypeStruct((1, rows, D_MODEL), F32)],
        compiler_params=pltpu.CompilerParams(
            dimension_semantics=("arbitrary", "arbitrary"), vmem_limit_bytes=VMEM_LIMIT),
        name="inproj_packed",
    )(x, xprev, *weights)


def _fox_prompt_kernel(q_ref, cq_ref, km_ref, ckm_ref, vm_ref, k_ref, ck_ref, v_ref, o_ref):
    pair = pl.program_id(1)
    qi = pl.program_id(2)
    tile = q_ref.shape[1]
    lane = lax.broadcasted_iota(jnp.int32, (1, LANES), 1)
    q2 = q_ref[0]
    cq = cq_ref[0]
    lhs = []
    for hh in range(2):
        head_lanes = (lane >= hh * HEAD_DIM) & (lane < (hh + 1) * HEAD_DIM)
        g0 = (2 * pair + hh) * BIAS_GROUP
        group_lanes = (lane >= g0) & (lane < g0 + BIAS_GROUP)
        lhs.append(jnp.concatenate([q2 * _mask_bf16(head_lanes),
                                    cq * _mask_bf16(group_lanes)], axis=1))

    def block(carry, kk, vv, mask):
        out = []
        for hh in range(2):
            m, l, acc = carry[hh]
            s = _dot_nt(lhs[hh], kk)
            if mask is not None:
                s = jnp.where(mask, s, NEG_BIG)
            m_new = jnp.maximum(m, jnp.max(s, axis=-1, keepdims=True))
            alpha = jnp.exp(m - m_new)
            pe = jnp.exp(s - m_new)
            l_new = alpha * l + jnp.sum(pe, axis=-1, keepdims=True)
            acc_new = alpha * acc + _dot(pe.astype(BF16), vv)
            out.append((m_new, l_new, acc_new))
        return tuple(out)

    init = tuple((jnp.full((tile, 1), NEG_BIG, F32), jnp.zeros((tile, 1), F32),
                  jnp.zeros((tile, LANES), F32)) for _ in range(2))
    carry = block(init, jnp.concatenate([km_ref[0], ckm_ref[0]], axis=1), vm_ref[0], None)

    def keys(j):
        start = pl.multiple_of(j * ATT_K_TILE, ATT_K_TILE)
        rows = pl.ds(start, ATT_K_TILE)
        return jnp.concatenate([k_ref[0, rows, :], ck_ref[0, rows, :]], axis=1), v_ref[0, rows, :]

    per_q = tile // ATT_K_TILE
    carry = lax.fori_loop(0, qi * per_q, lambda j, c: block(c, *keys(j), None), carry)
    row_ge_col = (lax.broadcasted_iota(jnp.int32, (tile, ATT_K_TILE), 0)
                  - lax.broadcasted_iota(jnp.int32, (tile, ATT_K_TILE), 1))
    for d in range(per_q):
        carry = block(carry, *keys(qi * per_q + d), row_ge_col >= d * ATT_K_TILE)
    o0 = carry[0][2] / carry[0][1]
    o1 = carry[1][2] / carry[1][1]
    first_head = lax.broadcasted_iota(jnp.int32, o0.shape, 1) < HEAD_DIM
    o_ref[0] = jnp.where(first_head, o0, o1).astype(BF16)


def _fox_prompt(q, cq, k_meta, ck_meta, v_meta, k, ck, v):
    batch, seq, _ = q.shape
    n_pairs = N_HEADS // 2
    return pl.pallas_call(
        _fox_prompt_kernel,
        grid=(batch, n_pairs, seq // ATT_TILE),
        in_specs=[
            pl.BlockSpec((1, ATT_TILE, LANES), lambda b, p, i: (b, i, p)),
            pl.BlockSpec((1, ATT_TILE, LANES), lambda b, p, i: (b, i, 0)),
            pl.BlockSpec((1, N_META, LANES), lambda b, p, i: (0, 0, p)),
            pl.BlockSpec((1, N_META, LANES), lambda b, p, i: (0, 0, 0)),
            pl.BlockSpec((1, N_META, LANES), lambda b, p, i: (0, 0, p)),
            pl.BlockSpec((1, seq, LANES), lambda b, p, i: (b, 0, p)),
            pl.BlockSpec((1, seq, LANES), lambda b, p, i: (b, 0, 0)),
            pl.BlockSpec((1, seq, LANES), lambda b, p, i: (b, 0, p)),
        ],
        out_specs=pl.BlockSpec((1, ATT_TILE, LANES), lambda b, p, i: (b, i, p)),
        out_shape=jax.ShapeDtypeStruct((batch, seq, WIDTH), BF16),
        compiler_params=pltpu.CompilerParams(
            dimension_semantics=("arbitrary", "arbitrary", "arbitrary"),
            vmem_limit_bytes=VMEM_LIMIT),
        name="fox_prompt",
    )(q, cq, k_meta, ck_meta, v_meta, k, ck, v)


def _heads_to_rows(x):
    parts = []
    for pair in range(N_HEADS // 2):
        two = x[:, pair * LANES:(pair + 1) * LANES]
        parts.append(two[:, :HEAD_DIM])
        parts.append(pltpu.roll(two, HEAD_DIM, 1)[:, :HEAD_DIM])
    return jnp.concatenate(parts, axis=0)


def _rows_to_heads(x, n):
    return jnp.concatenate([x[h * n:(h + 1) * n, :] for h in range(N_HEADS)], axis=1)


def _fox_sample_kernel(pt_ref, q_ref, cq_ref, kn_ref, ckn_ref, vn_ref, *rest, n_new):
    del pt_ref
    g_pages = PAGES_PER_STEP
    k_refs, v_refs, lf_refs = rest[:g_pages], rest[g_pages:2 * g_pages], rest[2 * g_pages:3 * g_pages]
    o_ref, q_sc, cqm_sc, roff_sc, carry_sc, m_sc, l_sc, acc_sc = rest[3 * g_pages:]
    step = pl.program_id(1)
    rows = N_HEADS * n_new
    cols = PAGE_SIZE * N_HEADS
    shift = n_new.bit_length() - 1
    head_mask = N_HEADS - 1
    head_shift = N_HEADS.bit_length() - 1
    row_head = lax.broadcasted_iota(jnp.int32, (rows, 1), 0) >> shift

    @pl.when(step == 0)
    def _():
        q_sc[...] = _heads_to_rows(q_ref[0].astype(F32)).astype(BF16)
        cq = jnp.concatenate([cq_ref[0].astype(F32)] * N_HEADS, axis=0)
        lane = lax.broadcasted_iota(jnp.int32, (1, LANES), 1)
        cqm = jnp.where((lane >> GROUP_SHIFT) == row_head, cq, 0.0)
        cqm_sc[...] = cqm.astype(BF16)
        slot = lax.broadcasted_iota(jnp.int32, cqm.shape, 1) & (BIAS_GROUP - 1)
        roff_sc[...] = jnp.sum(jnp.where(slot < 3, cqm, 0.0), axis=-1, keepdims=True)
        carry_sc[...] = jnp.zeros(carry_sc.shape, F32)
        m_sc[...] = jnp.full(m_sc.shape, NEG_BIG, F32)
        l_sc[...] = jnp.zeros(l_sc.shape, F32)
        acc_sc[...] = jnp.zeros(acc_sc.shape, F32)

    q_rows = q_sc[...]
    li = lax.broadcasted_iota(jnp.int32, (LANES, LANES), 0)
    lj = lax.broadcasted_iota(jnp.int32, (LANES, LANES), 1)
    same_head = (li & head_mask) == (lj & head_mask)
    later_in_row = (same_head & ((li >> head_shift) > (lj >> head_shift))).astype(F32)
    row_total = same_head.astype(F32)
    lf_rows = cols // LANES
    later_rows = (lax.broadcasted_iota(jnp.int32, (lf_rows, lf_rows), 1)
                  > lax.broadcasted_iota(jnp.int32, (lf_rows, lf_rows), 0)).astype(F32)
    own_head = row_head == (lax.broadcasted_iota(jnp.int32, (1, cols), 1) & head_mask)

    carry = carry_sc[...]
    roff = roff_sc[...]
    scores = []
    for g in range(g_pages):
        z = lf_refs[g][0]
        tot = _dot(z, row_total, HIGHEST)
        r = _dot(z, later_in_row, HIGHEST) + _dot(later_rows, tot, HIGHEST) + carry
        carry = carry + jnp.sum(tot, axis=0, keepdims=True)
        bias = jnp.concatenate([r[a:a + 1, :] for a in range(lf_rows)], axis=1)
        k_flat = k_refs[g][0].reshape(cols, HEAD_DIM).astype(BF16)
        scores.append(jnp.where(own_head, _dot_nt(q_rows, k_flat) + bias + roff, NEG_BIG))
    carry_sc[...] = carry

    m = m_sc[...]
    m_new = m
    for s in scores:
        m_new = jnp.maximum(m_new, jnp.max(s, axis=-1, keepdims=True))
    alpha = jnp.exp(m - m_new)
    l_new = alpha * l_sc[...]
    acc = alpha * acc_sc[...]
    for g, s in enumerate(scores):
        pe = jnp.exp(s - m_new)
        l_new = l_new + jnp.sum(pe, axis=-1, keepdims=True)
        acc = acc + _dot(pe.astype(BF16), v_refs[g][0].reshape(cols, HEAD_DIM).astype(BF16))
    m_sc[...] = m_new
    l_sc[...] = l_new
    acc_sc[...] = acc

    @pl.when(step == pl.num_programs(1) - 1)
    def _():
        kn = _heads_to_rows(kn_ref[0].astype(F32)).astype(BF16)
        vn = _heads_to_rows(vn_ref[0].astype(F32)).astype(BF16)
        ckn = jnp.concatenate([ckn_ref[0].astype(F32)] * N_HEADS, axis=0).astype(BF16)
        s = _dot_nt(q_rows, kn) + _dot_nt(cqm_sc[...], ckn)
        row = lax.broadcasted_iota(jnp.int32, (rows, 1), 0)
        col = lax.broadcasted_iota(jnp.int32, (1, rows), 1)
        keep = (row_head == (col >> shift)) & ((col & (n_new - 1)) <= (row & (n_new - 1)))
        s = jnp.where(keep, s, NEG_BIG)
        m_fin = jnp.maximum(m_new, jnp.max(s, axis=-1, keepdims=True))
        a_fin = jnp.exp(m_new - m_fin)
        pe = jnp.exp(s - m_fin)
        l_fin = a_fin * l_new + jnp.sum(pe, axis=-1, keepdims=True)
        out = (a_fin * acc + _dot(pe.astype(BF16), vn)) / l_fin
        o_ref[0] = _rows_to_heads(out, n_new).astype(BF16)


def _fox_sample(page_table, q, cq, k_new, ck_new, v_new, cache_k, cache_v, cache_logf):
    n_seq, n_new, _ = q.shape
    n_pages = page_table.shape[1]
    g_pages = PAGES_PER_STEP
    rows = N_HEADS * n_new

    def per_seq(width):
        return pl.BlockSpec((1, n_new, width), lambda b, j, pt: (b, 0, 0))

    def page_spec(g, arr):
        block = (1,) + arr.shape[1:]
        return pl.BlockSpec(block, lambda b, j, pt: (pt[b, n_pages - 1 - (j * g_pages + g)],)
                            + (0,) * (arr.ndim - 1))

    grid_spec = pltpu.PrefetchScalarGridSpec(
        num_scalar_prefetch=1,
        grid=(n_seq, n_pages // g_pages),
        in_specs=([per_seq(WIDTH), per_seq(LANES), per_seq(WIDTH), per_seq(LANES), per_seq(WIDTH)]
                  + [page_spec(g, cache_k) for g in range(g_pages)]
                  + [page_spec(g, cache_v) for g in range(g_pages)]
                  + [page_spec(g, cache_logf) for g in range(g_pages)]),
        out_specs=pl.BlockSpec((1, n_new, WIDTH), lambda b, j, pt: (b, 0, 0)),
        scratch_shapes=[pltpu.VMEM((rows, HEAD_DIM), BF16), pltpu.VMEM((rows, LANES), BF16),
                        pltpu.VMEM((rows, 1), F32), pltpu.VMEM((1, LANES), F32),
                        pltpu.VMEM((rows, 1), F32), pltpu.VMEM((rows, 1), F32),
                        pltpu.VMEM((rows, HEAD_DIM), F32)],
    )
    return pl.pallas_call(
        functools.partial(_fox_sample_kernel, n_new=n_new),
        grid_spec=grid_spec,
        out_shape=jax.ShapeDtypeStruct((n_seq, n_new, WIDTH), BF16),
        compiler_params=pltpu.CompilerParams(
            dimension_semantics=("arbitrary", "arbitrary"), vmem_limit_bytes=VMEM_LIMIT),
        name="fox_sample",
    )(page_table, q, cq, k_new, ck_new, v_new,
      *([cache_k] * g_pages), *([cache_v] * g_pages), *([cache_logf] * g_pages))


def _wkv_kernel(r_ref, lw_ref, kr_ref, vr_ref, a_ref, g_ref, s0_ref,
                kk_ref, ka_ref, rk_ref, gg_ref, gb_ref, o_ref, so_ref, s_sc):
    @pl.when(pl.program_id(1) == 0)
    def _():
        s_sc[...] = s0_ref[0]

    C = r_ref.shape[1]
    ri = lax.broadcasted_iota(jnp.int32, (C, C), 0)
    ci = lax.broadcasted_iota(jnp.int32, (C, C), 1)
    incl = ri >= ci
    strict = ri > ci
    eye = (ri == ci).astype(F32)
    lcum = _dot(incl.astype(F32), lw_ref[0], HIGHEST)
    levels = max(1, (C - 1).bit_length())

    heads = range(N_HEADS)
    hd = []
    for pair in range(N_HEADS // 2):
        lanes = slice(pair * LANES, (pair + 1) * LANES)
        blocks = [ref[0, :, lanes] for ref in (r_ref, lw_ref, kr_ref, vr_ref, a_ref, g_ref)]
        blocks.append(lcum[:, lanes])
        hd.append([x[:, :HEAD_DIM] for x in blocks])
        hd.append([pltpu.roll(x, HEAD_DIM, 1)[:, :HEAD_DIM] for x in blocks])

    xs, bts, kts, bgs, kgs, vbs, kps, e_ends = [], [], [], [], [], [], [], []
    for h in heads:
        rh, lwh, krh, vh, ah, gh, lc = hd[h]
        kkr = krh * kk_ref[h:h + 1, :]
        norm = jnp.sqrt(jnp.sum(kkr * kkr, axis=-1, keepdims=True))
        kk = kkr / jnp.maximum(norm, 1e-12)
        kp = krh * (1.0 + (ah - 1.0) * ka_ref[h:h + 1, :])
        bv = kk * ah
        e_neg = jnp.exp(-lc)
        l_end = lc[C - 1:C, :]
        e_rem = jnp.exp(l_end - lc)
        rt = rh * jnp.exp(lc)
        at = -kk * jnp.exp(lc - lwh)
        xs.append(jnp.concatenate([at, rt], axis=0).astype(BF16))
        bts.append((bv * e_neg).astype(BF16))
        kts.append((kp * e_neg).astype(BF16))
        bgs.append((bv * e_rem).astype(BF16))
        kgs.append((kp * e_rem).astype(BF16))
        vbs.append(vh.astype(BF16))
        kps.append(kp)
        e_ends.append(jnp.exp(l_end))

    sc_b = [_dot_nt(xs[h], bts[h]) for h in heads]
    sc_k = [_dot_nt(xs[h], kts[h]) for h in heads]
    a_ab = [jnp.where(strict, sc_b[h][:C], 0.0) for h in heads]
    a_ak = [jnp.where(strict, sc_k[h][:C], 0.0).astype(BF16) for h in heads]
    a_rb = [jnp.where(incl, sc_b[h][C:], 0.0).astype(BF16) for h in heads]
    a_rk = [jnp.where(incl, sc_k[h][C:], 0.0).astype(BF16) for h in heads]

    tinv = [eye + a_ab[h] for h in heads]
    if levels > 1:
        lb = [a_ab[h].astype(BF16) for h in heads]
        lpow = [_dot(lb[h], lb[h]) for h in heads]
        for level in range(1, levels):
            qb = [lpow[h].astype(BF16) for h in heads]
            if level == levels - 1:
                tinv = [tinv[h] + _dot(tinv[h].astype(BF16), qb[h]) for h in heads]
            else:
                both = [_dot(jnp.concatenate([tinv[h], lpow[h]], axis=0).astype(BF16), qb[h])
                        for h in heads]
                tinv = [tinv[h] + both[h][:C] for h in heads]
                lpow = [both[h][C:] for h in heads]
    tb = [tinv[h].astype(BF16) for h in heads]

    s_old = [s_sc[h] for h in heads]
    ps = [_dot_nt(xs[h], s_old[h].astype(BF16)) for h in heads]
    w1 = [(ps[h][:C] + _dot(a_ak[h], vbs[h])).astype(BF16) for h in heads]
    ub = [_dot(tb[h], w1[h]).astype(BF16) for h in heads]
    ys = [ps[h][C:] + _dot(a_rb[h], ub[h]) + _dot(a_rk[h], vbs[h]) for h in heads]
    for h in heads:
        s_sc[h] = s_old[h] * e_ends[h] + _dot_tn(ub[h], bgs[h]) + _dot_tn(vbs[h], kgs[h])

    outs = []
    for h in heads:
        rh, _, _, vh, _, gh, _ = hd[h]
        y = ys[h]
        mean = jnp.mean(y, axis=-1, keepdims=True)
        yc = y - mean
        var = jnp.mean(yc * yc, axis=-1, keepdims=True)
        yn = yc * lax.rsqrt(var + GN_EPS) * gg_ref[h:h + 1, :] + gb_ref[h:h + 1, :]
        bonus = jnp.sum(rh * kps[h] * rk_ref[h:h + 1, :], axis=-1, keepdims=True) * vh
        outs.append((yn + bonus) * gh)
    for pair in range(N_HEADS // 2):
        o_ref[0, :, pair * LANES:(pair + 1) * LANES] = jnp.concatenate(
            outs[2 * pair:2 * pair + 2], axis=1).astype(BF16)

    @pl.when(pl.program_id(1) == pl.num_programs(1) - 1)
    def _():
        so_ref[0] = s_sc[...]


def _wkv(r, lw, kr, vr, a, g, s0, params, chunk):
    batch, seq, _ = r.shape
    shared_state = s0.shape[0] == 1
    tok = pl.BlockSpec((1, chunk, WIDTH), lambda b, c: (b, c, 0))
    state_in = pl.BlockSpec((1, N_HEADS, HEAD_DIM, HEAD_DIM),
                            (lambda b, c: (0, 0, 0, 0)) if shared_state else (lambda b, c: (b, 0, 0, 0)))
    state_out = pl.BlockSpec((1, N_HEADS, HEAD_DIM, HEAD_DIM), lambda b, c: (b, 0, 0, 0))
    return pl.pallas_call(
        _wkv_kernel,
        grid=(batch, seq // chunk),
        in_specs=[tok] * 6 + [state_in] + [_const_spec(p) for p in params],
        out_specs=[tok, state_out],
        out_shape=[jax.ShapeDtypeStruct((batch, seq, WIDTH), BF16),
                   jax.ShapeDtypeStruct((batch, N_HEADS, HEAD_DIM, HEAD_DIM), F32)],
        scratch_shapes=[pltpu.VMEM((N_HEADS, HEAD_DIM, HEAD_DIM), F32)],
        compiler_params=pltpu.CompilerParams(
            dimension_semantics=("arbitrary", "arbitrary"), vmem_limit_bytes=VMEM_LIMIT),
        name="wkv",
    )(r, lw, kr, vr, a, g, s0, *params)


def _out_ffn_kernel(x_ref, att_ref, rw_ref, woa_ref, wor_ref, ln2_ref, wg_ref, wu_ref, wd_ref,
                    lnf_ref, y_ref):
    h = x_ref[...] + _dot(att_ref[...], woa_ref[...]) + _dot(rw_ref[...], wor_ref[...])
    hb = _rms(h, ln2_ref[...]).astype(BF16)
    ffn = jnp.zeros(h.shape, F32)
    d_ff = wg_ref.shape[1]
    for c0 in range(0, d_ff, FF_CHUNK):
        gate = _dot(hb, wg_ref[:, c0:c0 + FF_CHUNK])
        up = _dot(hb, wu_ref[:, c0:c0 + FF_CHUNK])
        act = gate * _sigmoid(gate) * up
        ffn = ffn + _dot(act.astype(BF16), wd_ref[c0:c0 + FF_CHUNK, :])
    y_ref[...] = _rms(h + ffn, lnf_ref[...])


def _out_ffn(x, att, rw, weights):
    rows = x.shape[0]
    tile = min(ROW_TILE, rows)
    tok = lambda width: pl.BlockSpec((tile, width), lambda i: (i, 0))
    return pl.pallas_call(
        _out_ffn_kernel,
        grid=(rows // tile,),
        in_specs=[tok(D_MODEL), tok(WIDTH), tok(WIDTH)] + [_const_spec(w) for w in weights],
        out_specs=tok(D_MODEL),
        out_shape=jax.ShapeDtypeStruct((rows, D_MODEL), F32),
        compiler_params=pltpu.CompilerParams(
            dimension_semantics=("arbitrary",), vmem_limit_bytes=VMEM_LIMIT),
        name="out_ffn",
    )(x, att, rw, *weights)


def _bias_selectors():
    part = jnp.arange(3)[:, None, None]
    src = jnp.arange(LANES)[None, :, None]
    dst = jnp.arange(LANES)[None, None, :]
    is_head = src < N_HEADS
    q_sel = (is_head & (dst == src * BIAS_GROUP + part)).astype(F32)
    k_sel = -(is_head & (dst == src * BIAS_GROUP + 3 + part)).astype(F32)
    sel = jnp.concatenate([q_sel, k_sel], axis=2).reshape(3 * LANES, 2 * LANES)
    slot = jnp.arange(LANES) % BIAS_GROUP
    ones = jnp.concatenate([(slot >= 3) & (slot < 6), slot < 3]).astype(F32)[None, :]
    return sel.astype(BF16), ones


def kernel(x_prompt, x_sample, cache_k, cache_v, cache_logf, state_wkv, state_shift, page_table,
           meta_tokens, ln1_g, w_in, b_f, mu_shift, w0, w2, a0, a2, g2, k_k, k_a, r_k,
           gn_g, gn_b, w_out, ln2_g, w_gate, w_up, w_down, lnf_g):
    batch, seq, _ = x_prompt.shape
    n_seq, n_new, _ = x_sample.shape
    n_pool = cache_k.shape[1]
    fox_cols = 3 * WIDTH + N_HEADS

    w_in0 = w_in[0]
    wqkv = w_in0[:, :3 * WIDTH].astype(BF16)
    wf = jnp.pad(w_in0[:, 3 * WIDTH:fox_cols], ((0, 0), (0, LANES - N_HEADS))).astype(BF16)
    bf = jnp.pad(b_f[0], (0, LANES - N_HEADS))[None, :]
    wrw = w_in0[:, fox_cols:].astype(BF16)
    sel, ones = _bias_selectors()
    w2p = jnp.concatenate([w2[0], jnp.zeros((ICLR_LORA, WIDTH), F32)], axis=0).astype(BF16)
    a2p = jnp.concatenate([jnp.zeros((DECAY_LORA, WIDTH), F32), a2[0]], axis=0).astype(BF16)
    inproj_w = (ln1_g[0][None, :], wqkv, wf, bf, wrw, mu_shift[0][None, :], sel, ones,
                w2p, a2p, g2[0].astype(BF16), w0[0][None, :], a0[0][None, :])
    per_head = lambda t: t.reshape(N_HEADS, HEAD_DIM)
    wkv_p = (per_head(k_k[0]), per_head(k_a[0]), r_k[0], per_head(gn_g[0]), per_head(gn_b[0]))
    ffn_w = (w_out[0][:WIDTH].astype(BF16), w_out[0][WIDTH:].astype(BF16), ln2_g[0][None, :],
             w_gate[0].astype(BF16), w_up[0].astype(BF16), w_down[0].astype(BF16), lnf_g[None, :])

    zrow = jnp.zeros((1, 1, RWKV_COLS), F32)
    zc = jnp.zeros((1, 1, LANES), F32)
    mo = _inproj_long(meta_tokens[None], zrow, zc, inproj_w, N_META)
    (_, mk, mkb, mv, mvb, mlf, _, mck, mr, mlw, mkr, mvr, ma, mg, _, mprow, mclast) = mo
    zstate = jnp.zeros((1, N_HEADS, HEAD_DIM, HEAD_DIM), F32)
    _, s_meta = _wkv(mr, mlw, mkr, mvr, ma, mg, zstate, wkv_p, N_META)

    po = _inproj_long(x_prompt, mprow, mclast, inproj_w, ROW_TILE)
    (pq, pk, pkb, pv, pvb, plf, pcq, pck, pr, plw, pkr, pvr, pa, pg, pxl, _, _) = po
    att_p = _fox_prompt(pq, pcq, mkb, mck, mvb, pkb, pck, pvb)
    rw_p, wkv_prompt = _wkv(pr, plw, pkr, pvr, pa, pg, s_meta, wkv_p, WKV_CHUNK)
    y_prompt = _out_ffn(x_prompt.reshape(batch * seq, D_MODEL), att_p.reshape(batch * seq, WIDTH),
                        rw_p.reshape(batch * seq, WIDTH), ffn_w).reshape(batch, seq, D_MODEL)

    rows_s = n_seq * n_new
    xprev = jnp.repeat(state_shift[0], n_new, axis=0)[None]
    so = _inproj_packed(x_sample.reshape(1, rows_s, D_MODEL), xprev, inproj_w, n_new)
    (sq, sk, skb, sv, svb, slf, scq, sck, sr, slw, skr, svr, sa, sg, sxn) = so
    seqs = lambda t: t.reshape(n_seq, n_new, t.shape[-1])
    att_s = _fox_sample(page_table, seqs(sq), seqs(scq), seqs(skb), seqs(sck), seqs(svb),
                        cache_k[0], cache_v[0],
                        cache_logf[0].reshape(n_pool, PAGE_SIZE * N_HEADS // LANES, LANES))
    rw_s, wkv_sample = _wkv(seqs(sr), seqs(slw), seqs(skr), seqs(svr), seqs(sa), seqs(sg),
                            state_wkv[0], wkv_p, n_new)
    y_sample = _out_ffn(x_sample.reshape(rows_s, D_MODEL), att_s.reshape(rows_s, WIDTH),
                        rw_s.reshape(rows_s, WIDTH), ffn_w).reshape(n_seq, n_new, D_MODEL)

    def with_meta(meta, main):
        return jnp.concatenate([jnp.broadcast_to(meta, (batch,) + meta.shape[1:]), main], axis=1)

    heads = lambda t: t.reshape(t.shape[:-1] + (N_HEADS, HEAD_DIM))
    k_prompt = heads(with_meta(mk, pk))[None]
    v_prompt = heads(with_meta(mv, pv))[None]
    logf_prompt = with_meta(mlf, plf)[None]
    shift_prompt = pxl.reshape(1, batch, D_MODEL)
    k_sample = heads(seqs(sk))[None]
    v_sample = heads(seqs(sv))[None]
    logf_sample = seqs(slf)[None]
    shift_sample = seqs(sxn)[:, -1][None]
    return (y_prompt, y_sample, k_prompt, v_prompt, logf_prompt, wkv_prompt[None], shift_prompt,
            k_sample, v_sample, logf_sample, wkv_sample[None], shift_sample)
```

```python
import functools

import jax
import jax.numpy as jnp
from jax import lax
from jax.experimental import pallas as pl
from jax.experimental.pallas import tpu as pltpu

D_MODEL = 1024
N_META = 16
HEAD_DIM = 64
N_HEADS = 8
WIDTH = N_HEADS * HEAD_DIM
PAGE_SIZE = 128
DECAY_LORA = 64
ICLR_LORA = 64
GATE_LORA = 128
RWKV_COLS = 3 * WIDTH + DECAY_LORA + ICLR_LORA + GATE_LORA
RMS_EPS = 1e-6
GN_EPS = 64e-5
NEG_BIG = -1e30

LANES = 128
BIAS_GROUP = 16
HEAD_SHIFT = HEAD_DIM.bit_length() - 1
GROUP_SHIFT = BIAS_GROUP.bit_length() - 1
VMEM_LIMIT = 56 * 1024 * 1024

ROW_TILE = 256
ATT_TILE = 512
ATT_K_TILE = 256
WKV_CHUNK = 64
PAGES_PER_STEP = 8
FF_CHUNK = 256

BF16 = jnp.bfloat16
F32 = jnp.float32
HIGHEST = lax.Precision.HIGHEST


def _dot(a, b, precision=None):
    return jnp.dot(a, b, preferred_element_type=F32, precision=precision)


def _dot_nt(a, b, precision=None):
    return lax.dot_general(a, b, (((1,), (1,)), ((), ())), preferred_element_type=F32,
                           precision=precision)


def _dot_tn(a, b, precision=None):
    return lax.dot_general(a, b, (((0,), (0,)), ((), ())), preferred_element_type=F32,
                           precision=precision)


def _rms(x, g):
    return x * lax.rsqrt(jnp.mean(x * x, axis=-1, keepdims=True) + RMS_EPS) * g


def _softplus(z):
    return jnp.maximum(z, 0.0) + jnp.log1p(jnp.exp(-jnp.abs(z)))


def _sigmoid(z):
    return 1.0 / (1.0 + jnp.exp(-z))


def _mask_bf16(cond):
    return cond.astype(F32).astype(BF16)


def _split3(c):
    hi = c.astype(BF16)
    r1 = c - hi.astype(F32)
    mid = r1.astype(BF16)
    lo = (r1 - mid.astype(F32)).astype(BF16)
    return hi, mid, lo


def _inproj_outputs(xn, qkv, logf, c, p, p_prev, w, outs):
    (mu_ref, sel_ref, ones_ref, w2_ref, a2_ref, g2_ref, w0_ref, a0_ref) = w
    (q_ref, k_ref, kb_ref, v_ref, vb_ref, lf_ref, cq_ref, ck_ref,
     r_ref, lw_ref, kr_ref, vr_ref, a_ref, g_ref) = outs
    q_ref[0] = (qkv[:, :WIDTH] * (HEAD_DIM ** -0.5)).astype(BF16)
    k = qkv[:, WIDTH:2 * WIDTH]
    k_ref[0] = k
    kb_ref[0] = k.astype(BF16)
    v = qkv[:, 2 * WIDTH:]
    v_ref[0] = v
    vb_ref[0] = v.astype(BF16)
    lf_ref[0] = logf[:, :N_HEADS]
    cparts = jnp.concatenate(_split3(c), axis=1)
    cc = _dot(cparts, sel_ref[...]) + ones_ref[...]
    cq_ref[0] = cc[:, :LANES].astype(BF16)
    ck_ref[0] = cc[:, LANES:].astype(BF16)
    rw = p + (p_prev - p) * mu_ref[...]
    r_ref[0] = rw[:, :WIDTH]
    kr_ref[0] = rw[:, WIDTH:2 * WIDTH]
    vr_ref[0] = rw[:, 2 * WIDTH:3 * WIDTH]
    z = rw[:, 3 * WIDTH:3 * WIDTH + LANES]
    gl = rw[:, 3 * WIDTH + LANES:]
    w_log = -_softplus(-(w0_ref[...] + _dot(jnp.tanh(z).astype(BF16), w2_ref[...]))) - 0.5
    lw_ref[0] = -jnp.exp(w_log)
    a_ref[0] = _sigmoid(a0_ref[...] + _dot(z.astype(BF16), a2_ref[...]))
    g_ref[0] = _dot(_sigmoid(gl).astype(BF16), g2_ref[...])


def _inproj_long_kernel(x_ref, prow_ref, c0_ref, ln_ref, wqkv_ref, wf_ref, bf_ref, wrw_ref,
                        mu_ref, sel_ref, ones_ref, w2_ref, a2_ref, g2_ref, w0_ref, a0_ref,
                        q_ref, k_ref, kb_ref, v_ref, vb_ref, lf_ref, cq_ref, ck_ref,
                        r_ref, lw_ref, kr_ref, vr_ref, a_ref, g_ref, xl_ref, pl_ref, cl_ref,
                        pcar_ref, ccar_ref):
    @pl.when(pl.program_id(1) == 0)
    def _():
        pcar_ref[...] = prow_ref[0]
        ccar_ref[...] = c0_ref[0]

    rows = x_ref.shape[1]
    xn = _rms(x_ref[0], ln_ref[...])
    xb = xn.astype(BF16)
    qkv = _dot(xb, wqkv_ref[...])
    logf = -_softplus(-(_dot(xb, wf_ref[...]) + bf_ref[...]))
    ri = lax.broadcasted_iota(jnp.int32, (rows, rows), 0)
    ci = lax.broadcasted_iota(jnp.int32, (rows, rows), 1)
    c = _dot((ri >= ci).astype(F32), logf, HIGHEST) + ccar_ref[...]
    ccar_ref[...] = c[rows - 1:rows, :]
    p = _dot(xb, wrw_ref[...])
    first = lax.broadcasted_iota(jnp.int32, p.shape, 0) == 0
    p_prev = jnp.where(first, pcar_ref[...], pltpu.roll(p, 1, 0))
    pcar_ref[...] = p[rows - 1:rows, :]
    xl_ref[0] = xn[rows - 1:rows, :]
    pl_ref[0] = p[rows - 1:rows, :]
    cl_ref[0] = c[rows - 1:rows, :]
    _inproj_outputs(xn, qkv, logf, c, p, p_prev,
                    (mu_ref, sel_ref, ones_ref, w2_ref, a2_ref, g2_ref, w0_ref, a0_ref),
                    (q_ref, k_ref, kb_ref, v_ref, vb_ref, lf_ref, cq_ref, ck_ref,
                     r_ref, lw_ref, kr_ref, vr_ref, a_ref, g_ref))


def _inproj_packed_kernel(x_ref, xprev_ref, ln_ref, wqkv_ref, wf_ref, bf_ref, wrw_ref,
                          mu_ref, sel_ref, ones_ref, w2_ref, a2_ref, g2_ref, w0_ref, a0_ref,
                          q_ref, k_ref, kb_ref, v_ref, vb_ref, lf_ref, cq_ref, ck_ref,
                          r_ref, lw_ref, kr_ref, vr_ref, a_ref, g_ref, xn_ref, *, seq_len):
    rows = x_ref.shape[1]
    shift = seq_len.bit_length() - 1
    xn = _rms(x_ref[0], ln_ref[...])
    xn_ref[0] = xn
    xb = xn.astype(BF16)
    qkv = _dot(xb, wqkv_ref[...])
    logf = -_softplus(-(_dot(xb, wf_ref[...]) + bf_ref[...]))
    ri = lax.broadcasted_iota(jnp.int32, (rows, rows), 0)
    ci = lax.broadcasted_iota(jnp.int32, (rows, rows), 1)
    same_seq = (ri >> shift) == (ci >> shift)
    c = _dot(((ri >= ci) & same_seq).astype(F32), logf, HIGHEST)
    p = _dot(xb, wrw_ref[...])
    p_first = _dot(xprev_ref[0].astype(BF16), wrw_ref[...])
    first = (lax.broadcasted_iota(jnp.int32, p.shape, 0) & (seq_len - 1)) == 0
    p_prev = jnp.where(first, p_first, pltpu.roll(p, 1, 0))
    _inproj_outputs(xn, qkv, logf, c, p, p_prev,
                    (mu_ref, sel_ref, ones_ref, w2_ref, a2_ref, g2_ref, w0_ref, a0_ref),
                    (q_ref, k_ref, kb_ref, v_ref, vb_ref, lf_ref, cq_ref, ck_ref,
                     r_ref, lw_ref, kr_ref, vr_ref, a_ref, g_ref))


def _const_spec(arr):
    return pl.BlockSpec(arr.shape, lambda *_: (0,) * arr.ndim)


def _inproj_out_shapes(batch, rows_total):
    def s(width, dtype):
        return jax.ShapeDtypeStruct((batch, rows_total, width), dtype)
    return [s(WIDTH, BF16), s(WIDTH, F32), s(WIDTH, BF16), s(WIDTH, F32), s(WIDTH, BF16),
            s(N_HEADS, F32), s(LANES, BF16), s(LANES, BF16)] + [s(WIDTH, F32)] * 6


def _inproj_out_specs(tile):
    def s(width):
        return pl.BlockSpec((1, tile, width), lambda b, i: (b, i, 0))
    return [s(WIDTH)] * 5 + [s(N_HEADS), s(LANES), s(LANES)] + [s(WIDTH)] * 6


def _inproj_long(x, prow, c0, weights, tile):
    batch, seq, _ = x.shape
    n_tiles = seq // tile
    row = lambda width: pl.BlockSpec((1, 1, width), lambda b, i: (b, 0, 0))
    shapes = _inproj_out_shapes(batch, seq) + [
        jax.ShapeDtypeStruct((batch, 1, D_MODEL), F32),
        jax.ShapeDtypeStruct((batch, 1, RWKV_COLS), F32),
        jax.ShapeDtypeStruct((batch, 1, LANES), F32)]
    specs = _inproj_out_specs(tile) + [row(D_MODEL), row(RWKV_COLS), row(LANES)]
    return pl.pallas_call(
        _inproj_long_kernel,
        grid=(batch, n_tiles),
        in_specs=[pl.BlockSpec((1, tile, D_MODEL), lambda b, i: (b, i, 0)),
                  _const_spec(prow), _const_spec(c0)] + [_const_spec(w) for w in weights],
        out_specs=specs,
        out_shape=shapes,
        scratch_shapes=[pltpu.VMEM((1, RWKV_COLS), F32), pltpu.VMEM((1, LANES), F32)],
        compiler_params=pltpu.CompilerParams(
            dimension_semantics=("arbitrary", "arbitrary"), vmem_limit_bytes=VMEM_LIMIT),
        name="inproj_long",
    )(x, prow, c0, *weights)


def _inproj_packed(x, xprev, weights, seq_len):
    _, rows, _ = x.shape
    full = pl.BlockSpec((1, rows, D_MODEL), lambda b, i: (0, 0, 0))
    return pl.pallas_call(
        functools.partial(_inproj_packed_kernel, seq_len=seq_len),
        grid=(1, 1),
        in_specs=[full, full] + [_const_spec(w) for w in weights],
        out_specs=_inproj_out_specs(rows) + [full],
        out_shape=_inproj_out_shapes(1, rows) + [jax.ShapeDtypeStruct((1, rows, D_MODEL), F32)],
        compiler_params=pltpu.CompilerParams(
            dimension_semantics=("arbitrary", "arbitrary"), vmem_limit_bytes=VMEM_LIMIT),
        name="inproj_packed",
    )(x, xprev, *weights)


def _fox_prompt_kernel(q_ref, cq_ref, km_ref, ckm_ref, vm_ref, k_ref, ck_ref, v_ref, o_ref):
    pair = pl.program_id(1)
    qi = pl.program_id(2)
    tile = q_ref.shape[1]
    lane = lax.broadcasted_iota(jnp.int32, (1, LANES), 1)
    q2 = q_ref[0]
    cq = cq_ref[0]
    lhs = []
    for hh in range(2):
        head_lanes = (lane >= hh * HEAD_DIM) & (lane < (hh + 1) * HEAD_DIM)
        g0 = (2 * pair + hh) * BIAS_GROUP
        group_lanes = (lane >= g0) & (lane < g0 + BIAS_GROUP)
        lhs.append(jnp.concatenate([q2 * _mask_bf16(head_lanes),
                                    cq * _mask_bf16(group_lanes)], axis=1))

    def block(carry, kk, vv, mask):
        out = []
        for hh in range(2):
            m, l, acc = carry[hh]
            s = _dot_nt(lhs[hh], kk)
            if mask is not None:
                s = jnp.where(mask, s, NEG_BIG)
            m_new = jnp.maximum(m, jnp.max(s, axis=-1, keepdims=True))
            alpha = jnp.exp(m - m_new)
            pe = jnp.exp(s - m_new)
            l_new = alpha * l + jnp.sum(pe, axis=-1, keepdims=True)
            acc_new = alpha * acc + _dot(pe.astype(BF16), vv)
            out.append((m_new, l_new, acc_new))
        return tuple(out)

    init = tuple((jnp.full((tile, 1), NEG_BIG, F32), jnp.zeros((tile, 1), F32),
                  jnp.zeros((tile, LANES), F32)) for _ in range(2))
    carry = block(init, jnp.concatenate([km_ref[0], ckm_ref[0]], axis=1), vm_ref[0], None)

    def keys(j):
        start = pl.multiple_of(j * ATT_K_TILE, ATT_K_TILE)
        rows = pl.ds(start, ATT_K_TILE)
        return jnp.concatenate([k_ref[0, rows, :], ck_ref[0, rows, :]], axis=1), v_ref[0, rows, :]

    per_q = tile // ATT_K_TILE
    carry = lax.fori_loop(0, qi * per_q, lambda j, c: block(c, *keys(j), None), carry)
    row_ge_col = (lax.broadcasted_iota(jnp.int32, (tile, ATT_K_TILE), 0)
                  - lax.broadcasted_iota(jnp.int32, (tile, ATT_K_TILE), 1))
    for d in range(per_q):
        carry = block(carry, *keys(qi * per_q + d), row_ge_col >= d * ATT_K_TILE)
    o0 = carry[0][2] / carry[0][1]
    o1 = carry[1][2] / carry[1][1]
    first_head = lax.broadcasted_iota(jnp.int32, o0.shape, 1) < HEAD_DIM
    o_ref[0] = jnp.where(first_head, o0, o1).astype(BF16)


def _fox_prompt(q, cq, k_meta, ck_meta, v_meta, k, ck, v):
    batch, seq, _ = q.shape
    n_pairs = N_HEADS // 2
    return pl.pallas_call(
        _fox_prompt_kernel,
        grid=(batch, n_pairs, seq // ATT_TILE),
        in_specs=[
            pl.BlockSpec((1, ATT_TILE, LANES), lambda b, p, i: (b, i, p)),
            pl.BlockSpec((1, ATT_TILE, LANES), lambda b, p, i: (b, i, 0)),
            pl.BlockSpec((1, N_META, LANES), lambda b, p, i: (0, 0, p)),
            pl.BlockSpec((1, N_META, LANES), lambda b, p, i: (0, 0, 0)),
            pl.BlockSpec((1, N_META, LANES), lambda b, p, i: (0, 0, p)),
            pl.BlockSpec((1, seq, LANES), lambda b, p, i: (b, 0, p)),
            pl.BlockSpec((1, seq, LANES), lambda b, p, i: (b, 0, 0)),
            pl.BlockSpec((1, seq, LANES), lambda b, p, i: (b, 0, p)),
        ],
        out_specs=pl.BlockSpec((1, ATT_TILE, LANES), lambda b, p, i: (b, i, p)),
        out_shape=jax.ShapeDtypeStruct((batch, seq, WIDTH), BF16),
        compiler_params=pltpu.CompilerParams(
            dimension_semantics=("arbitrary", "arbitrary", "arbitrary"),
            vmem_limit_bytes=VMEM_LIMIT),
        name="fox_prompt",
    )(q, cq, k_meta, ck_meta, v_meta, k, ck, v)


def _fox_sample_kernel(pt_ref, q_ref, cq_ref, kn_ref, ckn_ref, vn_ref, *rest, n_new):
    del pt_ref
    g_pages = PAGES_PER_STEP
    k_refs, v_refs, lf_refs = rest[:g_pages], rest[g_pages:2 * g_pages], rest[2 * g_pages:3 * g_pages]
    o_ref, q_sc, cqm_sc, roff_sc, carry_sc, m_sc, l_sc, acc_sc = rest[3 * g_pages:]
    step = pl.program_id(1)
    rows = N_HEADS * n_new
    shift = n_new.bit_length() - 1
    row_head = lax.broadcasted_iota(jnp.int32, (rows, 1), 0) >> shift

    @pl.when(step == 0)
    def _():
        q = jnp.concatenate([q_ref[0].astype(F32)] * N_HEADS, axis=0)
        lane_head = lax.broadcasted_iota(jnp.int32, (1, WIDTH), 1) >> HEAD_SHIFT
        q_sc[...] = jnp.where(row_head == lane_head, q, 0.0).astype(BF16)
        cq = jnp.concatenate([cq_ref[0].astype(F32)] * N_HEADS, axis=0)
        lane = lax.broadcasted_iota(jnp.int32, (1, LANES), 1)
        cqm = jnp.where((lane >> GROUP_SHIFT) == row_head, cq, 0.0)
        cqm_sc[...] = cqm.astype(BF16)
        slot = lax.broadcasted_iota(jnp.int32, cqm.shape, 1) & (BIAS_GROUP - 1)
        roff_sc[...] = jnp.sum(jnp.where(slot < 3, cqm, 0.0), axis=-1, keepdims=True)
        carry_sc[...] = jnp.zeros(carry_sc.shape, F32)
        m_sc[...] = jnp.full(m_sc.shape, NEG_BIG, F32)
        l_sc[...] = jnp.zeros(l_sc.shape, F32)
        acc_sc[...] = jnp.zeros(acc_sc.shape, F32)

    q_bd = q_sc[...]
    ti = lax.broadcasted_iota(jnp.int32, (PAGE_SIZE, PAGE_SIZE), 0)
    tj = lax.broadcasted_iota(jnp.int32, (PAGE_SIZE, PAGE_SIZE), 1)
    later = (ti > tj).astype(F32)
    every = jnp.ones((PAGE_SIZE, PAGE_SIZE), F32)
    expand = (row_head == lax.broadcasted_iota(jnp.int32, (1, N_HEADS), 1)).astype(F32)

    carry = carry_sc[...]
    roff = roff_sc[...]
    scores = []
    for g in range(g_pages):
        lf = lf_refs[g][0]
        bias = _dot(expand, _dot(lf, later, HIGHEST) + carry, HIGHEST)
        carry = carry + _dot(lf, every, HIGHEST)
        k_t = k_refs[g][0].reshape(WIDTH, PAGE_SIZE).astype(BF16)
        scores.append(_dot(q_bd, k_t) + bias + roff)
    carry_sc[...] = carry

    m = m_sc[...]
    m_new = m
    for s in scores:
        m_new = jnp.maximum(m_new, jnp.max(s, axis=-1, keepdims=True))
    alpha = jnp.exp(m - m_new)
    l_new = alpha * l_sc[...]
    acc = alpha * acc_sc[...]
    for g, s in enumerate(scores):
        pe = jnp.exp(s - m_new)
        l_new = l_new + jnp.sum(pe, axis=-1, keepdims=True)
        v_t = v_refs[g][0].reshape(WIDTH, PAGE_SIZE).astype(BF16)
        acc = acc + _dot_nt(pe.astype(BF16), v_t)
    m_sc[...] = m_new
    l_sc[...] = l_new
    acc_sc[...] = acc

    @pl.when(step == pl.num_programs(1) - 1)
    def _():
        s = _dot_nt(q_bd, kn_ref[0]) + _dot_nt(cqm_sc[...], ckn_ref[0])
        row_tok = lax.broadcasted_iota(jnp.int32, (rows, 1), 0) & (n_new - 1)
        s = jnp.where(lax.broadcasted_iota(jnp.int32, (1, n_new), 1) <= row_tok, s, NEG_BIG)
        m_fin = jnp.maximum(m_new, jnp.max(s, axis=-1, keepdims=True))
        a_fin = jnp.exp(m_new - m_fin)
        pe = jnp.exp(s - m_fin)
        l_fin = a_fin * l_new + jnp.sum(pe, axis=-1, keepdims=True)
        out = (a_fin * acc + _dot(pe.astype(BF16), vn_ref[0])) / l_fin
        lane_head = lax.broadcasted_iota(jnp.int32, (n_new, WIDTH), 1) >> HEAD_SHIFT
        res = jnp.zeros((n_new, WIDTH), F32)
        for h in range(N_HEADS):
            res = res + jnp.where(lane_head == h, out[h * n_new:(h + 1) * n_new, :], 0.0)
        o_ref[0] = res.astype(BF16)


def _fox_sample(page_table, q, cq, k_new, ck_new, v_new, cache_k, cache_v, cache_logf):
    n_seq, n_new, _ = q.shape
    n_pages = page_table.shape[1]
    g_pages = PAGES_PER_STEP
    rows = N_HEADS * n_new

    def per_seq(width):
        return pl.BlockSpec((1, n_new, width), lambda b, j, pt: (b, 0, 0))

    def page_spec(g, arr):
        block = (1,) + arr.shape[1:]
        return pl.BlockSpec(block, lambda b, j, pt: (pt[b, n_pages - 1 - (j * g_pages + g)],)
                            + (0,) * (arr.ndim - 1))

    grid_spec = pltpu.PrefetchScalarGridSpec(
        num_scalar_prefetch=1,
        grid=(n_seq, n_pages // g_pages),
        in_specs=([per_seq(WIDTH), per_seq(LANES), per_seq(WIDTH), per_seq(LANES), per_seq(WIDTH)]
                  + [page_spec(g, cache_k) for g in range(g_pages)]
                  + [page_spec(g, cache_v) for g in range(g_pages)]
                  + [page_spec(g, cache_logf) for g in range(g_pages)]),
        out_specs=pl.BlockSpec((1, n_new, WIDTH), lambda b, j, pt: (b, 0, 0)),
        scratch_shapes=[pltpu.VMEM((rows, WIDTH), BF16), pltpu.VMEM((rows, LANES), BF16),
                        pltpu.VMEM((rows, 1), F32), pltpu.VMEM((N_HEADS, PAGE_SIZE), F32),
                        pltpu.VMEM((rows, 1), F32), pltpu.VMEM((rows, 1), F32),
                        pltpu.VMEM((rows, WIDTH), F32)],
    )
    return pl.pallas_call(
        functools.partial(_fox_sample_kernel, n_new=n_new),
        grid_spec=grid_spec,
        out_shape=jax.ShapeDtypeStruct((n_seq, n_new, WIDTH), BF16),
        compiler_params=pltpu.CompilerParams(
            dimension_semantics=("arbitrary", "arbitrary"), vmem_limit_bytes=VMEM_LIMIT),
        name="fox_sample",
    )(page_table, q, cq, k_new, ck_new, v_new,
      *([cache_k] * g_pages), *([cache_v] * g_pages), *([cache_logf] * g_pages))


def _wkv_kernel(r_ref, lw_ref, kr_ref, vr_ref, a_ref, g_ref, s0_ref,
                kk_ref, ka_ref, rk_ref, gg_ref, gb_ref, o_ref, so_ref, s_sc):
    @pl.when(pl.program_id(1) == 0)
    def _():
        s_sc[...] = s0_ref[0]

    C = r_ref.shape[1]
    ri = lax.broadcasted_iota(jnp.int32, (C, C), 0)
    ci = lax.broadcasted_iota(jnp.int32, (C, C), 1)
    incl = ri >= ci
    strict = ri > ci
    eye = (ri == ci).astype(F32)
    lcum = _dot(incl.astype(F32), lw_ref[0], HIGHEST)
    levels = max(1, (C - 1).bit_length())

    heads = range(N_HEADS)
    hd = []
    for pair in range(N_HEADS // 2):
        lanes = slice(pair * LANES, (pair + 1) * LANES)
        blocks = [ref[0, :, lanes] for ref in (r_ref, lw_ref, kr_ref, vr_ref, a_ref, g_ref)]
        blocks.append(lcum[:, lanes])
        hd.append([x[:, :HEAD_DIM] for x in blocks])
        hd.append([pltpu.roll(x, HEAD_DIM, 1)[:, :HEAD_DIM] for x in blocks])

    xs, bts, kts, bgs, kgs, vbs, kps, e_ends = [], [], [], [], [], [], [], []
    for h in heads:
        rh, lwh, krh, vh, ah, gh, lc = hd[h]
        kkr = krh * kk_ref[h:h + 1, :]
        norm = jnp.sqrt(jnp.sum(kkr * kkr, axis=-1, keepdims=True))
        kk = kkr / jnp.maximum(norm, 1e-12)
        kp = krh * (1.0 + (ah - 1.0) * ka_ref[h:h + 1, :])
        bv = kk * ah
        e_neg = jnp.exp(-lc)
        l_end = lc[C - 1:C, :]
        e_rem = jnp.exp(l_end - lc)
        rt = rh * jnp.exp(lc)
        at = -kk * jnp.exp(lc - lwh)
        xs.append(jnp.concatenate([at, rt], axis=0).astype(BF16))
        bts.append((bv * e_neg).astype(BF16))
        kts.append((kp * e_neg).astype(BF16))
        bgs.append((bv * e_rem).astype(BF16))
        kgs.append((kp * e_rem).astype(BF16))
        vbs.append(vh.astype(BF16))
        kps.append(kp)
        e_ends.append(jnp.exp(l_end))

    sc_b = [_dot_nt(xs[h], bts[h]) for h in heads]
    sc_k = [_dot_nt(xs[h], kts[h]) for h in heads]
    a_ab = [jnp.where(strict, sc_b[h][:C], 0.0) for h in heads]
    a_ak = [jnp.where(strict, sc_k[h][:C], 0.0).astype(BF16) for h in heads]
    a_rb = [jnp.where(incl, sc_b[h][C:], 0.0).astype(BF16) for h in heads]
    a_rk = [jnp.where(incl, sc_k[h][C:], 0.0).astype(BF16) for h in heads]

    tinv = [eye + a_ab[h] for h in heads]
    if levels > 1:
        lb = [a_ab[h].astype(BF16) for h in heads]
        lpow = [_dot(lb[h], lb[h]) for h in heads]
        for level in range(1, levels):
            qb = [lpow[h].astype(BF16) for h in heads]
            if level == levels - 1:
                tinv = [tinv[h] + _dot(tinv[h].astype(BF16), qb[h]) for h in heads]
            else:
                both = [_dot(jnp.concatenate([tinv[h], lpow[h]], axis=0).astype(BF16), qb[h])
                        for h in heads]
                tinv = [tinv[h] + both[h][:C] for h in heads]
                lpow = [both[h][C:] for h in heads]
    tb = [tinv[h].astype(BF16) for h in heads]

    s_old = [s_sc[h] for h in heads]
    ps = [_dot_nt(xs[h], s_old[h].astype(BF16)) for h in heads]
    w1 = [(ps[h][:C] + _dot(a_ak[h], vbs[h])).astype(BF16) for h in heads]
    ub = [_dot(tb[h], w1[h]).astype(BF16) for h in heads]
    ys = [ps[h][C:] + _dot(a_rb[h], ub[h]) + _dot(a_rk[h], vbs[h]) for h in heads]
    for h in heads:
        s_sc[h] = s_old[h] * e_ends[h] + _dot_tn(ub[h], bgs[h]) + _dot_tn(vbs[h], kgs[h])

    outs = []
    for h in heads:
        rh, _, _, vh, _, gh, _ = hd[h]
        y = ys[h]
        mean = jnp.mean(y, axis=-1, keepdims=True)
        yc = y - mean
        var = jnp.mean(yc * yc, axis=-1, keepdims=True)
        yn = yc * lax.rsqrt(var + GN_EPS) * gg_ref[h:h + 1, :] + gb_ref[h:h + 1, :]
        bonus = jnp.sum(rh * kps[h] * rk_ref[h:h + 1, :], axis=-1, keepdims=True) * vh
        outs.append((yn + bonus) * gh)
    for pair in range(N_HEADS // 2):
        o_ref[0, :, pair * LANES:(pair + 1) * LANES] = jnp.concatenate(
            outs[2 * pair:2 * pair + 2], axis=1).astype(BF16)

    @pl.when(pl.program_id(1) == pl.num_programs(1) - 1)
    def _():
        so_ref[0] = s_sc[...]


def _wkv(r, lw, kr, vr, a, g, s0, params, chunk):
    batch, seq, _ = r.shape
    shared_state = s0.shape[0] == 1
    tok = pl.BlockSpec((1, chunk, WIDTH), lambda b, c: (b, c, 0))
    state_in = pl.BlockSpec((1, N_HEADS, HEAD_DIM, HEAD_DIM),
                            (lambda b, c: (0, 0, 0, 0)) if shared_state else (lambda b, c: (b, 0, 0, 0)))
    state_out = pl.BlockSpec((1, N_HEADS, HEAD_DIM, HEAD_DIM), lambda b, c: (b, 0, 0, 0))
    return pl.pallas_call(
        _wkv_kernel,
        grid=(batch, seq // chunk),
        in_specs=[tok] * 6 + [state_in] + [_const_spec(p) for p in params],
        out_specs=[tok, state_out],
        out_shape=[jax.ShapeDtypeStruct((batch, seq, WIDTH), BF16),
                   jax.ShapeDtypeStruct((batch, N_HEADS, HEAD_DIM, HEAD_DIM), F32)],
        scratch_shapes=[pltpu.VMEM((N_HEADS, HEAD_DIM, HEAD_DIM), F32)],
        compiler_params=pltpu.CompilerParams(
            dimension_semantics=("arbitrary", "arbitrary"), vmem_limit_bytes=VMEM_LIMIT),
        name="wkv",
    )(r, lw, kr, vr, a, g, s0, *params)


def _out_ffn_kernel(x_ref, att_ref, rw_ref, woa_ref, wor_ref, ln2_ref, wg_ref, wu_ref, wd_ref,
                    lnf_ref, y_ref):
    h = x_ref[...] + _dot(att_ref[...], woa_ref[...]) + _dot(rw_ref[...], wor_ref[...])
    hb = _rms(h, ln2_ref[...]).astype(BF16)
    ffn = jnp.zeros(h.shape, F32)
    d_ff = wg_ref.shape[1]
    for c0 in range(0, d_ff, FF_CHUNK):
        gate = _dot(hb, wg_ref[:, c0:c0 + FF_CHUNK])
        up = _dot(hb, wu_ref[:, c0:c0 + FF_CHUNK])
        act = gate * _sigmoid(gate) * up
        ffn = ffn + _dot(act.astype(BF16), wd_ref[c0:c0 + FF_CHUNK, :])
    y_ref[...] = _rms(h + ffn, lnf_ref[...])


def _out_ffn(x, att, rw, weights):
    rows = x.shape[0]
    tile = min(ROW_TILE, rows)
    tok = lambda width: pl.BlockSpec((tile, width), lambda i: (i, 0))
    return pl.pallas_call(
        _out_ffn_kernel,
        grid=(rows // tile,),
        in_specs=[tok(D_MODEL), tok(WIDTH), tok(WIDTH)] + [_const_spec(w) for w in weights],
        out_specs=tok(D_MODEL),
        out_shape=jax.ShapeDtypeStruct((rows, D_MODEL), F32),
        compiler_params=pltpu.CompilerParams(
            dimension_semantics=("arbitrary",), vmem_limit_bytes=VMEM_LIMIT),
        name="out_ffn",
    )(x, att, rw, *weights)


def _bias_selectors():
    part = jnp.arange(3)[:, None, None]
    src = jnp.arange(LANES)[None, :, None]
    dst = jnp.arange(LANES)[None, None, :]
    is_head = src < N_HEADS
    q_sel = (is_head & (dst == src * BIAS_GROUP + part)).astype(F32)
    k_sel = -(is_head & (dst == src * BIAS_GROUP + 3 + part)).astype(F32)
    sel = jnp.concatenate([q_sel, k_sel], axis=2).reshape(3 * LANES, 2 * LANES)
    slot = jnp.arange(LANES) % BIAS_GROUP
    ones = jnp.concatenate([(slot >= 3) & (slot < 6), slot < 3]).astype(F32)[None, :]
    return sel.astype(BF16), ones


def kernel(x_prompt, x_sample, cache_k, cache_v, cache_logf, state_wkv, state_shift, page_table,
           meta_tokens, ln1_g, w_in, b_f, mu_shift, w0, w2, a0, a2, g2, k_k, k_a, r_k,
           gn_g, gn_b, w_out, ln2_g, w_gate, w_up, w_down, lnf_g):
    batch, seq, _ = x_prompt.shape
    n_seq, n_new, _ = x_sample.shape
    fox_cols = 3 * WIDTH + N_HEADS

    w_in0 = w_in[0]
    wqkv = w_in0[:, :3 * WIDTH].astype(BF16)
    wf = jnp.pad(w_in0[:, 3 * WIDTH:fox_cols], ((0, 0), (0, LANES - N_HEADS))).astype(BF16)
    bf = jnp.pad(b_f[0], (0, LANES - N_HEADS))[None, :]
    wrw = w_in0[:, fox_cols:].astype(BF16)
    sel, ones = _bias_selectors()
    w2p = jnp.concatenate([w2[0], jnp.zeros((ICLR_LORA, WIDTH), F32)], axis=0).astype(BF16)
    a2p = jnp.concatenate([jnp.zeros((DECAY_LORA, WIDTH), F32), a2[0]], axis=0).astype(BF16)
    inproj_w = (ln1_g[0][None, :], wqkv, wf, bf, wrw, mu_shift[0][None, :], sel, ones,
                w2p, a2p, g2[0].astype(BF16), w0[0][None, :], a0[0][None, :])
    per_head = lambda t: t.reshape(N_HEADS, HEAD_DIM)
    wkv_p = (per_head(k_k[0]), per_head(k_a[0]), r_k[0], per_head(gn_g[0]), per_head(gn_b[0]))
    ffn_w = (w_out[0][:WIDTH].astype(BF16), w_out[0][WIDTH:].astype(BF16), ln2_g[0][None, :],
             w_gate[0].astype(BF16), w_up[0].astype(BF16), w_down[0].astype(BF16), lnf_g[None, :])

    zrow = jnp.zeros((1, 1, RWKV_COLS), F32)
    zc = jnp.zeros((1, 1, LANES), F32)
    mo = _inproj_long(meta_tokens[None], zrow, zc, inproj_w, N_META)
    (_, mk, mkb, mv, mvb, mlf, _, mck, mr, mlw, mkr, mvr, ma, mg, _, mprow, mclast) = mo
    zstate = jnp.zeros((1, N_HEADS, HEAD_DIM, HEAD_DIM), F32)
    _, s_meta = _wkv(mr, mlw, mkr, mvr, ma, mg, zstate, wkv_p, N_META)

    po = _inproj_long(x_prompt, mprow, mclast, inproj_w, ROW_TILE)
    (pq, pk, pkb, pv, pvb, plf, pcq, pck, pr, plw, pkr, pvr, pa, pg, pxl, _, _) = po
    att_p = _fox_prompt(pq, pcq, mkb, mck, mvb, pkb, pck, pvb)
    rw_p, wkv_prompt = _wkv(pr, plw, pkr, pvr, pa, pg, s_meta, wkv_p, WKV_CHUNK)
    y_prompt = _out_ffn(x_prompt.reshape(batch * seq, D_MODEL), att_p.reshape(batch * seq, WIDTH),
                        rw_p.reshape(batch * seq, WIDTH), ffn_w).reshape(batch, seq, D_MODEL)

    rows_s = n_seq * n_new
    xprev = jnp.repeat(state_shift[0], n_new, axis=0)[None]
    so = _inproj_packed(x_sample.reshape(1, rows_s, D_MODEL), xprev, inproj_w, n_new)
    (sq, sk, skb, sv, svb, slf, scq, sck, sr, slw, skr, svr, sa, sg, sxn) = so
    seqs = lambda t: t.reshape(n_seq, n_new, t.shape[-1])
    att_s = _fox_sample(page_table, seqs(sq), seqs(scq), seqs(skb), seqs(sck), seqs(svb),
                        jnp.transpose(cache_k[0], (0, 2, 3, 1)), jnp.transpose(cache_v[0], (0, 2, 3, 1)),
                        jnp.transpose(cache_logf[0], (0, 2, 1)))
    rw_s, wkv_sample = _wkv(seqs(sr), seqs(slw), seqs(skr), seqs(svr), seqs(sa), seqs(sg),
                            state_wkv[0], wkv_p, n_new)
    y_sample = _out_ffn(x_sample.reshape(rows_s, D_MODEL), att_s.reshape(rows_s, WIDTH),
                        rw_s.reshape(rows_s, WIDTH), ffn_w).reshape(n_seq, n_new, D_MODEL)

    def with_meta(meta, main):
        return jnp.concatenate([jnp.broadcast_to(meta, (batch,) + meta.shape[1:]), main], axis=1)

    heads = lambda t: t.reshape(t.shape[:-1] + (N_HEADS, HEAD_DIM))
    k_prompt = heads(with_meta(mk, pk))[None]
    v_prompt = heads(with_meta(mv, pv))[None]
    logf_prompt = with_meta(mlf, plf)[None]
    shift_prompt = pxl.reshape(1, batch, D_MODEL)
    k_sample = heads(seqs(sk))[None]
    v_sample = heads(seqs(sv))[None]
    logf_sample = seqs(slf)[None]
    shift_sample = seqs(sxn)[:, -1][None]
    return (y_prompt, y_sample, k_prompt, v_prompt, logf_prompt, wkv_prompt[None], shift_prompt,
            k_sample, v_sample, logf_sample, wkv_sample[None], shift_sample)
```

```python
import functools

import jax
import jax.numpy as jnp
from jax import lax
from jax.experimental import pallas as pl
from jax.experimental.pallas import tpu as pltpu

D_MODEL = 1024
N_META = 16
HEAD_DIM = 64
N_HEADS = 8
WIDTH = N_HEADS * HEAD_DIM
PAGE_SIZE = 128
DECAY_LORA = 64
ICLR_LORA = 64
GATE_LORA = 128
RWKV_COLS = 3 * WIDTH + DECAY_LORA + ICLR_LORA + GATE_LORA
RMS_EPS = 1e-6
GN_EPS = 64e-5
NEG_BIG = -1e30

LANES = 128
BIAS_GROUP = 16
HEAD_SHIFT = HEAD_DIM.bit_length() - 1
GROUP_SHIFT = BIAS_GROUP.bit_length() - 1
VMEM_LIMIT = 56 * 1024 * 1024

ROW_TILE = 256
ATT_TILE = 1024
ATT_K_TILE = 256
ATT_SUB = 128
ATT_GROUP = 8
WKV_CHUNK = 64
PAGES_PER_STEP = 8
FF_CHUNK = 256

BF16 = jnp.bfloat16
F32 = jnp.float32
HIGHEST = lax.Precision.HIGHEST


def _dot(a, b, precision=None):
    return jnp.dot(a, b, preferred_element_type=F32, precision=precision)


def _dot_nt(a, b, precision=None):
    return lax.dot_general(a, b, (((1,), (1,)), ((), ())), preferred_element_type=F32,
                           precision=precision)


def _dot_tn(a, b, precision=None):
    return lax.dot_general(a, b, (((0,), (0,)), ((), ())), preferred_element_type=F32,
                           precision=precision)


def _rms(x, g):
    return x * lax.rsqrt(jnp.mean(x * x, axis=-1, keepdims=True) + RMS_EPS) * g


def _softplus(z):
    return jnp.maximum(z, 0.0) + jnp.log1p(jnp.exp(-jnp.abs(z)))


def _sigmoid(z):
    return 1.0 / (1.0 + jnp.exp(-z))


def _mask_bf16(cond):
    return cond.astype(F32).astype(BF16)


def _split3(c):
    hi = c.astype(BF16)
    r1 = c - hi.astype(F32)
    mid = r1.astype(BF16)
    lo = (r1 - mid.astype(F32)).astype(BF16)
    return hi, mid, lo


def _inproj_outputs(xn, qkv, logf, c, p, p_prev, w, outs):
    (mu_ref, sel_ref, ones_ref, w2_ref, a2_ref, g2_ref, w0_ref, a0_ref) = w
    (q_ref, k_ref, kb_ref, v_ref, vb_ref, lf_ref, cq_ref, ck_ref,
     r_ref, lw_ref, kr_ref, vr_ref, a_ref, g_ref) = outs
    q_ref[0] = (qkv[:, :WIDTH] * (HEAD_DIM ** -0.5)).astype(BF16)
    k = qkv[:, WIDTH:2 * WIDTH]
    k_ref[0] = k
    kb_ref[0] = k.astype(BF16)
    v = qkv[:, 2 * WIDTH:]
    v_ref[0] = v
    vb_ref[0] = v.astype(BF16)
    lf_ref[0] = logf[:, :N_HEADS]
    cparts = jnp.concatenate(_split3(c), axis=1)
    cc = _dot(cparts, sel_ref[...]) + ones_ref[...]
    cq_ref[0] = cc[:, :LANES].astype(BF16)
    ck_ref[0] = cc[:, LANES:].astype(BF16)
    rw = p + (p_prev - p) * mu_ref[...]
    r_ref[0] = rw[:, :WIDTH]
    kr_ref[0] = rw[:, WIDTH:2 * WIDTH]
    vr_ref[0] = rw[:, 2 * WIDTH:3 * WIDTH]
    z = rw[:, 3 * WIDTH:3 * WIDTH + LANES]
    gl = rw[:, 3 * WIDTH + LANES:]
    w_log = -_softplus(-(w0_ref[...] + _dot(jnp.tanh(z).astype(BF16), w2_ref[...]))) - 0.5
    lw_ref[0] = -jnp.exp(w_log)
    a_ref[0] = _sigmoid(a0_ref[...] + _dot(z.astype(BF16), a2_ref[...]))
    g_ref[0] = _dot(_sigmoid(gl).astype(BF16), g2_ref[...])


def _inproj_long_kernel(x_ref, prow_ref, c0_ref, ln_ref, wqkv_ref, wf_ref, bf_ref, wrw_ref,
                        mu_ref, sel_ref, ones_ref, w2_ref, a2_ref, g2_ref, w0_ref, a0_ref,
                        q_ref, k_ref, kb_ref, v_ref, vb_ref, lf_ref, cq_ref, ck_ref,
                        r_ref, lw_ref, kr_ref, vr_ref, a_ref, g_ref, xl_ref, pl_ref, cl_ref,
                        pcar_ref, ccar_ref):
    @pl.when(pl.program_id(1) == 0)
    def _():
        pcar_ref[...] = prow_ref[0]
        ccar_ref[...] = c0_ref[0]

    rows = x_ref.shape[1]
    xn = _rms(x_ref[0], ln_ref[...])
    xb = xn.astype(BF16)
    qkv = _dot(xb, wqkv_ref[...])
    logf = -_softplus(-(_dot(xb, wf_ref[...]) + bf_ref[...]))
    ri = lax.broadcasted_iota(jnp.int32, (rows, rows), 0)
    ci = lax.broadcasted_iota(jnp.int32, (rows, rows), 1)
    c = _dot((ri >= ci).astype(F32), logf, HIGHEST) + ccar_ref[...]
    ccar_ref[...] = c[rows - 1:rows, :]
    p = _dot(xb, wrw_ref[...])
    first = lax.broadcasted_iota(jnp.int32, p.shape, 0) == 0
    p_prev = jnp.where(first, pcar_ref[...], pltpu.roll(p, 1, 0))
    pcar_ref[...] = p[rows - 1:rows, :]
    xl_ref[0] = xn[rows - 1:rows, :]
    pl_ref[0] = p[rows - 1:rows, :]
    cl_ref[0] = c[rows - 1:rows, :]
    _inproj_outputs(xn, qkv, logf, c, p, p_prev,
                    (mu_ref, sel_ref, ones_ref, w2_ref, a2_ref, g2_ref, w0_ref, a0_ref),
                    (q_ref, k_ref, kb_ref, v_ref, vb_ref, lf_ref, cq_ref, ck_ref,
                     r_ref, lw_ref, kr_ref, vr_ref, a_ref, g_ref))


def _inproj_packed_kernel(x_ref, xprev_ref, ln_ref, wqkv_ref, wf_ref, bf_ref, wrw_ref,
                          mu_ref, sel_ref, ones_ref, w2_ref, a2_ref, g2_ref, w0_ref, a0_ref,
                          q_ref, k_ref, kb_ref, v_ref, vb_ref, lf_ref, cq_ref, ck_ref,
                          r_ref, lw_ref, kr_ref, vr_ref, a_ref, g_ref, xn_ref, *, seq_len):
    rows = x_ref.shape[1]
    shift = seq_len.bit_length() - 1
    xn = _rms(x_ref[0], ln_ref[...])
    xn_ref[0] = xn
    xb = xn.astype(BF16)
    qkv = _dot(xb, wqkv_ref[...])
    logf = -_softplus(-(_dot(xb, wf_ref[...]) + bf_ref[...]))
    ri = lax.broadcasted_iota(jnp.int32, (rows, rows), 0)
    ci = lax.broadcasted_iota(jnp.int32, (rows, rows), 1)
    same_seq = (ri >> shift) == (ci >> shift)
    c = _dot(((ri >= ci) & same_seq).astype(F32), logf, HIGHEST)
    p = _dot(xb, wrw_ref[...])
    p_first = _dot(xprev_ref[0].astype(BF16), wrw_ref[...])
    first = (lax.broadcasted_iota(jnp.int32, p.shape, 0) & (seq_len - 1)) == 0
    p_prev = jnp.where(first, p_first, pltpu.roll(p, 1, 0))
    _inproj_outputs(xn, qkv, logf, c, p, p_prev,
                    (mu_ref, sel_ref, ones_ref, w2_ref, a2_ref, g2_ref, w0_ref, a0_ref),
                    (q_ref, k_ref, kb_ref, v_ref, vb_ref, lf_ref, cq_ref, ck_ref,
                     r_ref, lw_ref, kr_ref, vr_ref, a_ref, g_ref))


def _const_spec(arr):
    return pl.BlockSpec(arr.shape, lambda *_: (0,) * arr.ndim)


def _inproj_out_shapes(batch, rows_total):
    def s(width, dtype):
        return jax.ShapeDtypeStruct((batch, rows_total, width), dtype)
    return [s(WIDTH, BF16), s(WIDTH, F32), s(WIDTH, BF16), s(WIDTH, F32), s(WIDTH, BF16),
            s(N_HEADS, F32), s(LANES, BF16), s(LANES, BF16)] + [s(WIDTH, F32)] * 6


def _inproj_out_specs(tile):
    def s(width):
        return pl.BlockSpec((1, tile, width), lambda b, i: (b, i, 0))
    return [s(WIDTH)] * 5 + [s(N_HEADS), s(LANES), s(LANES)] + [s(WIDTH)] * 6


def _inproj_long(x, prow, c0, weights, tile):
    batch, seq, _ = x.shape
    n_tiles = seq // tile
    row = lambda width: pl.BlockSpec((1, 1, width), lambda b, i: (b, 0, 0))
    shapes = _inproj_out_shapes(batch, seq) + [
        jax.ShapeDtypeStruct((batch, 1, D_MODEL), F32),
        jax.ShapeDtypeStruct((batch, 1, RWKV_COLS), F32),
        jax.ShapeDtypeStruct((batch, 1, LANES), F32)]
    specs = _inproj_out_specs(tile) + [row(D_MODEL), row(RWKV_COLS), row(LANES)]
    return pl.pallas_call(
        _inproj_long_kernel,
        grid=(batch, n_tiles),
        in_specs=[pl.BlockSpec((1, tile, D_MODEL), lambda b, i: (b, i, 0)),
                  _const_spec(prow), _const_spec(c0)] + [_const_spec(w) for w in weights],
        out_specs=specs,
        out_shape=shapes,
        scratch_shapes=[pltpu.VMEM((1, RWKV_COLS), F32), pltpu.VMEM((1, LANES), F32)],
        compiler_params=pltpu.CompilerParams(
            dimension_semantics=("arbitrary", "arbitrary"), vmem_limit_bytes=VMEM_LIMIT),
        name="inproj_long",
    )(x, prow, c0, *weights)


def _inproj_packed(x, xprev, weights, seq_len):
    _, rows, _ = x.shape
    full = pl.BlockSpec((1, rows, D_MODEL), lambda b, i: (0, 0, 0))
    return pl.pallas_call(
        functools.partial(_inproj_packed_kernel, seq_len=seq_len),
        grid=(1, 1),
        in_specs=[full, full] + [_const_spec(w) for w in weights],
        out_specs=_inproj_out_specs(rows) + [full],
        out_shape=_inproj_out_shapes(1, rows) + [jax.ShapeDtypeStruct((1, rows, D_MODEL), F32)],
        compiler_params=pltpu.CompilerParams(
            dimension_semantics=("arbitrary", "arbitrary"), vmem_limit_bytes=VMEM_LIMIT),
        name="inproj_packed",
    )(x, xprev, *weights)


def _fox_prompt_kernel(q_ref, cq_ref, km_ref, ckm_ref, vm_ref, k_ref, ck_ref, v_ref, o_ref):
    pair = pl.program_id(1)
    qi = pl.program_id(2)
    tile = q_ref.shape[1]
    lane = lax.broadcasted_iota(jnp.int32, (1, LANES), 1)
    q2 = q_ref[0]
    cq = cq_ref[0]
    n_sub = tile // ATT_SUB
    lhs = []
    for hh in range(2):
        head_lanes = (lane >= hh * HEAD_DIM) & (lane < (hh + 1) * HEAD_DIM)
        g0 = (2 * pair + hh) * BIAS_GROUP
        group_lanes = (lane >= g0) & (lane < g0 + BIAS_GROUP)
        full = jnp.concatenate([q2 * _mask_bf16(head_lanes), cq * _mask_bf16(group_lanes)], axis=1)
        lhs.append([full[r * ATT_SUB:(r + 1) * ATT_SUB, :] for r in range(n_sub)])
    row_minus_col = (lax.broadcasted_iota(jnp.int32, (ATT_SUB, ATT_K_TILE), 0)
                     - lax.broadcasted_iota(jnp.int32, (ATT_SUB, ATT_K_TILE), 1))

    def block(carry, kk, vv, first_col):
        out = list(carry)
        live = [(hh, r) for hh in range(2) for r in range(n_sub)
                if first_col is None or first_col <= (r + 1) * ATT_SUB - 1]
        for g0 in range(0, len(live), ATT_GROUP):
            group = live[g0:g0 + ATT_GROUP]
            scores, m_news = [], []
            for hh, r in group:
                s = _dot_nt(lhs[hh][r], kk)
                if first_col is not None and first_col + kk.shape[0] - 1 > r * ATT_SUB:
                    s = jnp.where(row_minus_col >= first_col - r * ATT_SUB, s, NEG_BIG)
                scores.append(s)
            for (hh, r), s in zip(group, scores):
                m_news.append(jnp.maximum(out[hh * n_sub + r][0], jnp.max(s, axis=-1, keepdims=True)))
            for (hh, r), s, m_new in zip(group, scores, m_news):
                m, l, acc = out[hh * n_sub + r]
                alpha = jnp.exp(m - m_new)
                pe = jnp.exp(s - m_new)
                l_new = alpha * l + jnp.sum(pe, axis=-1, keepdims=True)
                acc_new = alpha * acc + _dot(pe.astype(BF16), vv)
                out[hh * n_sub + r] = (m_new, l_new, acc_new)
        return tuple(out)

    init = tuple((jnp.full((ATT_SUB, 1), NEG_BIG, F32), jnp.zeros((ATT_SUB, 1), F32),
                  jnp.zeros((ATT_SUB, LANES), F32)) for _ in range(2 * n_sub))
    carry = block(init, jnp.concatenate([km_ref[0], ckm_ref[0]], axis=1), vm_ref[0], None)

    def keys(j):
        start = pl.multiple_of(j * ATT_K_TILE, ATT_K_TILE)
        rows = pl.ds(start, ATT_K_TILE)
        return jnp.concatenate([k_ref[0, rows, :], ck_ref[0, rows, :]], axis=1), v_ref[0, rows, :]

    per_q = tile // ATT_K_TILE
    carry = lax.fori_loop(0, qi * per_q, lambda j, c: block(c, *keys(j), None), carry)
    for d in range(per_q):
        carry = block(carry, *keys(qi * per_q + d), d * ATT_K_TILE)
    outs = [jnp.concatenate([carry[hh * n_sub + r][2] / carry[hh * n_sub + r][1]
                             for r in range(n_sub)], axis=0) for hh in range(2)]
    first_head = lax.broadcasted_iota(jnp.int32, outs[0].shape, 1) < HEAD_DIM
    o_ref[0] = jnp.where(first_head, outs[0], outs[1]).astype(BF16)


def _fox_prompt(q, cq, k_meta, ck_meta, v_meta, k, ck, v):
    batch, seq, _ = q.shape
    n_pairs = N_HEADS // 2
    return pl.pallas_call(
        _fox_prompt_kernel,
        grid=(batch, n_pairs, seq // ATT_TILE),
        in_specs=[
            pl.BlockSpec((1, ATT_TILE, LANES), lambda b, p, i: (b, i, p)),
            pl.BlockSpec((1, ATT_TILE, LANES), lambda b, p, i: (b, i, 0)),
            pl.BlockSpec((1, N_META, LANES), lambda b, p, i: (0, 0, p)),
            pl.BlockSpec((1, N_META, LANES), lambda b, p, i: (0, 0, 0)),
            pl.BlockSpec((1, N_META, LANES), lambda b, p, i: (0, 0, p)),
            pl.BlockSpec((1, seq, LANES), lambda b, p, i: (b, 0, p)),
            pl.BlockSpec((1, seq, LANES), lambda b, p, i: (b, 0, 0)),
            pl.BlockSpec((1, seq, LANES), lambda b, p, i: (b, 0, p)),
        ],
        out_specs=pl.BlockSpec((1, ATT_TILE, LANES), lambda b, p, i: (b, i, p)),
        out_shape=jax.ShapeDtypeStruct((batch, seq, WIDTH), BF16),
        compiler_params=pltpu.CompilerParams(
            dimension_semantics=("arbitrary", "arbitrary", "arbitrary"),
            vmem_limit_bytes=VMEM_LIMIT),
        name="fox_prompt",
    )(q, cq, k_meta, ck_meta, v_meta, k, ck, v)


def _fox_sample_kernel(pt_ref, q_ref, cq_ref, kn_ref, ckn_ref, vn_ref, *rest, n_new):
    del pt_ref
    g_pages = PAGES_PER_STEP
    k_refs, v_refs, lf_refs = rest[:g_pages], rest[g_pages:2 * g_pages], rest[2 * g_pages:3 * g_pages]
    o_ref, q_sc, cqm_sc, roff_sc, carry_sc, m_sc, l_sc, acc_sc = rest[3 * g_pages:]
    step = pl.program_id(1)
    rows = N_HEADS * n_new
    row_head = lax.broadcasted_iota(jnp.int32, (rows, 1), 0) & (N_HEADS - 1)

    def per_head_rows(x):
        return jnp.concatenate([jnp.broadcast_to(x[i:i + 1, :], (N_HEADS, x.shape[1]))
                                for i in range(n_new)], axis=0)

    @pl.when(step == 0)
    def _():
        q = per_head_rows(q_ref[0].astype(F32))
        lane_head = lax.broadcasted_iota(jnp.int32, (1, WIDTH), 1) >> HEAD_SHIFT
        q_sc[...] = jnp.where(row_head == lane_head, q, 0.0).astype(BF16)
        cq = per_head_rows(cq_ref[0].astype(F32))
        lane = lax.broadcasted_iota(jnp.int32, (1, LANES), 1)
        cqm = jnp.where((lane >> GROUP_SHIFT) == row_head, cq, 0.0)
        cqm_sc[...] = cqm.astype(BF16)
        slot = lax.broadcasted_iota(jnp.int32, cqm.shape, 1) & (BIAS_GROUP - 1)
        roff_sc[...] = jnp.sum(jnp.where(slot < 3, cqm, 0.0), axis=-1, keepdims=True)
        carry_sc[...] = jnp.zeros(carry_sc.shape, F32)
        m_sc[...] = jnp.full(m_sc.shape, NEG_BIG, F32)
        l_sc[...] = jnp.zeros(l_sc.shape, F32)
        acc_sc[...] = jnp.zeros(acc_sc.shape, F32)

    q_bd = q_sc[...]
    ti = lax.broadcasted_iota(jnp.int32, (PAGE_SIZE, PAGE_SIZE), 0)
    tj = lax.broadcasted_iota(jnp.int32, (PAGE_SIZE, PAGE_SIZE), 1)
    later_and_all = jnp.concatenate([(ti > tj).astype(F32), jnp.ones((PAGE_SIZE, PAGE_SIZE), F32)],
                                    axis=1)
    lf_all = jnp.concatenate([lf_refs[g][0] for g in range(g_pages)], axis=0)
    sums = _dot(lf_all, later_and_all, HIGHEST)

    carry = carry_sc[...]
    roff = roff_sc[...]
    scores = []
    for g in range(g_pages):
        page_sums = sums[g * N_HEADS:(g + 1) * N_HEADS, :]
        bias = jnp.concatenate([page_sums[:, :PAGE_SIZE] + carry] * n_new, axis=0)
        carry = carry + page_sums[:, PAGE_SIZE:]
        k_t = k_refs[g][0].reshape(WIDTH, PAGE_SIZE).astype(BF16)
        scores.append(_dot(q_bd, k_t) + bias + roff)
    carry_sc[...] = carry

    m = m_sc[...]
    m_new = m
    for s in scores:
        m_new = jnp.maximum(m_new, jnp.max(s, axis=-1, keepdims=True))
    alpha = jnp.exp(m - m_new)
    l_new = alpha * l_sc[...]
    acc = alpha * acc_sc[...]
    for g, s in enumerate(scores):
        pe = jnp.exp(s - m_new)
        l_new = l_new + jnp.sum(pe, axis=-1, keepdims=True)
        v_t = v_refs[g][0].reshape(WIDTH, PAGE_SIZE).astype(BF16)
        acc = acc + _dot_nt(pe.astype(BF16), v_t)
    m_sc[...] = m_new
    l_sc[...] = l_new
    acc_sc[...] = acc

    @pl.when(step == pl.num_programs(1) - 1)
    def _():
        s = _dot_nt(q_bd, kn_ref[0]) + _dot_nt(cqm_sc[...], ckn_ref[0])
        row_tok = lax.broadcasted_iota(jnp.int32, (rows, 1), 0) >> (N_HEADS.bit_length() - 1)
        s = jnp.where(lax.broadcasted_iota(jnp.int32, (1, n_new), 1) <= row_tok, s, NEG_BIG)
        m_fin = jnp.maximum(m_new, jnp.max(s, axis=-1, keepdims=True))
        a_fin = jnp.exp(m_new - m_fin)
        pe = jnp.exp(s - m_fin)
        l_fin = a_fin * l_new + jnp.sum(pe, axis=-1, keepdims=True)
        out = (a_fin * acc + _dot(pe.astype(BF16), vn_ref[0])) / l_fin
        own = (lax.broadcasted_iota(jnp.int32, (N_HEADS, WIDTH), 0)
               == lax.broadcasted_iota(jnp.int32, (N_HEADS, WIDTH), 1) >> HEAD_SHIFT)
        o_ref[0] = jnp.concatenate(
            [jnp.sum(jnp.where(own, out[i * N_HEADS:(i + 1) * N_HEADS, :], 0.0), axis=0, keepdims=True)
             for i in range(n_new)], axis=0).astype(BF16)


def _fox_sample(page_table, q, cq, k_new, ck_new, v_new, cache_k, cache_v, cache_logf):
    n_seq, n_new, _ = q.shape
    n_pages = page_table.shape[1]
    g_pages = PAGES_PER_STEP
    rows = N_HEADS * n_new

    def per_seq(width):
        return pl.BlockSpec((1, n_new, width), lambda b, j, pt: (b, 0, 0))

    def page_spec(g, arr):
        block = (1,) + arr.shape[1:]
        return pl.BlockSpec(block, lambda b, j, pt: (pt[b, n_pages - 1 - (j * g_pages + g)],)
                            + (0,) * (arr.ndim - 1))

    grid_spec = pltpu.PrefetchScalarGridSpec(
        num_scalar_prefetch=1,
        grid=(n_seq, n_pages // g_pages),
        in_specs=([per_seq(WIDTH), per_seq(LANES), per_seq(WIDTH), per_seq(LANES), per_seq(WIDTH)]
                  + [page_spec(g, cache_k) for g in range(g_pages)]
                  + [page_spec(g, cache_v) for g in range(g_pages)]
                  + [page_spec(g, cache_logf) for g in range(g_pages)]),
        out_specs=pl.BlockSpec((1, n_new, WIDTH), lambda b, j, pt: (b, 0, 0)),
        scratch_shapes=[pltpu.VMEM((rows, WIDTH), BF16), pltpu.VMEM((rows, LANES), BF16),
                        pltpu.VMEM((rows, 1), F32), pltpu.VMEM((N_HEADS, PAGE_SIZE), F32),
                        pltpu.VMEM((rows, 1), F32), pltpu.VMEM((rows, 1), F32),
                        pltpu.VMEM((rows, WIDTH), F32)],
    )
    return pl.pallas_call(
        functools.partial(_fox_sample_kernel, n_new=n_new),
        grid_spec=grid_spec,
        out_shape=jax.ShapeDtypeStruct((n_seq, n_new, WIDTH), BF16),
        compiler_params=pltpu.CompilerParams(
            dimension_semantics=("arbitrary", "arbitrary"), vmem_limit_bytes=VMEM_LIMIT),
        name="fox_sample",
    )(page_table, q, cq, k_new, ck_new, v_new,
      *([cache_k] * g_pages), *([cache_v] * g_pages), *([cache_logf] * g_pages))


def _wkv_kernel(r_ref, lw_ref, kr_ref, vr_ref, a_ref, g_ref, s0_ref,
                kk_ref, ka_ref, rk_ref, gg_ref, gb_ref, o_ref, so_ref, s_sc):
    @pl.when(pl.program_id(1) == 0)
    def _():
        s_sc[...] = s0_ref[0]

    C = r_ref.shape[1]
    ri = lax.broadcasted_iota(jnp.int32, (C, C), 0)
    ci = lax.broadcasted_iota(jnp.int32, (C, C), 1)
    incl = ri >= ci
    strict = ri > ci
    eye = (ri == ci).astype(F32)
    lcum = _dot(incl.astype(F32), lw_ref[0], HIGHEST)
    levels = max(1, (C - 1).bit_length())

    heads = range(N_HEADS)
    hd = []
    for pair in range(N_HEADS // 2):
        lanes = slice(pair * LANES, (pair + 1) * LANES)
        blocks = [ref[0, :, lanes] for ref in (r_ref, lw_ref, kr_ref, vr_ref, a_ref, g_ref)]
        blocks.append(lcum[:, lanes])
        hd.append([x[:, :HEAD_DIM] for x in blocks])
        hd.append([pltpu.roll(x, HEAD_DIM, 1)[:, :HEAD_DIM] for x in blocks])

    xs, bts, kts, bgs, kgs, vbs, kps, e_ends = [], [], [], [], [], [], [], []
    for h in heads:
        rh, lwh, krh, vh, ah, gh, lc = hd[h]
        kkr = krh * kk_ref[h:h + 1, :]
        norm = jnp.sqrt(jnp.sum(kkr * kkr, axis=-1, keepdims=True))
        kk = kkr / jnp.maximum(norm, 1e-12)
        kp = krh * (1.0 + (ah - 1.0) * ka_ref[h:h + 1, :])
        bv = kk * ah
        e_neg = jnp.exp(-lc)
        l_end = lc[C - 1:C, :]
        e_rem = jnp.exp(l_end - lc)
        rt = rh * jnp.exp(lc)
        at = -kk * jnp.exp(lc - lwh)
        xs.append(jnp.concatenate([at, rt], axis=0).astype(BF16))
        bts.append((bv * e_neg).astype(BF16))
        kts.append((kp * e_neg).astype(BF16))
        bgs.append((bv * e_rem).astype(BF16))
        kgs.append((kp * e_rem).astype(BF16))
        vbs.append(vh.astype(BF16))
        kps.append(kp)
        e_ends.append(jnp.exp(l_end))

    sc_b = [_dot_nt(xs[h], bts[h]) for h in heads]
    sc_k = [_dot_nt(xs[h], kts[h]) for h in heads]
    a_ab = [jnp.where(strict, sc_b[h][:C], 0.0) for h in heads]
    a_ak = [jnp.where(strict, sc_k[h][:C], 0.0).astype(BF16) for h in heads]
    a_rb = [jnp.where(incl, sc_b[h][C:], 0.0).astype(BF16) for h in heads]
    a_rk = [jnp.where(incl, sc_k[h][C:], 0.0).astype(BF16) for h in heads]

    tinv = [eye + a_ab[h] for h in heads]
    if levels > 1:
        lb = [a_ab[h].astype(BF16) for h in heads]
        lpow = [_dot(lb[h], lb[h]) for h in heads]
        for level in range(1, levels):
            qb = [lpow[h].astype(BF16) for h in heads]
            if level == levels - 1:
                tinv = [tinv[h] + _dot(tinv[h].astype(BF16), qb[h]) for h in heads]
            else:
                both = [_dot(jnp.concatenate([tinv[h], lpow[h]], axis=0).astype(BF16), qb[h])
                        for h in heads]
                tinv = [tinv[h] + both[h][:C] for h in heads]
                lpow = [both[h][C:] for h in heads]
    tb = [tinv[h].astype(BF16) for h in heads]

    s_old = [s_sc[h] for h in heads]
    ps = [_dot_nt(xs[h], s_old[h].astype(BF16)) for h in heads]
    w1 = [(ps[h][:C] + _dot(a_ak[h], vbs[h])).astype(BF16) for h in heads]
    ub = [_dot(tb[h], w1[h]).astype(BF16) for h in heads]
    ys = [ps[h][C:] + _dot(a_rb[h], ub[h]) + _dot(a_rk[h], vbs[h]) for h in heads]
    for h in heads:
        s_sc[h] = s_old[h] * e_ends[h] + _dot_tn(ub[h], bgs[h]) + _dot_tn(vbs[h], kgs[h])

    outs = []
    for h in heads:
        rh, _, _, vh, _, gh, _ = hd[h]
        y = ys[h]
        mean = jnp.mean(y, axis=-1, keepdims=True)
        yc = y - mean
        var = jnp.mean(yc * yc, axis=-1, keepdims=True)
        yn = yc * lax.rsqrt(var + GN_EPS) * gg_ref[h:h + 1, :] + gb_ref[h:h + 1, :]
        bonus = jnp.sum(rh * kps[h] * rk_ref[h:h + 1, :], axis=-1, keepdims=True) * vh
        outs.append((yn + bonus) * gh)
    for pair in range(N_HEADS // 2):
        o_ref[0, :, pair * LANES:(pair + 1) * LANES] = jnp.concatenate(
            outs[2 * pair:2 * pair + 2], axis=1).astype(BF16)

    @pl.when(pl.program_id(1) == pl.num_programs(1) - 1)
    def _():
        so_ref[0] = s_sc[...]


def _wkv(r, lw, kr, vr, a, g, s0, params, chunk):
    batch, seq, _ = r.shape
    shared_state = s0.shape[0] == 1
    tok = pl.BlockSpec((1, chunk, WIDTH), lambda b, c: (b, c, 0))
    state_in = pl.BlockSpec((1, N_HEADS, HEAD_DIM, HEAD_DIM),
                            (lambda b, c: (0, 0, 0, 0)) if shared_state else (lambda b, c: (b, 0, 0, 0)))
    state_out = pl.BlockSpec((1, N_HEADS, HEAD_DIM, HEAD_DIM), lambda b, c: (b, 0, 0, 0))
    return pl.pallas_call(
        _wkv_kernel,
        grid=(batch, seq // chunk),
        in_specs=[tok] * 6 + [state_in] + [_const_spec(p) for p in params],
        out_specs=[tok, state_out],
        out_shape=[jax.ShapeDtypeStruct((batch, seq, WIDTH), BF16),
                   jax.ShapeDtypeStruct((batch, N_HEADS, HEAD_DIM, HEAD_DIM), F32)],
        scratch_shapes=[pltpu.VMEM((N_HEADS, HEAD_DIM, HEAD_DIM), F32)],
        compiler_params=pltpu.CompilerParams(
            dimension_semantics=("arbitrary", "arbitrary"), vmem_limit_bytes=VMEM_LIMIT),
        name="wkv",
    )(r, lw, kr, vr, a, g, s0, *params)


def _out_ffn_kernel(x_ref, att_ref, rw_ref, woa_ref, wor_ref, ln2_ref, wg_ref, wu_ref, wd_ref,
                    lnf_ref, y_ref):
    h = x_ref[...] + _dot(att_ref[...], woa_ref[...]) + _dot(rw_ref[...], wor_ref[...])
    hb = _rms(h, ln2_ref[...]).astype(BF16)
    ffn = jnp.zeros(h.shape, F32)
    d_ff = wg_ref.shape[1]
    for c0 in range(0, d_ff, FF_CHUNK):
        gate = _dot(hb, wg_ref[:, c0:c0 + FF_CHUNK])
        up = _dot(hb, wu_ref[:, c0:c0 + FF_CHUNK])
        act = gate * _sigmoid(gate) * up
        ffn = ffn + _dot(act.astype(BF16), wd_ref[c0:c0 + FF_CHUNK, :])
    y_ref[...] = _rms(h + ffn, lnf_ref[...])


def _out_ffn(x, att, rw, weights):
    rows = x.shape[0]
    tile = min(ROW_TILE, rows)
    tok = lambda width: pl.BlockSpec((tile, width), lambda i: (i, 0))
    return pl.pallas_call(
        _out_ffn_kernel,
        grid=(rows // tile,),
        in_specs=[tok(D_MODEL), tok(WIDTH), tok(WIDTH)] + [_const_spec(w) for w in weights],
        out_specs=tok(D_MODEL),
        out_shape=jax.ShapeDtypeStruct((rows, D_MODEL), F32),
        compiler_params=pltpu.CompilerParams(
            dimension_semantics=("arbitrary",), vmem_limit_bytes=VMEM_LIMIT),
        name="out_ffn",
    )(x, att, rw, *weights)


def _bias_selectors():
    part = jnp.arange(3)[:, None, None]
    src = jnp.arange(LANES)[None, :, None]
    dst = jnp.arange(LANES)[None, None, :]
    is_head = src < N_HEADS
    q_sel = (is_head & (dst == src * BIAS_GROUP + part)).astype(F32)
    k_sel = -(is_head & (dst == src * BIAS_GROUP + 3 + part)).astype(F32)
    sel = jnp.concatenate([q_sel, k_sel], axis=2).reshape(3 * LANES, 2 * LANES)
    slot = jnp.arange(LANES) % BIAS_GROUP
    ones = jnp.concatenate([(slot >= 3) & (slot < 6), slot < 3]).astype(F32)[None, :]
    return sel.astype(BF16), ones


def kernel(x_prompt, x_sample, cache_k, cache_v, cache_logf, state_wkv, state_shift, page_table,
           meta_tokens, ln1_g, w_in, b_f, mu_shift, w0, w2, a0, a2, g2, k_k, k_a, r_k,
           gn_g, gn_b, w_out, ln2_g, w_gate, w_up, w_down, lnf_g):
    batch, seq, _ = x_prompt.shape
    n_seq, n_new, _ = x_sample.shape
    fox_cols = 3 * WIDTH + N_HEADS

    w_in0 = w_in[0]
    wqkv = w_in0[:, :3 * WIDTH].astype(BF16)
    wf = jnp.pad(w_in0[:, 3 * WIDTH:fox_cols], ((0, 0), (0, LANES - N_HEADS))).astype(BF16)
    bf = jnp.pad(b_f[0], (0, LANES - N_HEADS))[None, :]
    wrw = w_in0[:, fox_cols:].astype(BF16)
    sel, ones = _bias_selectors()
    w2p = jnp.concatenate([w2[0], jnp.zeros((ICLR_LORA, WIDTH), F32)], axis=0).astype(BF16)
    a2p = jnp.concatenate([jnp.zeros((DECAY_LORA, WIDTH), F32), a2[0]], axis=0).astype(BF16)
    inproj_w = (ln1_g[0][None, :], wqkv, wf, bf, wrw, mu_shift[0][None, :], sel, ones,
                w2p, a2p, g2[0].astype(BF16), w0[0][None, :], a0[0][None, :])
    per_head = lambda t: t.reshape(N_HEADS, HEAD_DIM)
    wkv_p = (per_head(k_k[0]), per_head(k_a[0]), r_k[0], per_head(gn_g[0]), per_head(gn_b[0]))
    ffn_w = (w_out[0][:WIDTH].astype(BF16), w_out[0][WIDTH:].astype(BF16), ln2_g[0][None, :],
             w_gate[0].astype(BF16), w_up[0].astype(BF16), w_down[0].astype(BF16), lnf_g[None, :])

    zrow = jnp.zeros((1, 1, RWKV_COLS), F32)
    zc = jnp.zeros((1, 1, LANES), F32)
    mo = _inproj_long(meta_tokens[None], zrow, zc, inproj_w, N_META)
    (_, mk, mkb, mv, mvb, mlf, _, mck, mr, mlw, mkr, mvr, ma, mg, _, mprow, mclast) = mo
    zstate = jnp.zeros((1, N_HEADS, HEAD_DIM, HEAD_DIM), F32)
    _, s_meta = _wkv(mr, mlw, mkr, mvr, ma, mg, zstate, wkv_p, N_META)

    po = _inproj_long(x_prompt, mprow, mclast, inproj_w, ROW_TILE)
    (pq, pk, pkb, pv, pvb, plf, pcq, pck, pr, plw, pkr, pvr, pa, pg, pxl, _, _) = po
    att_p = _fox_prompt(pq, pcq, mkb, mck, mvb, pkb, pck, pvb)
    rw_p, wkv_prompt = _wkv(pr, plw, pkr, pvr, pa, pg, s_meta, wkv_p, WKV_CHUNK)
    y_prompt = _out_ffn(x_prompt.reshape(batch * seq, D_MODEL), att_p.reshape(batch * seq, WIDTH),
                        rw_p.reshape(batch * seq, WIDTH), ffn_w).reshape(batch, seq, D_MODEL)

    rows_s = n_seq * n_new
    xprev = jnp.repeat(state_shift[0], n_new, axis=0)[None]
    so = _inproj_packed(x_sample.reshape(1, rows_s, D_MODEL), xprev, inproj_w, n_new)
    (sq, sk, skb, sv, svb, slf, scq, sck, sr, slw, skr, svr, sa, sg, sxn) = so
    seqs = lambda t: t.reshape(n_seq, n_new, t.shape[-1])
    att_s = _fox_sample(page_table, seqs(sq), seqs(scq), seqs(skb), seqs(sck), seqs(svb),
                        jnp.transpose(cache_k[0], (0, 2, 3, 1)), jnp.transpose(cache_v[0], (0, 2, 3, 1)),
                        jnp.transpose(cache_logf[0], (0, 2, 1)))
    rw_s, wkv_sample = _wkv(seqs(sr), seqs(slw), seqs(skr), seqs(svr), seqs(sa), seqs(sg),
                            state_wkv[0], wkv_p, n_new)
    y_sample = _out_ffn(x_sample.reshape(rows_s, D_MODEL), att_s.reshape(rows_s, WIDTH),
                        rw_s.reshape(rows_s, WIDTH), ffn_w).reshape(n_seq, n_new, D_MODEL)

    def with_meta(meta, main):
        return jnp.concatenate([jnp.broadcast_to(meta, (batch,) + meta.shape[1:]), main], axis=1)

    heads = lambda t: t.reshape(t.shape[:-1] + (N_HEADS, HEAD_DIM))
    k_prompt = heads(with_meta(mk, pk))[None]
    v_prompt = heads(with_meta(mv, pv))[None]
    logf_prompt = with_meta(mlf, plf)[None]
    shift_prompt = pxl.reshape(1, batch, D_MODEL)
    k_sample = heads(seqs(sk))[None]
    v_sample = heads(seqs(sv))[None]
    logf_sample = seqs(slf)[None]
    shift_sample = seqs(sxn)[:, -1][None]
    return (y_prompt, y_sample, k_prompt, v_prompt, logf_prompt, wkv_prompt[None], shift_prompt,
            k_sample, v_sample, logf_sample, wkv_sample[None], shift_sample)
```

```python
import functools

import jax
import jax.numpy as jnp
from jax import lax
from jax.experimental import pallas as pl
from jax.experimental.pallas import tpu as pltpu

D_MODEL = 1024
N_META = 16
HEAD_DIM = 64
N_HEADS = 8
WIDTH = N_HEADS * HEAD_DIM
PAGE_SIZE = 128
DECAY_LORA = 64
ICLR_LORA = 64
GATE_LORA = 128
RWKV_COLS = 3 * WIDTH + DECAY_LORA + ICLR_LORA + GATE_LORA
RMS_EPS = 1e-6
GN_EPS = 64e-5
NEG_BIG = -1e30

LANES = 128
BIAS_GROUP = 16
HEAD_SHIFT = HEAD_DIM.bit_length() - 1
GROUP_SHIFT = BIAS_GROUP.bit_length() - 1
VMEM_LIMIT = 56 * 1024 * 1024

ROW_TILE = 512
FFN_ROW_TILE = 512
ATT_TILE = 1024
ATT_K_TILE = 256
ATT_SUB = 128
ATT_GROUP = 8
WKV_CHUNK = 64
WKV_PROMPT_SEQS = 2
WKV_SAMPLE_SEQS = 4
PAGES_PER_STEP = 16
FF_CHUNK = 256

BF16 = jnp.bfloat16
F32 = jnp.float32
HIGHEST = lax.Precision.HIGHEST


def _dot(a, b, precision=None):
    return jnp.dot(a, b, preferred_element_type=F32, precision=precision)


def _dot_nt(a, b, precision=None):
    return lax.dot_general(a, b, (((1,), (1,)), ((), ())), preferred_element_type=F32,
                           precision=precision)


def _dot_tn(a, b, precision=None):
    return lax.dot_general(a, b, (((0,), (0,)), ((), ())), preferred_element_type=F32,
                           precision=precision)


def _rms(x, g):
    return x * lax.rsqrt(jnp.mean(x * x, axis=-1, keepdims=True) + RMS_EPS) * g


def _softplus(z):
    return jnp.maximum(z, 0.0) + jnp.log1p(jnp.exp(-jnp.abs(z)))


def _sigmoid(z):
    return 1.0 / (1.0 + jnp.exp(-z))


def _mask_bf16(cond):
    return cond.astype(F32).astype(BF16)


def _split3(c):
    hi = c.astype(BF16)
    r1 = c - hi.astype(F32)
    mid = r1.astype(BF16)
    lo = (r1 - mid.astype(F32)).astype(BF16)
    return hi, mid, lo


def _inproj_outputs(xn, qkv, logf, c, p, p_prev, w, outs):
    (mu_ref, sel_ref, ones_ref, w2_ref, a2_ref, g2_ref, w0_ref, a0_ref) = w
    (q_ref, k_ref, kb_ref, v_ref, vb_ref, lf_ref, cq_ref, ck_ref,
     r_ref, lw_ref, kr_ref, vr_ref, a_ref, g_ref) = outs
    q_ref[0] = (qkv[:, :WIDTH] * (HEAD_DIM ** -0.5)).astype(BF16)
    k = qkv[:, WIDTH:2 * WIDTH]
    k_ref[0] = k
    kb_ref[0] = k.astype(BF16)
    v = qkv[:, 2 * WIDTH:]
    v_ref[0] = v
    vb_ref[0] = v.astype(BF16)
    lf_ref[0] = logf[:, :N_HEADS]
    cparts = jnp.concatenate(_split3(c), axis=1)
    cc = _dot(cparts, sel_ref[...]) + ones_ref[...]
    cq_ref[0] = cc[:, :LANES].astype(BF16)
    ck_ref[0] = cc[:, LANES:].astype(BF16)
    rw = p + (p_prev - p) * mu_ref[...]
    r_ref[0] = rw[:, :WIDTH]
    kr_ref[0] = rw[:, WIDTH:2 * WIDTH]
    vr_ref[0] = rw[:, 2 * WIDTH:3 * WIDTH]
    z = rw[:, 3 * WIDTH:3 * WIDTH + LANES]
    gl = rw[:, 3 * WIDTH + LANES:]
    w_log = -_softplus(-(w0_ref[...] + _dot(jnp.tanh(z).astype(BF16), w2_ref[...]))) - 0.5
    lw_ref[0] = -jnp.exp(w_log)
    a_ref[0] = _sigmoid(a0_ref[...] + _dot(z.astype(BF16), a2_ref[...]))
    g_ref[0] = _dot(_sigmoid(gl).astype(BF16), g2_ref[...])


def _inproj_long_kernel(x_ref, prow_ref, c0_ref, ln_ref, wqkv_ref, wf_ref, bf_ref, wrw_ref,
                        mu_ref, sel_ref, ones_ref, w2_ref, a2_ref, g2_ref, w0_ref, a0_ref,
                        q_ref, k_ref, kb_ref, v_ref, vb_ref, lf_ref, cq_ref, ck_ref,
                        r_ref, lw_ref, kr_ref, vr_ref, a_ref, g_ref, xl_ref, pl_ref, cl_ref,
                        pcar_ref, ccar_ref):
    @pl.when(pl.program_id(1) == 0)
    def _():
        pcar_ref[...] = prow_ref[0]
        ccar_ref[...] = c0_ref[0]

    rows = x_ref.shape[1]
    xn = _rms(x_ref[0], ln_ref[...])
    xb = xn.astype(BF16)
    qkv = _dot(xb, wqkv_ref[...])
    logf = -_softplus(-(_dot(xb, wf_ref[...]) + bf_ref[...]))
    ri = lax.broadcasted_iota(jnp.int32, (rows, rows), 0)
    ci = lax.broadcasted_iota(jnp.int32, (rows, rows), 1)
    c = _dot((ri >= ci).astype(F32), logf, HIGHEST) + ccar_ref[...]
    ccar_ref[...] = c[rows - 1:rows, :]
    p = _dot(xb, wrw_ref[...])
    first = lax.broadcasted_iota(jnp.int32, p.shape, 0) == 0
    p_prev = jnp.where(first, pcar_ref[...], pltpu.roll(p, 1, 0))
    pcar_ref[...] = p[rows - 1:rows, :]
    xl_ref[0] = xn[rows - 1:rows, :]
    pl_ref[0] = p[rows - 1:rows, :]
    cl_ref[0] = c[rows - 1:rows, :]
    _inproj_outputs(xn, qkv, logf, c, p, p_prev,
                    (mu_ref, sel_ref, ones_ref, w2_ref, a2_ref, g2_ref, w0_ref, a0_ref),
                    (q_ref, k_ref, kb_ref, v_ref, vb_ref, lf_ref, cq_ref, ck_ref,
                     r_ref, lw_ref, kr_ref, vr_ref, a_ref, g_ref))


def _inproj_packed_kernel(x_ref, xprev_ref, ln_ref, wqkv_ref, wf_ref, bf_ref, wrw_ref,
                          mu_ref, sel_ref, ones_ref, w2_ref, a2_ref, g2_ref, w0_ref, a0_ref,
                          q_ref, k_ref, kb_ref, v_ref, vb_ref, lf_ref, cq_ref, ck_ref,
                          r_ref, lw_ref, kr_ref, vr_ref, a_ref, g_ref, xn_ref, *, seq_len):
    rows = x_ref.shape[1]
    shift = seq_len.bit_length() - 1
    xn = _rms(x_ref[0], ln_ref[...])
    xn_ref[0] = xn
    xb = xn.astype(BF16)
    qkv = _dot(xb, wqkv_ref[...])
    logf = -_softplus(-(_dot(xb, wf_ref[...]) + bf_ref[...]))
    ri = lax.broadcasted_iota(jnp.int32, (rows, rows), 0)
    ci = lax.broadcasted_iota(jnp.int32, (rows, rows), 1)
    same_seq = (ri >> shift) == (ci >> shift)
    c = _dot(((ri >= ci) & same_seq).astype(F32), logf, HIGHEST)
    p = _dot(xb, wrw_ref[...])
    p_first = _dot(xprev_ref[0].astype(BF16), wrw_ref[...])
    first = (lax.broadcasted_iota(jnp.int32, p.shape, 0) & (seq_len - 1)) == 0
    p_prev = jnp.where(first, p_first, pltpu.roll(p, 1, 0))
    _inproj_outputs(xn, qkv, logf, c, p, p_prev,
                    (mu_ref, sel_ref, ones_ref, w2_ref, a2_ref, g2_ref, w0_ref, a0_ref),
                    (q_ref, k_ref, kb_ref, v_ref, vb_ref, lf_ref, cq_ref, ck_ref,
                     r_ref, lw_ref, kr_ref, vr_ref, a_ref, g_ref))


def _const_spec(arr):
    return pl.BlockSpec(arr.shape, lambda *_: (0,) * arr.ndim)


def _inproj_out_shapes(batch, rows_total):
    def s(width, dtype):
        return jax.ShapeDtypeStruct((batch, rows_total, width), dtype)
    return [s(WIDTH, BF16), s(WIDTH, F32), s(WIDTH, BF16), s(WIDTH, F32), s(WIDTH, BF16),
            s(N_HEADS, F32), s(LANES, BF16), s(LANES, BF16)] + [s(WIDTH, F32)] * 6


def _inproj_out_specs(tile):
    def s(width):
        return pl.BlockSpec((1, tile, width), lambda b, i: (b, i, 0))
    return [s(WIDTH)] * 5 + [s(N_HEADS), s(LANES), s(LANES)] + [s(WIDTH)] * 6


def _inproj_long(x, prow, c0, weights, tile):
    batch, seq, _ = x.shape
    n_tiles = seq // tile
    row = lambda width: pl.BlockSpec((1, 1, width), lambda b, i: (b, 0, 0))
    shapes = _inproj_out_shapes(batch, seq) + [
        jax.ShapeDtypeStruct((batch, 1, D_MODEL), F32),
        jax.ShapeDtypeStruct((batch, 1, RWKV_COLS), F32),
        jax.ShapeDtypeStruct((batch, 1, LANES), F32)]
    specs = _inproj_out_specs(tile) + [row(D_MODEL), row(RWKV_COLS), row(LANES)]
    return pl.pallas_call(
        _inproj_long_kernel,
        grid=(batch, n_tiles),
        in_specs=[pl.BlockSpec((1, tile, D_MODEL), lambda b, i: (b, i, 0)),
                  _const_spec(prow), _const_spec(c0)] + [_const_spec(w) for w in weights],
        out_specs=specs,
        out_shape=shapes,
        scratch_shapes=[pltpu.VMEM((1, RWKV_COLS), F32), pltpu.VMEM((1, LANES), F32)],
        compiler_params=pltpu.CompilerParams(
            dimension_semantics=("arbitrary", "arbitrary"), vmem_limit_bytes=VMEM_LIMIT),
        name="inproj_long",
    )(x, prow, c0, *weights)


def _inproj_packed(x, xprev, weights, seq_len):
    _, rows, _ = x.shape
    full = pl.BlockSpec((1, rows, D_MODEL), lambda b, i: (0, 0, 0))
    return pl.pallas_call(
        functools.partial(_inproj_packed_kernel, seq_len=seq_len),
        grid=(1, 1),
        in_specs=[full, full] + [_const_spec(w) for w in weights],
        out_specs=_inproj_out_specs(rows) + [full],
        out_shape=_inproj_out_shapes(1, rows) + [jax.ShapeDtypeStruct((1, rows, D_MODEL), F32)],
        compiler_params=pltpu.CompilerParams(
            dimension_semantics=("arbitrary", "arbitrary"), vmem_limit_bytes=VMEM_LIMIT),
        name="inproj_packed",
    )(x, xprev, *weights)


def _fox_prompt_kernel(q_ref, cq_ref, km_ref, ckm_ref, vm_ref, k_ref, ck_ref, v_ref, o_ref):
    pair = pl.program_id(1)
    qi = pl.program_id(2)
    tile = q_ref.shape[1]
    lane = lax.broadcasted_iota(jnp.int32, (1, LANES), 1)
    q2 = q_ref[0]
    cq = cq_ref[0]
    n_sub = tile // ATT_SUB
    lhs = []
    for hh in range(2):
        head_lanes = (lane >= hh * HEAD_DIM) & (lane < (hh + 1) * HEAD_DIM)
        g0 = (2 * pair + hh) * BIAS_GROUP
        group_lanes = (lane >= g0) & (lane < g0 + BIAS_GROUP)
        full = jnp.concatenate([q2 * _mask_bf16(head_lanes), cq * _mask_bf16(group_lanes)], axis=1)
        lhs.append([full[r * ATT_SUB:(r + 1) * ATT_SUB, :] for r in range(n_sub)])
    row_minus_col = (lax.broadcasted_iota(jnp.int32, (ATT_SUB, ATT_K_TILE), 0)
                     - lax.broadcasted_iota(jnp.int32, (ATT_SUB, ATT_K_TILE), 1))

    def block(carry, kk, vv, first_col):
        out = list(carry)
        live = [(hh, r) for hh in range(2) for r in range(n_sub)
                if first_col is None or first_col <= (r + 1) * ATT_SUB - 1]
        for g0 in range(0, len(live), ATT_GROUP):
            group = live[g0:g0 + ATT_GROUP]
            scores, m_news = [], []
            for hh, r in group:
                s = _dot_nt(lhs[hh][r], kk)
                if first_col is not None and first_col + kk.shape[0] - 1 > r * ATT_SUB:
                    s = jnp.where(row_minus_col >= first_col - r * ATT_SUB, s, NEG_BIG)
                scores.append(s)
            for (hh, r), s in zip(group, scores):
                m_news.append(jnp.maximum(out[hh * n_sub + r][0], jnp.max(s, axis=-1, keepdims=True)))
            for (hh, r), s, m_new in zip(group, scores, m_news):
                m, l, acc = out[hh * n_sub + r]
                alpha = jnp.exp(m - m_new)
                pe = jnp.exp(s - m_new)
                l_new = alpha * l + jnp.sum(pe, axis=-1, keepdims=True)
                acc_new = alpha * acc + _dot(pe.astype(BF16), vv)
                out[hh * n_sub + r] = (m_new, l_new, acc_new)
        return tuple(out)

    init = tuple((jnp.full((ATT_SUB, 1), NEG_BIG, F32), jnp.zeros((ATT_SUB, 1), F32),
                  jnp.zeros((ATT_SUB, LANES), F32)) for _ in range(2 * n_sub))
    carry = block(init, jnp.concatenate([km_ref[0], ckm_ref[0]], axis=1), vm_ref[0], None)

    def keys(j):
        start = pl.multiple_of(j * ATT_K_TILE, ATT_K_TILE)
        rows = pl.ds(start, ATT_K_TILE)
        return jnp.concatenate([k_ref[0, rows, :], ck_ref[0, rows, :]], axis=1), v_ref[0, rows, :]

    per_q = tile // ATT_K_TILE
    carry = lax.fori_loop(0, qi * per_q, lambda j, c: block(c, *keys(j), None), carry)
    for d in range(per_q):
        carry = block(carry, *keys(qi * per_q + d), d * ATT_K_TILE)
    outs = [jnp.concatenate([carry[hh * n_sub + r][2] / carry[hh * n_sub + r][1]
                             for r in range(n_sub)], axis=0) for hh in range(2)]
    first_head = lax.broadcasted_iota(jnp.int32, outs[0].shape, 1) < HEAD_DIM
    o_ref[0] = jnp.where(first_head, outs[0], outs[1]).astype(BF16)


def _fox_prompt(q, cq, k_meta, ck_meta, v_meta, k, ck, v):
    batch, seq, _ = q.shape
    n_pairs = N_HEADS // 2
    return pl.pallas_call(
        _fox_prompt_kernel,
        grid=(batch, n_pairs, seq // ATT_TILE),
        in_specs=[
            pl.BlockSpec((1, ATT_TILE, LANES), lambda b, p, i: (b, i, p)),
            pl.BlockSpec((1, ATT_TILE, LANES), lambda b, p, i: (b, i, 0)),
            pl.BlockSpec((1, N_META, LANES), lambda b, p, i: (0, 0, p)),
            pl.BlockSpec((1, N_META, LANES), lambda b, p, i: (0, 0, 0)),
            pl.BlockSpec((1, N_META, LANES), lambda b, p, i: (0, 0, p)),
            pl.BlockSpec((1, seq, LANES), lambda b, p, i: (b, 0, p)),
            pl.BlockSpec((1, seq, LANES), lambda b, p, i: (b, 0, 0)),
            pl.BlockSpec((1, seq, LANES), lambda b, p, i: (b, 0, p)),
        ],
        out_specs=pl.BlockSpec((1, ATT_TILE, LANES), lambda b, p, i: (b, i, p)),
        out_shape=jax.ShapeDtypeStruct((batch, seq, WIDTH), BF16),
        compiler_params=pltpu.CompilerParams(
            dimension_semantics=("arbitrary", "arbitrary", "arbitrary"),
            vmem_limit_bytes=VMEM_LIMIT),
        name="fox_prompt",
    )(q, cq, k_meta, ck_meta, v_meta, k, ck, v)


def _fox_sample_kernel(pt_ref, q_ref, cq_ref, kn_ref, ckn_ref, vn_ref, *rest, n_new):
    del pt_ref
    g_pages = PAGES_PER_STEP
    k_refs, v_refs, lf_refs = rest[:g_pages], rest[g_pages:2 * g_pages], rest[2 * g_pages:3 * g_pages]
    o_ref, q_sc, cqm_sc, roff_sc, carry_sc, m_sc, l_sc, acc_sc = rest[3 * g_pages:]
    step = pl.program_id(1)
    rows = N_HEADS * n_new
    row_head = lax.broadcasted_iota(jnp.int32, (rows, 1), 0) & (N_HEADS - 1)

    def per_head_rows(x):
        return jnp.concatenate([jnp.broadcast_to(x[i:i + 1, :], (N_HEADS, x.shape[1]))
                                for i in range(n_new)], axis=0)

    @pl.when(step == 0)
    def _():
        q = per_head_rows(q_ref[0].astype(F32))
        lane_head = lax.broadcasted_iota(jnp.int32, (1, WIDTH), 1) >> HEAD_SHIFT
        q_sc[...] = jnp.where(row_head == lane_head, q, 0.0).astype(BF16)
        cq = per_head_rows(cq_ref[0].astype(F32))
        lane = lax.broadcasted_iota(jnp.int32, (1, LANES), 1)
        cqm = jnp.where((lane >> GROUP_SHIFT) == row_head, cq, 0.0)
        cqm_sc[...] = cqm.astype(BF16)
        slot = lax.broadcasted_iota(jnp.int32, cqm.shape, 1) & (BIAS_GROUP - 1)
        roff_sc[...] = jnp.sum(jnp.where(slot < 3, cqm, 0.0), axis=-1, keepdims=True)
        carry_sc[...] = jnp.zeros(carry_sc.shape, F32)
        m_sc[...] = jnp.full(m_sc.shape, NEG_BIG, F32)
        l_sc[...] = jnp.zeros(l_sc.shape, F32)
        acc_sc[...] = jnp.zeros(acc_sc.shape, F32)

    q_bd = q_sc[...]
    ti = lax.broadcasted_iota(jnp.int32, (PAGE_SIZE, PAGE_SIZE), 0)
    tj = lax.broadcasted_iota(jnp.int32, (PAGE_SIZE, PAGE_SIZE), 1)
    later_and_all = jnp.concatenate([(ti > tj).astype(F32), jnp.ones((PAGE_SIZE, PAGE_SIZE), F32)],
                                    axis=1)
    lf_all = jnp.concatenate([lf_refs[g][0] for g in range(g_pages)], axis=0)
    sums = _dot(lf_all, later_and_all, HIGHEST)

    carry = carry_sc[...]
    roff = roff_sc[...]
    scores = []
    for g in range(g_pages):
        page_sums = sums[g * N_HEADS:(g + 1) * N_HEADS, :]
        bias = jnp.concatenate([page_sums[:, :PAGE_SIZE] + carry] * n_new, axis=0)
        carry = carry + page_sums[:, PAGE_SIZE:]
        k_t = k_refs[g][0].reshape(WIDTH, PAGE_SIZE).astype(BF16)
        scores.append(_dot(q_bd, k_t) + bias + roff)
    carry_sc[...] = carry

    m = m_sc[...]
    m_new = m
    for s in scores:
        m_new = jnp.maximum(m_new, jnp.max(s, axis=-1, keepdims=True))
    alpha = jnp.exp(m - m_new)
    l_new = alpha * l_sc[...]
    acc = alpha * acc_sc[...]
    for g, s in enumerate(scores):
        pe = jnp.exp(s - m_new)
        l_new = l_new + jnp.sum(pe, axis=-1, keepdims=True)
        v_t = v_refs[g][0].reshape(WIDTH, PAGE_SIZE).astype(BF16)
        acc = acc + _dot_nt(pe.astype(BF16), v_t)
    m_sc[...] = m_new
    l_sc[...] = l_new
    acc_sc[...] = acc

    @pl.when(step == pl.num_programs(1) - 1)
    def _():
        s = _dot_nt(q_bd, kn_ref[0]) + _dot_nt(cqm_sc[...], ckn_ref[0])
        row_tok = lax.broadcasted_iota(jnp.int32, (rows, 1), 0) >> (N_HEADS.bit_length() - 1)
        s = jnp.where(lax.broadcasted_iota(jnp.int32, (1, n_new), 1) <= row_tok, s, NEG_BIG)
        m_fin = jnp.maximum(m_new, jnp.max(s, axis=-1, keepdims=True))
        a_fin = jnp.exp(m_new - m_fin)
        pe = jnp.exp(s - m_fin)
        l_fin = a_fin * l_new + jnp.sum(pe, axis=-1, keepdims=True)
        out = (a_fin * acc + _dot(pe.astype(BF16), vn_ref[0])) / l_fin
        own = (lax.broadcasted_iota(jnp.int32, (N_HEADS, WIDTH), 0)
               == lax.broadcasted_iota(jnp.int32, (N_HEADS, WIDTH), 1) >> HEAD_SHIFT)
        o_ref[0] = jnp.concatenate(
            [jnp.sum(jnp.where(own, out[i * N_HEADS:(i + 1) * N_HEADS, :], 0.0), axis=0, keepdims=True)
             for i in range(n_new)], axis=0).astype(BF16)


def _fox_sample(page_table, q, cq, k_new, ck_new, v_new, cache_k, cache_v, cache_logf):
    n_seq, n_new, _ = q.shape
    n_pages = page_table.shape[1]
    g_pages = PAGES_PER_STEP
    rows = N_HEADS * n_new

    def per_seq(width):
        return pl.BlockSpec((1, n_new, width), lambda b, j, pt: (b, 0, 0))

    def page_spec(g, arr):
        block = (1,) + arr.shape[1:]
        return pl.BlockSpec(block, lambda b, j, pt: (pt[b, n_pages - 1 - (j * g_pages + g)],)
                            + (0,) * (arr.ndim - 1))

    grid_spec = pltpu.PrefetchScalarGridSpec(
        num_scalar_prefetch=1,
        grid=(n_seq, n_pages // g_pages),
        in_specs=([per_seq(WIDTH), per_seq(LANES), per_seq(WIDTH), per_seq(LANES), per_seq(WIDTH)]
                  + [page_spec(g, cache_k) for g in range(g_pages)]
                  + [page_spec(g, cache_v) for g in range(g_pages)]
                  + [page_spec(g, cache_logf) for g in range(g_pages)]),
        out_specs=pl.BlockSpec((1, n_new, WIDTH), lambda b, j, pt: (b, 0, 0)),
        scratch_shapes=[pltpu.VMEM((rows, WIDTH), BF16), pltpu.VMEM((rows, LANES), BF16),
                        pltpu.VMEM((rows, 1), F32), pltpu.VMEM((N_HEADS, PAGE_SIZE), F32),
                        pltpu.VMEM((rows, 1), F32), pltpu.VMEM((rows, 1), F32),
                        pltpu.VMEM((rows, WIDTH), F32)],
    )
    return pl.pallas_call(
        functools.partial(_fox_sample_kernel, n_new=n_new),
        grid_spec=grid_spec,
        out_shape=jax.ShapeDtypeStruct((n_seq, n_new, WIDTH), BF16),
        compiler_params=pltpu.CompilerParams(
            dimension_semantics=("arbitrary", "arbitrary"), vmem_limit_bytes=VMEM_LIMIT),
        name="fox_sample",
    )(page_table, q, cq, k_new, ck_new, v_new,
      *([cache_k] * g_pages), *([cache_v] * g_pages), *([cache_logf] * g_pages))


def _wkv_kernel(r_ref, lw_ref, kr_ref, vr_ref, a_ref, g_ref, s0_ref,
                kk_ref, ka_ref, rk_ref, gg_ref, gb_ref, o_ref, so_ref, s_sc):
    nb, C, _ = r_ref.shape
    n_pairs = N_HEADS // 2
    units = [(bb, pair) for bb in range(nb) for pair in range(n_pairs)]
    lo_state = ((lax.broadcasted_iota(jnp.int32, (LANES, LANES), 0) < HEAD_DIM)
                == (lax.broadcasted_iota(jnp.int32, (LANES, LANES), 1) < HEAD_DIM))

    @pl.when(pl.program_id(1) == 0)
    def _():
        zeros = jnp.zeros((HEAD_DIM, HEAD_DIM), F32)
        for bb, pair in units:
            sb = bb if s0_ref.shape[0] == nb else 0
            top = jnp.concatenate([s0_ref[sb, 2 * pair], zeros], axis=1)
            bottom = jnp.concatenate([zeros, s0_ref[sb, 2 * pair + 1]], axis=1)
            s_sc[bb, pair] = jnp.concatenate([top, bottom], axis=0)

    ri = lax.broadcasted_iota(jnp.int32, (C, C), 0)
    ci = lax.broadcasted_iota(jnp.int32, (C, C), 1)
    incl = ri >= ci
    strict = ri > ci
    eye = (ri == ci).astype(F32)
    lo = lax.broadcasted_iota(jnp.int32, (C, LANES), 1) < HEAD_DIM
    lo2 = lax.broadcasted_iota(jnp.int32, (2 * C, LANES), 1) < HEAD_DIM
    levels = max(1, (C - 1).bit_length())
    lcum = [_dot(incl.astype(F32), lw_ref[bb], HIGHEST) for bb in range(nb)]

    def per_head(x):
        first = jnp.sum(jnp.where(lo, x, 0.0), axis=-1, keepdims=True)
        second = jnp.sum(jnp.where(lo, 0.0, x), axis=-1, keepdims=True)
        return jnp.where(lo, first, second)

    xs, x_all, bts, kts, bgs, kgs, vbs, kps, e_ends = [], [], [], [], [], [], [], [], []
    for bb, pair in units:
        lanes = slice(pair * LANES, (pair + 1) * LANES)
        r2, lw2, kr2, a2 = (ref[bb, :, lanes] for ref in (r_ref, lw_ref, kr_ref, a_ref))
        lc = lcum[bb][:, lanes]
        kkr = kr2 * kk_ref[:, lanes]
        kk = kkr / jnp.maximum(jnp.sqrt(per_head(kkr * kkr)), 1e-12)
        kp = kr2 * (1.0 + (a2 - 1.0) * ka_ref[:, lanes])
        bv = kk * a2
        e_neg = jnp.exp(-lc)
        l_end = lc[C - 1:C, :]
        e_rem = jnp.exp(l_end - lc)
        x2 = jnp.concatenate([-kk * jnp.exp(lc - lw2), r2 * jnp.exp(lc)], axis=0)
        x_all.append(x2.astype(BF16))
        xs.append((jnp.where(lo2, x2, 0.0).astype(BF16), jnp.where(lo2, 0.0, x2).astype(BF16)))
        bts.append((bv * e_neg).astype(BF16))
        kts.append((kp * e_neg).astype(BF16))
        bgs.append((bv * e_rem).astype(BF16))
        kgs.append((kp * e_rem).astype(BF16))
        vbs.append(vr_ref[bb, :, lanes].astype(BF16))
        kps.append(kp)
        e_ends.append(jnp.exp(l_end))

    chains = [(u, hh) for u in range(len(units)) for hh in range(2)]
    sc_b = [_dot_nt(xs[u][hh], bts[u]) for u, hh in chains]
    sc_k = [_dot_nt(xs[u][hh], kts[u]) for u, hh in chains]
    n_ch = range(len(chains))
    a_ab = [jnp.where(strict, sc_b[i][:C], 0.0) for i in n_ch]
    a_ak = [jnp.where(strict, sc_k[i][:C], 0.0).astype(BF16) for i in n_ch]
    a_rb = [jnp.where(incl, sc_b[i][C:], 0.0).astype(BF16) for i in n_ch]
    a_rk = [jnp.where(incl, sc_k[i][C:], 0.0).astype(BF16) for i in n_ch]

    tinv = [eye + a_ab[i] for i in n_ch]
    if levels > 1:
        lb = [a_ab[i].astype(BF16) for i in n_ch]
        lpow = [_dot(lb[i], lb[i]) for i in n_ch]
        for level in range(1, levels):
            qb = [lpow[i].astype(BF16) for i in n_ch]
            if level == levels - 1:
                tinv = [tinv[i] + _dot(tinv[i].astype(BF16), qb[i]) for i in n_ch]
            else:
                both = [_dot(jnp.concatenate([tinv[i], lpow[i]], axis=0).astype(BF16), qb[i])
                        for i in n_ch]
                tinv = [tinv[i] + both[i][:C] for i in n_ch]
                lpow = [both[i][C:] for i in n_ch]
    tb = [tinv[i].astype(BF16) for i in n_ch]

    def both_heads(u, f):
        return jnp.where(lo, f(2 * u), f(2 * u + 1))

    n_u = range(len(units))
    s_old = [s_sc[bb, pair] for bb, pair in units]
    ps = [_dot_nt(x_all[u], s_old[u].astype(BF16)) for u in n_u]
    w1 = [(ps[u][:C] + both_heads(u, lambda i, u=u: _dot(a_ak[i], vbs[u]))).astype(BF16) for u in n_u]
    ub = [both_heads(u, lambda i, u=u: _dot(tb[i], w1[u])).astype(BF16) for u in n_u]
    ys = [ps[u][C:] + both_heads(u, lambda i, u=u: _dot(a_rb[i], ub[u]) + _dot(a_rk[i], vbs[u]))
          for u in n_u]
    for u, (bb, pair) in enumerate(units):
        cross = _dot_tn(ub[u], bgs[u]) + _dot_tn(vbs[u], kgs[u])
        s_sc[bb, pair] = s_old[u] * e_ends[u] + jnp.where(lo_state, cross, 0.0)

    for u, (bb, pair) in enumerate(units):
        lanes = slice(pair * LANES, (pair + 1) * LANES)
        y = ys[u]
        yc = y - per_head(y) * (1.0 / HEAD_DIM)
        var = per_head(yc * yc) * (1.0 / HEAD_DIM)
        yn = yc * lax.rsqrt(var + GN_EPS) * gg_ref[:, lanes] + gb_ref[:, lanes]
        bonus = per_head(r_ref[bb, :, lanes] * kps[u] * rk_ref[:, lanes]) * vr_ref[bb, :, lanes]
        o_ref[bb, :, lanes] = ((yn + bonus) * g_ref[bb, :, lanes]).astype(BF16)

    @pl.when(pl.program_id(1) == pl.num_programs(1) - 1)
    def _():
        for bb, pair in units:
            s2 = s_sc[bb, pair]
            so_ref[bb, 2 * pair] = s2[:HEAD_DIM, :HEAD_DIM]
            so_ref[bb, 2 * pair + 1] = pltpu.roll(s2[HEAD_DIM:, :], HEAD_DIM, 1)[:, :HEAD_DIM]


def _wkv(r, lw, kr, vr, a, g, s0, params, chunk, batch_tile):
    batch, seq, _ = r.shape
    shared_state = s0.shape[0] == 1
    tok = pl.BlockSpec((batch_tile, chunk, WIDTH), lambda b, c: (b, c, 0))
    state_out = pl.BlockSpec((batch_tile, N_HEADS, HEAD_DIM, HEAD_DIM), lambda b, c: (b, 0, 0, 0))
    state_in = (pl.BlockSpec((1, N_HEADS, HEAD_DIM, HEAD_DIM), lambda b, c: (0, 0, 0, 0))
                if shared_state else state_out)
    return pl.pallas_call(
        _wkv_kernel,
        grid=(batch // batch_tile, seq // chunk),
        in_specs=[tok] * 6 + [state_in] + [_const_spec(p) for p in params],
        out_specs=[tok, state_out],
        out_shape=[jax.ShapeDtypeStruct((batch, seq, WIDTH), BF16),
                   jax.ShapeDtypeStruct((batch, N_HEADS, HEAD_DIM, HEAD_DIM), F32)],
        scratch_shapes=[pltpu.VMEM((batch_tile, N_HEADS // 2, LANES, LANES), F32)],
        compiler_params=pltpu.CompilerParams(
            dimension_semantics=("arbitrary", "arbitrary"), vmem_limit_bytes=VMEM_LIMIT),
        name="wkv",
    )(r, lw, kr, vr, a, g, s0, *params)


def _out_ffn_kernel(x_ref, att_ref, rw_ref, woa_ref, wor_ref, ln2_ref, wg_ref, wu_ref, wd_ref,
                    lnf_ref, y_ref):
    h = x_ref[...] + _dot(att_ref[...], woa_ref[...]) + _dot(rw_ref[...], wor_ref[...])
    hb = _rms(h, ln2_ref[...]).astype(BF16)
    ffn = jnp.zeros(h.shape, F32)
    d_ff = wg_ref.shape[1]
    for c0 in range(0, d_ff, FF_CHUNK):
        gate = _dot(hb, wg_ref[:, c0:c0 + FF_CHUNK])
        up = _dot(hb, wu_ref[:, c0:c0 + FF_CHUNK])
        act = gate * _sigmoid(gate) * up
        ffn = ffn + _dot(act.astype(BF16), wd_ref[c0:c0 + FF_CHUNK, :])
    y_ref[...] = _rms(h + ffn, lnf_ref[...])


def _out_ffn(x, att, rw, weights):
    rows = x.shape[0]
    tile = min(FFN_ROW_TILE, rows)
    tok = lambda width: pl.BlockSpec((tile, width), lambda i: (i, 0))
    resident = lambda w: pl.BlockSpec(w.shape, lambda i: (0,) * w.ndim, pipeline_mode=pl.Buffered(1))
    return pl.pallas_call(
        _out_ffn_kernel,
        grid=(rows // tile,),
        in_specs=[tok(D_MODEL), tok(WIDTH), tok(WIDTH)] + [resident(w) for w in weights],
        out_specs=tok(D_MODEL),
        out_shape=jax.ShapeDtypeStruct((rows, D_MODEL), F32),
        compiler_params=pltpu.CompilerParams(
            dimension_semantics=("arbitrary",), vmem_limit_bytes=VMEM_LIMIT),
        name="out_ffn",
    )(x, att, rw, *weights)


def _bias_selectors():
    part = jnp.arange(3)[:, None, None]
    src = jnp.arange(LANES)[None, :, None]
    dst = jnp.arange(LANES)[None, None, :]
    is_head = src < N_HEADS
    q_sel = (is_head & (dst == src * BIAS_GROUP + part)).astype(F32)
    k_sel = -(is_head & (dst == src * BIAS_GROUP + 3 + part)).astype(F32)
    sel = jnp.concatenate([q_sel, k_sel], axis=2).reshape(3 * LANES, 2 * LANES)
    slot = jnp.arange(LANES) % BIAS_GROUP
    ones = jnp.concatenate([(slot >= 3) & (slot < 6), slot < 3]).astype(F32)[None, :]
    return sel.astype(BF16), ones


def kernel(x_prompt, x_sample, cache_k, cache_v, cache_logf, state_wkv, state_shift, page_table,
           meta_tokens, ln1_g, w_in, b_f, mu_shift, w0, w2, a0, a2, g2, k_k, k_a, r_k,
           gn_g, gn_b, w_out, ln2_g, w_gate, w_up, w_down, lnf_g):
    batch, seq, _ = x_prompt.shape
    n_seq, n_new, _ = x_sample.shape
    fox_cols = 3 * WIDTH + N_HEADS

    w_in0 = w_in[0]
    wqkv = w_in0[:, :3 * WIDTH].astype(BF16)
    wf = jnp.pad(w_in0[:, 3 * WIDTH:fox_cols], ((0, 0), (0, LANES - N_HEADS))).astype(BF16)
    bf = jnp.pad(b_f[0], (0, LANES - N_HEADS))[None, :]
    wrw = w_in0[:, fox_cols:].astype(BF16)
    sel, ones = _bias_selectors()
    w2p = jnp.concatenate([w2[0], jnp.zeros((ICLR_LORA, WIDTH), F32)], axis=0).astype(BF16)
    a2p = jnp.concatenate([jnp.zeros((DECAY_LORA, WIDTH), F32), a2[0]], axis=0).astype(BF16)
    inproj_w = (ln1_g[0][None, :], wqkv, wf, bf, wrw, mu_shift[0][None, :], sel, ones,
                w2p, a2p, g2[0].astype(BF16), w0[0][None, :], a0[0][None, :])
    wkv_p = tuple(t[0].reshape(1, WIDTH) for t in (k_k, k_a, r_k, gn_g, gn_b))
    ffn_w = (w_out[0][:WIDTH].astype(BF16), w_out[0][WIDTH:].astype(BF16), ln2_g[0][None, :],
             w_gate[0].astype(BF16), w_up[0].astype(BF16), w_down[0].astype(BF16), lnf_g[None, :])

    zrow = jnp.zeros((1, 1, RWKV_COLS), F32)
    zc = jnp.zeros((1, 1, LANES), F32)
    mo = _inproj_long(meta_tokens[None], zrow, zc, inproj_w, N_META)
    (_, mk, mkb, mv, mvb, mlf, _, mck, mr, mlw, mkr, mvr, ma, mg, _, mprow, mclast) = mo
    zstate = jnp.zeros((1, N_HEADS, HEAD_DIM, HEAD_DIM), F32)
    _, s_meta = _wkv(mr, mlw, mkr, mvr, ma, mg, zstate, wkv_p, N_META, 1)

    po = _inproj_long(x_prompt, mprow, mclast, inproj_w, ROW_TILE)
    (pq, pk, pkb, pv, pvb, plf, pcq, pck, pr, plw, pkr, pvr, pa, pg, pxl, _, _) = po
    att_p = _fox_prompt(pq, pcq, mkb, mck, mvb, pkb, pck, pvb)
    rw_p, wkv_prompt = _wkv(pr, plw, pkr, pvr, pa, pg, s_meta, wkv_p, WKV_CHUNK, WKV_PROMPT_SEQS)
    y_prompt = _out_ffn(x_prompt.reshape(batch * seq, D_MODEL), att_p.reshape(batch * seq, WIDTH),
                        rw_p.reshape(batch * seq, WIDTH), ffn_w).reshape(batch, seq, D_MODEL)

    rows_s = n_seq * n_new
    xprev = jnp.repeat(state_shift[0], n_new, axis=0)[None]
    so = _inproj_packed(x_sample.reshape(1, rows_s, D_MODEL), xprev, inproj_w, n_new)
    (sq, sk, skb, sv, svb, slf, scq, sck, sr, slw, skr, svr, sa, sg, sxn) = so
    seqs = lambda t: t.reshape(n_seq, n_new, t.shape[-1])
    att_s = _fox_sample(page_table, seqs(sq), seqs(scq), seqs(skb), seqs(sck), seqs(svb),
                        jnp.transpose(cache_k[0], (0, 2, 3, 1)), jnp.transpose(cache_v[0], (0, 2, 3, 1)),
                        jnp.transpose(cache_logf[0], (0, 2, 1)))
    rw_s, wkv_sample = _wkv(seqs(sr), seqs(slw), seqs(skr), seqs(svr), seqs(sa), seqs(sg),
                            state_wkv[0], wkv_p, n_new, WKV_SAMPLE_SEQS)
    y_sample = _out_ffn(x_sample.reshape(rows_s, D_MODEL), att_s.reshape(rows_s, WIDTH),
                        rw_s.reshape(rows_s, WIDTH), ffn_w).reshape(n_seq, n_new, D_MODEL)

    def with_meta(meta, main):
        return jnp.concatenate([jnp.broadcast_to(meta, (batch,) + meta.shape[1:]), main], axis=1)

    heads = lambda t: t.reshape(t.shape[:-1] + (N_HEADS, HEAD_DIM))
    k_prompt = heads(with_meta(mk, pk))[None]
    v_prompt = heads(with_meta(mv, pv))[None]
    logf_prompt = with_meta(mlf, plf)[None]
    shift_prompt = pxl.reshape(1, batch, D_MODEL)
    k_sample = heads(seqs(sk))[None]
    v_sample = heads(seqs(sv))[None]
    logf_sample = seqs(slf)[None]
    shift_sample = seqs(sxn)[:, -1][None]
    return (y_prompt, y_sample, k_prompt, v_prompt, logf_prompt, wkv_prompt[None], shift_prompt,
            k_sample, v_sample, logf_sample, wkv_sample[None], shift_sample)
```

```python
import functools

import jax
import jax.numpy as jnp
from jax import lax
from jax.experimental import pallas as pl
from jax.experimental.pallas import tpu as pltpu

D_MODEL = 1024
N_META = 16
HEAD_DIM = 64
N_HEADS = 8
WIDTH = N_HEADS * HEAD_DIM
PAGE_SIZE = 128
DECAY_LORA = 64
ICLR_LORA = 64
GATE_LORA = 128
RWKV_COLS = 3 * WIDTH + DECAY_LORA + ICLR_LORA + GATE_LORA
RMS_EPS = 1e-6
GN_EPS = 64e-5
NEG_BIG = -1e30

LANES = 128
BIAS_GROUP = 16
HEAD_SHIFT = HEAD_DIM.bit_length() - 1
GROUP_SHIFT = BIAS_GROUP.bit_length() - 1
VMEM_LIMIT = 56 * 1024 * 1024

ROW_TILE = 512
FFN_ROW_TILE = 512
ATT_TILE = 1024
ATT_K_TILE = 256
ATT_SUB = 256
ATT_GROUP = 8
WKV_CHUNK = 64
WKV_PROMPT_SEQS = 2
WKV_SAMPLE_SEQS = 4
PAGES_PER_STEP = 16
FF_CHUNK = 256

BF16 = jnp.bfloat16
F32 = jnp.float32


def _dot(a, b):
    return jnp.dot(a, b, preferred_element_type=F32)


def _dot_nt(a, b):
    return lax.dot_general(a, b, (((1,), (1,)), ((), ())), preferred_element_type=F32)


def _dot_tn(a, b):
    return lax.dot_general(a, b, (((0,), (0,)), ((), ())), preferred_element_type=F32)


def _rms(x, g):
    return x * lax.rsqrt(jnp.mean(x * x, axis=-1, keepdims=True) + RMS_EPS) * g


def _softplus(z):
    return jnp.maximum(z, 0.0) + jnp.log1p(jnp.exp(-jnp.abs(z)))


def _sigmoid(z):
    return 1.0 / (1.0 + jnp.exp(-z))


def _mask_bf16(cond):
    return cond.astype(F32).astype(BF16)


def _split3(c):
    hi = c.astype(BF16)
    r1 = c - hi.astype(F32)
    mid = r1.astype(BF16)
    lo = (r1 - mid.astype(F32)).astype(BF16)
    return hi, mid, lo


def _select_sum(sel, x):
    n = x.shape[1]
    out = _dot(sel.astype(BF16), jnp.concatenate(_split3(x), axis=1))
    return out[:, :n] + out[:, n:2 * n] + out[:, 2 * n:]


def _sum_select(x, sel):
    m = x.shape[0]
    out = _dot(jnp.concatenate(_split3(x), axis=0), sel.astype(BF16))
    return out[:m] + out[m:2 * m] + out[2 * m:]


def _inproj_outputs(xn, qkv, logf, c, p, p_prev, w, outs):
    (mu_ref, sel_ref, ones_ref, w2_ref, a2_ref, g2_ref, w0_ref, a0_ref) = w
    (q_ref, k_ref, kb_ref, v_ref, vb_ref, lf_ref, cq_ref, ck_ref,
     r_ref, lw_ref, kr_ref, vr_ref, a_ref, g_ref) = outs
    q_ref[0] = (qkv[:, :WIDTH] * (HEAD_DIM ** -0.5)).astype(BF16)
    k = qkv[:, WIDTH:2 * WIDTH]
    k_ref[0] = k
    kb_ref[0] = k.astype(BF16)
    v = qkv[:, 2 * WIDTH:]
    v_ref[0] = v
    vb_ref[0] = v.astype(BF16)
    lf_ref[0] = logf[:, :N_HEADS]
    cparts = jnp.concatenate(_split3(c), axis=1)
    cc = _dot(cparts, sel_ref[...]) + ones_ref[...]
    cq_ref[0] = cc[:, :LANES].astype(BF16)
    ck_ref[0] = cc[:, LANES:].astype(BF16)
    rw = p + (p_prev - p) * mu_ref[...]
    r_ref[0] = rw[:, :WIDTH]
    kr_ref[0] = rw[:, WIDTH:2 * WIDTH]
    vr_ref[0] = rw[:, 2 * WIDTH:3 * WIDTH]
    z = rw[:, 3 * WIDTH:3 * WIDTH + LANES]
    gl = rw[:, 3 * WIDTH + LANES:]
    w_log = -_softplus(-(w0_ref[...] + _dot(jnp.tanh(z).astype(BF16), w2_ref[...]))) - 0.5
    lw_ref[0] = -jnp.exp(w_log)
    a_ref[0] = _sigmoid(a0_ref[...] + _dot(z.astype(BF16), a2_ref[...]))
    g_ref[0] = _dot(_sigmoid(gl).astype(BF16), g2_ref[...])


def _inproj_long_kernel(x_ref, prow_ref, c0_ref, ln_ref, wqkv_ref, wf_ref, bf_ref, wrw_ref,
                        mu_ref, sel_ref, ones_ref, w2_ref, a2_ref, g2_ref, w0_ref, a0_ref,
                        q_ref, k_ref, kb_ref, v_ref, vb_ref, lf_ref, cq_ref, ck_ref,
                        r_ref, lw_ref, kr_ref, vr_ref, a_ref, g_ref, xl_ref, pl_ref, cl_ref,
                        pcar_ref, ccar_ref):
    @pl.when(pl.program_id(1) == 0)
    def _():
        pcar_ref[...] = prow_ref[0]
        ccar_ref[...] = c0_ref[0]

    rows = x_ref.shape[1]
    xn = _rms(x_ref[0], ln_ref[...])
    xb = xn.astype(BF16)
    qkv = _dot(xb, wqkv_ref[...])
    logf = -_softplus(-(_dot(xb, wf_ref[...]) + bf_ref[...]))
    ri = lax.broadcasted_iota(jnp.int32, (rows, rows), 0)
    ci = lax.broadcasted_iota(jnp.int32, (rows, rows), 1)
    c = _select_sum((ri >= ci).astype(F32), logf) + ccar_ref[...]
    ccar_ref[...] = c[rows - 1:rows, :]
    p = _dot(xb, wrw_ref[...])
    first = lax.broadcasted_iota(jnp.int32, p.shape, 0) == 0
    p_prev = jnp.where(first, pcar_ref[...], pltpu.roll(p, 1, 0))
    pcar_ref[...] = p[rows - 1:rows, :]
    xl_ref[0] = xn[rows - 1:rows, :]
    pl_ref[0] = p[rows - 1:rows, :]
    cl_ref[0] = c[rows - 1:rows, :]
    _inproj_outputs(xn, qkv, logf, c, p, p_prev,
                    (mu_ref, sel_ref, ones_ref, w2_ref, a2_ref, g2_ref, w0_ref, a0_ref),
                    (q_ref, k_ref, kb_ref, v_ref, vb_ref, lf_ref, cq_ref, ck_ref,
                     r_ref, lw_ref, kr_ref, vr_ref, a_ref, g_ref))


def _inproj_packed_kernel(x_ref, xprev_ref, ln_ref, wqkv_ref, wf_ref, bf_ref, wrw_ref,
                          mu_ref, sel_ref, ones_ref, w2_ref, a2_ref, g2_ref, w0_ref, a0_ref,
                          q_ref, k_ref, kb_ref, v_ref, vb_ref, lf_ref, cq_ref, ck_ref,
                          r_ref, lw_ref, kr_ref, vr_ref, a_ref, g_ref, xn_ref, *, seq_len):
    rows = x_ref.shape[1]
    shift = seq_len.bit_length() - 1
    xn = _rms(x_ref[0], ln_ref[...])
    xn_ref[0] = xn
    xb = xn.astype(BF16)
    qkv = _dot(xb, wqkv_ref[...])
    logf = -_softplus(-(_dot(xb, wf_ref[...]) + bf_ref[...]))
    ri = lax.broadcasted_iota(jnp.int32, (rows, rows), 0)
    ci = lax.broadcasted_iota(jnp.int32, (rows, rows), 1)
    same_seq = (ri >> shift) == (ci >> shift)
    c = _select_sum(((ri >= ci) & same_seq).astype(F32), logf)
    p = _dot(xb, wrw_ref[...])
    p_first = _dot(xprev_ref[0].astype(BF16), wrw_ref[...])
    first = (lax.broadcasted_iota(jnp.int32, p.shape, 0) & (seq_len - 1)) == 0
    p_prev = jnp.where(first, p_first, pltpu.roll(p, 1, 0))
    _inproj_outputs(xn, qkv, logf, c, p, p_prev,
                    (mu_ref, sel_ref, ones_ref, w2_ref, a2_ref, g2_ref, w0_ref, a0_ref),
                    (q_ref, k_ref, kb_ref, v_ref, vb_ref, lf_ref, cq_ref, ck_ref,
                     r_ref, lw_ref, kr_ref, vr_ref, a_ref, g_ref))


def _const_spec(arr):
    return pl.BlockSpec(arr.shape, lambda *_: (0,) * arr.ndim)


def _inproj_out_shapes(batch, rows_total):
    def s(width, dtype):
        return jax.ShapeDtypeStruct((batch, rows_total, width), dtype)
    return [s(WIDTH, BF16), s(WIDTH, F32), s(WIDTH, BF16), s(WIDTH, F32), s(WIDTH, BF16),
            s(N_HEADS, F32), s(LANES, BF16), s(LANES, BF16)] + [s(WIDTH, F32)] * 6


def _inproj_out_specs(tile):
    def s(width):
        return pl.BlockSpec((1, tile, width), lambda b, i: (b, i, 0))
    return [s(WIDTH)] * 5 + [s(N_HEADS), s(LANES), s(LANES)] + [s(WIDTH)] * 6


def _inproj_long(x, prow, c0, weights, tile):
    batch, seq, _ = x.shape
    n_tiles = seq // tile
    row = lambda width: pl.BlockSpec((1, 1, width), lambda b, i: (b, 0, 0))
    shapes = _inproj_out_shapes(batch, seq) + [
        jax.ShapeDtypeStruct((batch, 1, D_MODEL), F32),
        jax.ShapeDtypeStruct((batch, 1, RWKV_COLS), F32),
        jax.ShapeDtypeStruct((batch, 1, LANES), F32)]
    specs = _inproj_out_specs(tile) + [row(D_MODEL), row(RWKV_COLS), row(LANES)]
    return pl.pallas_call(
        _inproj_long_kernel,
        grid=(batch, n_tiles),
        in_specs=[pl.BlockSpec((1, tile, D_MODEL), lambda b, i: (b, i, 0)),
                  _const_spec(prow), _const_spec(c0)] + [_const_spec(w) for w in weights],
        out_specs=specs,
        out_shape=shapes,
        scratch_shapes=[pltpu.VMEM((1, RWKV_COLS), F32), pltpu.VMEM((1, LANES), F32)],
        compiler_params=pltpu.CompilerParams(
            dimension_semantics=("arbitrary", "arbitrary"), vmem_limit_bytes=VMEM_LIMIT),
        name="inproj_long",
    )(x, prow, c0, *weights)


def _inproj_packed(x, xprev, weights, seq_len):
    _, rows, _ = x.shape
    full = pl.BlockSpec((1, rows, D_MODEL), lambda b, i: (0, 0, 0))
    return pl.pallas_call(
        functools.partial(_inproj_packed_kernel, seq_len=seq_len),
        grid=(1, 1),
        in_specs=[full, full] + [_const_spec(w) for w in weights],
        out_specs=_inproj_out_specs(rows) + [full],
        out_shape=_inproj_out_shapes(1, rows) + [jax.ShapeDtypeStruct((1, rows, D_MODEL), F32)],
        compiler_params=pltpu.CompilerParams(
            dimension_semantics=("arbitrary", "arbitrary"), vmem_limit_bytes=VMEM_LIMIT),
        name="inproj_packed",
    )(x, xprev, *weights)


def _fox_prompt_kernel(q_ref, cq_ref, km_ref, ckm_ref, vm_ref, k_ref, ck_ref, v_ref, o_ref):
    pair = pl.program_id(1)
    qi = pl.program_id(2)
    tile = q_ref.shape[1]
    lane = lax.broadcasted_iota(jnp.int32, (1, LANES), 1)
    q2 = q_ref[0]
    cq = cq_ref[0]
    n_sub = tile // ATT_SUB
    lhs = []
    for hh in range(2):
        head_lanes = (lane >= hh * HEAD_DIM) & (lane < (hh + 1) * HEAD_DIM)
        g0 = (2 * pair + hh) * BIAS_GROUP
        group_lanes = (lane >= g0) & (lane < g0 + BIAS_GROUP)
        full = jnp.concatenate([q2 * _mask_bf16(head_lanes), cq * _mask_bf16(group_lanes)], axis=1)
        lhs.append([full[r * ATT_SUB:(r + 1) * ATT_SUB, :] for r in range(n_sub)])
    query_minus_key = (lax.broadcasted_iota(jnp.int32, (ATT_K_TILE, ATT_SUB), 1)
                       - lax.broadcasted_iota(jnp.int32, (ATT_K_TILE, ATT_SUB), 0))

    def block(carry, kk, vv, first_col):
        out = list(carry)
        live = [(hh, r) for hh in range(2) for r in range(n_sub)
                if first_col is None or first_col <= (r + 1) * ATT_SUB - 1]
        for g0 in range(0, len(live), ATT_GROUP):
            group = live[g0:g0 + ATT_GROUP]
            scores, m_news = [], []
            for hh, r in group:
                s = _dot_nt(kk, lhs[hh][r])
                if first_col is not None and first_col + kk.shape[0] - 1 > r * ATT_SUB:
                    s = jnp.where(query_minus_key >= first_col - r * ATT_SUB, s, NEG_BIG)
                scores.append(s)
            for (hh, r), s in zip(group, scores):
                m_news.append(jnp.maximum(out[hh * n_sub + r][0], jnp.max(s, axis=0, keepdims=True)))
            for (hh, r), s, m_new in zip(group, scores, m_news):
                m, l, acc = out[hh * n_sub + r]
                alpha = jnp.exp(m - m_new)
                pe = jnp.exp(s - m_new)
                l_new = alpha * l + jnp.sum(pe, axis=0, keepdims=True)
                acc_new = alpha * acc + _dot_tn(vv, pe.astype(BF16))
                out[hh * n_sub + r] = (m_new, l_new, acc_new)
        return tuple(out)

    init = tuple((jnp.full((1, ATT_SUB), NEG_BIG, F32), jnp.zeros((1, ATT_SUB), F32),
                  jnp.zeros((LANES, ATT_SUB), F32)) for _ in range(2 * n_sub))
    carry = block(init, jnp.concatenate([km_ref[0], ckm_ref[0]], axis=1), vm_ref[0], None)

    def keys(j):
        start = pl.multiple_of(j * ATT_K_TILE, ATT_K_TILE)
        rows = pl.ds(start, ATT_K_TILE)
        return jnp.concatenate([k_ref[0, rows, :], ck_ref[0, rows, :]], axis=1), v_ref[0, rows, :]

    per_q = tile // ATT_K_TILE
    carry = lax.fori_loop(0, qi * per_q, lambda j, c: block(c, *keys(j), None), carry)
    for d in range(per_q):
        carry = block(carry, *keys(qi * per_q + d), d * ATT_K_TILE)
    first_head = lax.broadcasted_iota(jnp.int32, (LANES, ATT_SUB), 0) < HEAD_DIM
    for r in range(n_sub):
        (_, l0, acc0), (_, l1, acc1) = carry[r], carry[n_sub + r]
        o_t = jnp.where(first_head, acc0 / l0, acc1 / l1)
        o_ref[0, r * ATT_SUB:(r + 1) * ATT_SUB, :] = o_t.T.astype(BF16)


def _fox_prompt(q, cq, k_meta, ck_meta, v_meta, k, ck, v):
    batch, seq, _ = q.shape
    n_pairs = N_HEADS // 2
    return pl.pallas_call(
        _fox_prompt_kernel,
        grid=(batch, n_pairs, seq // ATT_TILE),
        in_specs=[
            pl.BlockSpec((1, ATT_TILE, LANES), lambda b, p, i: (b, i, p)),
            pl.BlockSpec((1, ATT_TILE, LANES), lambda b, p, i: (b, i, 0)),
            pl.BlockSpec((1, N_META, LANES), lambda b, p, i: (0, 0, p)),
            pl.BlockSpec((1, N_META, LANES), lambda b, p, i: (0, 0, 0)),
            pl.BlockSpec((1, N_META, LANES), lambda b, p, i: (0, 0, p)),
            pl.BlockSpec((1, seq, LANES), lambda b, p, i: (b, 0, p)),
            pl.BlockSpec((1, seq, LANES), lambda b, p, i: (b, 0, 0)),
            pl.BlockSpec((1, seq, LANES), lambda b, p, i: (b, 0, p)),
        ],
        out_specs=pl.BlockSpec((1, ATT_TILE, LANES), lambda b, p, i: (b, i, p)),
        out_shape=jax.ShapeDtypeStruct((batch, seq, WIDTH), BF16),
        compiler_params=pltpu.CompilerParams(
            dimension_semantics=("arbitrary", "arbitrary", "arbitrary"),
            vmem_limit_bytes=VMEM_LIMIT),
        name="fox_prompt",
    )(q, cq, k_meta, ck_meta, v_meta, k, ck, v)


def _fox_sample_kernel(pt_ref, q_ref, cq_ref, kn_ref, ckn_ref, vn_ref, *rest, n_new):
    del pt_ref
    g_pages = PAGES_PER_STEP
    k_refs, v_refs, lf_refs = rest[:g_pages], rest[g_pages:2 * g_pages], rest[2 * g_pages:3 * g_pages]
    o_ref, q_sc, cqm_sc, roff_sc, carry_sc, m_sc, l_sc, acc_sc = rest[3 * g_pages:]
    step = pl.program_id(1)
    rows = N_HEADS * n_new
    row_head = lax.broadcasted_iota(jnp.int32, (rows, 1), 0) & (N_HEADS - 1)

    def per_head_rows(x):
        return jnp.concatenate([jnp.broadcast_to(x[i:i + 1, :], (N_HEADS, x.shape[1]))
                                for i in range(n_new)], axis=0)

    @pl.when(step == 0)
    def _():
        q = per_head_rows(q_ref[0].astype(F32))
        lane_head = lax.broadcasted_iota(jnp.int32, (1, WIDTH), 1) >> HEAD_SHIFT
        q_sc[...] = jnp.where(row_head == lane_head, q, 0.0).astype(BF16)
        cq = per_head_rows(cq_ref[0].astype(F32))
        lane = lax.broadcasted_iota(jnp.int32, (1, LANES), 1)
        cqm = jnp.where((lane >> GROUP_SHIFT) == row_head, cq, 0.0)
        cqm_sc[...] = cqm.astype(BF16)
        slot = lax.broadcasted_iota(jnp.int32, cqm.shape, 1) & (BIAS_GROUP - 1)
        roff_sc[...] = jnp.sum(jnp.where(slot < 3, cqm, 0.0), axis=-1, keepdims=True)
        carry_sc[...] = jnp.zeros(carry_sc.shape, F32)
        m_sc[...] = jnp.full(m_sc.shape, NEG_BIG, F32)
        l_sc[...] = jnp.zeros(l_sc.shape, F32)
        acc_sc[...] = jnp.zeros(acc_sc.shape, F32)

    q_bd = q_sc[...]
    ti = lax.broadcasted_iota(jnp.int32, (PAGE_SIZE, PAGE_SIZE), 0)
    tj = lax.broadcasted_iota(jnp.int32, (PAGE_SIZE, PAGE_SIZE), 1)
    later_and_all = jnp.concatenate([(ti > tj).astype(F32), jnp.ones((PAGE_SIZE, PAGE_SIZE), F32)],
                                    axis=1)
    lf_all = jnp.concatenate([lf_refs[g][0] for g in range(g_pages)], axis=0)
    sums = _sum_select(lf_all, later_and_all)

    carry = carry_sc[...]
    roff = roff_sc[...]
    def pages_t(refs, g0):
        return jnp.concatenate([refs[g][0].reshape(WIDTH, PAGE_SIZE) for g in (g0, g0 + 1)],
                               axis=1).astype(BF16)

    scores = []
    for g0 in range(0, g_pages, 2):
        bias = []
        for g in (g0, g0 + 1):
            page_sums = sums[g * N_HEADS:(g + 1) * N_HEADS, :]
            bias.append(jnp.concatenate([page_sums[:, :PAGE_SIZE] + carry] * n_new, axis=0))
            carry = carry + page_sums[:, PAGE_SIZE:]
        scores.append(_dot(q_bd, pages_t(k_refs, g0)) + jnp.concatenate(bias, axis=1) + roff)
    carry_sc[...] = carry

    m = m_sc[...]
    m_new = m
    for s in scores:
        m_new = jnp.maximum(m_new, jnp.max(s, axis=-1, keepdims=True))
    alpha = jnp.exp(m - m_new)
    l_new = alpha * l_sc[...]
    acc = alpha * acc_sc[...]
    for i, s in enumerate(scores):
        pe = jnp.exp(s - m_new)
        l_new = l_new + jnp.sum(pe, axis=-1, keepdims=True)
        acc = acc + _dot_nt(pe.astype(BF16), pages_t(v_refs, 2 * i))
    m_sc[...] = m_new
    l_sc[...] = l_new
    acc_sc[...] = acc

    @pl.when(step == pl.num_programs(1) - 1)
    def _():
        s = _dot_nt(q_bd, kn_ref[0]) + _dot_nt(cqm_sc[...], ckn_ref[0])
        row_tok = lax.broadcasted_iota(jnp.int32, (rows, 1), 0) >> (N_HEADS.bit_length() - 1)
        s = jnp.where(lax.broadcasted_iota(jnp.int32, (1, n_new), 1) <= row_tok, s, NEG_BIG)
        m_fin = jnp.maximum(m_new, jnp.max(s, axis=-1, keepdims=True))
        a_fin = jnp.exp(m_new - m_fin)
        pe = jnp.exp(s - m_fin)
        l_fin = a_fin * l_new + jnp.sum(pe, axis=-1, keepdims=True)
        out = (a_fin * acc + _dot(pe.astype(BF16), vn_ref[0])) / l_fin
        own = (lax.broadcasted_iota(jnp.int32, (N_HEADS, WIDTH), 0)
               == lax.broadcasted_iota(jnp.int32, (N_HEADS, WIDTH), 1) >> HEAD_SHIFT)
        o_ref[0] = jnp.concatenate(
            [jnp.sum(jnp.where(own, out[i * N_HEADS:(i + 1) * N_HEADS, :], 0.0), axis=0, keepdims=True)
             for i in range(n_new)], axis=0).astype(BF16)


def _fox_sample(page_table, q, cq, k_new, ck_new, v_new, cache_k, cache_v, cache_logf):
    n_seq, n_new, _ = q.shape
    n_pages = page_table.shape[1]
    g_pages = PAGES_PER_STEP
    rows = N_HEADS * n_new

    def per_seq(width):
        return pl.BlockSpec((1, n_new, width), lambda b, j, pt: (b, 0, 0))

    def page_spec(g, arr):
        block = (1,) + arr.shape[1:]
        return pl.BlockSpec(block, lambda b, j, pt: (pt[b, n_pages - 1 - (j * g_pages + g)],)
                            + (0,) * (arr.ndim - 1))

    grid_spec = pltpu.PrefetchScalarGridSpec(
        num_scalar_prefetch=1,
        grid=(n_seq, n_pages // g_pages),
        in_specs=([per_seq(WIDTH), per_seq(LANES), per_seq(WIDTH), per_seq(LANES), per_seq(WIDTH)]
                  + [page_spec(g, cache_k) for g in range(g_pages)]
                  + [page_spec(g, cache_v) for g in range(g_pages)]
                  + [page_spec(g, cache_logf) for g in range(g_pages)]),
        out_specs=pl.BlockSpec((1, n_new, WIDTH), lambda b, j, pt: (b, 0, 0)),
        scratch_shapes=[pltpu.VMEM((rows, WIDTH), BF16), pltpu.VMEM((rows, LANES), BF16),
                        pltpu.VMEM((rows, 1), F32), pltpu.VMEM((N_HEADS, PAGE_SIZE), F32),
                        pltpu.VMEM((rows, 1), F32), pltpu.VMEM((rows, 1), F32),
                        pltpu.VMEM((rows, WIDTH), F32)],
    )
    return pl.pallas_call(
        functools.partial(_fox_sample_kernel, n_new=n_new),
        grid_spec=grid_spec,
        out_shape=jax.ShapeDtypeStruct((n_seq, n_new, WIDTH), BF16),
        compiler_params=pltpu.CompilerParams(
            dimension_semantics=("arbitrary", "arbitrary"), vmem_limit_bytes=VMEM_LIMIT),
        name="fox_sample",
    )(page_table, q, cq, k_new, ck_new, v_new,
      *([cache_k] * g_pages), *([cache_v] * g_pages), *([cache_logf] * g_pages))


def _wkv_kernel(r_ref, lw_ref, kr_ref, vr_ref, a_ref, g_ref, s0_ref,
                kk_ref, ka_ref, rk_ref, gg_ref, gb_ref, o_ref, so_ref, s_sc):
    nb, C, _ = r_ref.shape
    n_pairs = N_HEADS // 2
    units = [(bb, pair) for bb in range(nb) for pair in range(n_pairs)]
    lo_state = ((lax.broadcasted_iota(jnp.int32, (LANES, LANES), 0) < HEAD_DIM)
                == (lax.broadcasted_iota(jnp.int32, (LANES, LANES), 1) < HEAD_DIM))

    @pl.when(pl.program_id(1) == 0)
    def _():
        zeros = jnp.zeros((HEAD_DIM, HEAD_DIM), F32)
        for bb, pair in units:
            sb = bb if s0_ref.shape[0] == nb else 0
            top = jnp.concatenate([s0_ref[sb, 2 * pair], zeros], axis=1)
            bottom = jnp.concatenate([zeros, s0_ref[sb, 2 * pair + 1]], axis=1)
            s_sc[bb, pair] = jnp.concatenate([top, bottom], axis=0)

    ri = lax.broadcasted_iota(jnp.int32, (C, C), 0)
    ci = lax.broadcasted_iota(jnp.int32, (C, C), 1)
    incl = ri >= ci
    strict = ri > ci
    eye = (ri == ci).astype(F32)
    lo = lax.broadcasted_iota(jnp.int32, (C, LANES), 1) < HEAD_DIM
    lo2 = lax.broadcasted_iota(jnp.int32, (2 * C, LANES), 1) < HEAD_DIM
    levels = max(1, (C - 1).bit_length())
    lcum = [_select_sum(incl.astype(F32), lw_ref[bb]) for bb in range(nb)]

    def per_head(x):
        first = jnp.sum(jnp.where(lo, x, 0.0), axis=-1, keepdims=True)
        second = jnp.sum(jnp.where(lo, 0.0, x), axis=-1, keepdims=True)
        return jnp.where(lo, first, second)

    xs, x_all, bts, kts, bgs, kgs, vbs, kps, e_ends = [], [], [], [], [], [], [], [], []
    for bb, pair in units:
        lanes = slice(pair * LANES, (pair + 1) * LANES)
        r2, lw2, kr2, a2 = (ref[bb, :, lanes] for ref in (r_ref, lw_ref, kr_ref, a_ref))
        lc = lcum[bb][:, lanes]
        kkr = kr2 * kk_ref[:, lanes]
        kk = kkr / jnp.maximum(jnp.sqrt(per_head(kkr * kkr)), 1e-12)
        kp = kr2 * (1.0 + (a2 - 1.0) * ka_ref[:, lanes])
        bv = kk * a2
        e_neg = jnp.exp(-lc)
        l_end = lc[C - 1:C, :]
        e_rem = jnp.exp(l_end - lc)
        x2 = jnp.concatenate([-kk * jnp.exp(lc - lw2), r2 * jnp.exp(lc)], axis=0)
        x_all.append(x2.astype(BF16))
        xs.append((jnp.where(lo2, x2, 0.0).astype(BF16), jnp.where(lo2, 0.0, x2).astype(BF16)))
        bts.append((bv * e_neg).astype(BF16))
        kts.append((kp * e_neg).astype(BF16))
        bgs.append((bv * e_rem).astype(BF16))
        kgs.append((kp * e_rem).astype(BF16))
        vbs.append(vr_ref[bb, :, lanes].astype(BF16))
        kps.append(kp)
        e_ends.append(jnp.exp(l_end))

    chains = [(u, hh) for u in range(len(units)) for hh in range(2)]
    sc_b = [_dot_nt(xs[u][hh], bts[u]) for u, hh in chains]
    sc_k = [_dot_nt(xs[u][hh], kts[u]) for u, hh in chains]
    n_ch = range(len(chains))
    a_ab = [jnp.where(strict, sc_b[i][:C], 0.0) for i in n_ch]
    a_ak = [jnp.where(strict, sc_k[i][:C], 0.0).astype(BF16) for i in n_ch]
    a_rb = [jnp.where(incl, sc_b[i][C:], 0.0).astype(BF16) for i in n_ch]
    a_rk = [jnp.where(incl, sc_k[i][C:], 0.0).astype(BF16) for i in n_ch]

    tinv = [eye + a_ab[i] for i in n_ch]
    if levels > 1:
        lb = [a_ab[i].astype(BF16) for i in n_ch]
        lpow = [_dot(lb[i], lb[i]) for i in n_ch]
        for level in range(1, levels):
            qb = [lpow[i].astype(BF16) for i in n_ch]
            if level == levels - 1:
                tinv = [tinv[i] + _dot(tinv[i].astype(BF16), qb[i]) for i in n_ch]
            else:
                both = [_dot(jnp.concatenate([tinv[i], lpow[i]], axis=0).astype(BF16), qb[i])
                        for i in n_ch]
                tinv = [tinv[i] + both[i][:C] for i in n_ch]
                lpow = [both[i][C:] for i in n_ch]
    tb = [tinv[i].astype(BF16) for i in n_ch]

    def both_heads(u, f):
        return jnp.where(lo, f(2 * u), f(2 * u + 1))

    n_u = range(len(units))
    s_old = [s_sc[bb, pair] for bb, pair in units]
    ps = [_dot_nt(x_all[u], s_old[u].astype(BF16)) for u in n_u]
    w1 = [(ps[u][:C] + both_heads(u, lambda i, u=u: _dot(a_ak[i], vbs[u]))).astype(BF16) for u in n_u]
    ub = [both_heads(u, lambda i, u=u: _dot(tb[i], w1[u])).astype(BF16) for u in n_u]
    ys = [ps[u][C:] + both_heads(u, lambda i, u=u: _dot(a_rb[i], ub[u]) + _dot(a_rk[i], vbs[u]))
          for u in n_u]
    for u, (bb, pair) in enumerate(units):
        cross = _dot_tn(ub[u], bgs[u]) + _dot_tn(vbs[u], kgs[u])
        s_sc[bb, pair] = s_old[u] * e_ends[u] + jnp.where(lo_state, cross, 0.0)

    for u, (bb, pair) in enumerate(units):
        lanes = slice(pair * LANES, (pair + 1) * LANES)
        y = ys[u]
        yc = y - per_head(y) * (1.0 / HEAD_DIM)
        var = per_head(yc * yc) * (1.0 / HEAD_DIM)
        yn = yc * lax.rsqrt(var + GN_EPS) * gg_ref[:, lanes] + gb_ref[:, lanes]
        bonus = per_head(r_ref[bb, :, lanes] * kps[u] * rk_ref[:, lanes]) * vr_ref[bb, :, lanes]
        o_ref[bb, :, lanes] = ((yn + bonus) * g_ref[bb, :, lanes]).astype(BF16)

    @pl.when(pl.program_id(1) == pl.num_programs(1) - 1)
    def _():
        for bb, pair in units:
            s2 = s_sc[bb, pair]
            so_ref[bb, 2 * pair] = s2[:HEAD_DIM, :HEAD_DIM]
            so_ref[bb, 2 * pair + 1] = pltpu.roll(s2[HEAD_DIM:, :], HEAD_DIM, 1)[:, :HEAD_DIM]


def _wkv(r, lw, kr, vr, a, g, s0, params, chunk, batch_tile):
    batch, seq, _ = r.shape
    shared_state = s0.shape[0] == 1
    tok = pl.BlockSpec((batch_tile, chunk, WIDTH), lambda b, c: (b, c, 0))
    state_out = pl.BlockSpec((batch_tile, N_HEADS, HEAD_DIM, HEAD_DIM), lambda b, c: (b, 0, 0, 0))
    state_in = (pl.BlockSpec((1, N_HEADS, HEAD_DIM, HEAD_DIM), lambda b, c: (0, 0, 0, 0))
                if shared_state else state_out)
    return pl.pallas_call(
        _wkv_kernel,
        grid=(batch // batch_tile, seq // chunk),
        in_specs=[tok] * 6 + [state_in] + [_const_spec(p) for p in params],
        out_specs=[tok, state_out],
        out_shape=[jax.ShapeDtypeStruct((batch, seq, WIDTH), BF16),
                   jax.ShapeDtypeStruct((batch, N_HEADS, HEAD_DIM, HEAD_DIM), F32)],
        scratch_shapes=[pltpu.VMEM((batch_tile, N_HEADS // 2, LANES, LANES), F32)],
        compiler_params=pltpu.CompilerParams(
            dimension_semantics=("arbitrary", "arbitrary"), vmem_limit_bytes=VMEM_LIMIT),
        name="wkv",
    )(r, lw, kr, vr, a, g, s0, *params)


def _out_ffn_kernel(x_ref, att_ref, rw_ref, woa_ref, wor_ref, ln2_ref, wg_ref, wu_ref, wd_ref,
                    lnf_ref, y_ref):
    h = x_ref[...] + _dot(att_ref[...], woa_ref[...]) + _dot(rw_ref[...], wor_ref[...])
    hb = _rms(h, ln2_ref[...]).astype(BF16)
    ffn = jnp.zeros(h.shape, F32)
    d_ff = wg_ref.shape[1]
    for c0 in range(0, d_ff, FF_CHUNK):
        gate = _dot(hb, wg_ref[:, c0:c0 + FF_CHUNK])
        up = _dot(hb, wu_ref[:, c0:c0 + FF_CHUNK])
        act = gate * _sigmoid(gate) * up
        ffn = ffn + _dot(act.astype(BF16), wd_ref[c0:c0 + FF_CHUNK, :])
    y_ref[...] = _rms(h + ffn, lnf_ref[...])


def _out_ffn(x, att, rw, weights):
    rows = x.shape[0]
    tile = min(FFN_ROW_TILE, rows)
    tok = lambda width: pl.BlockSpec((tile, width), lambda i: (i, 0))
    resident = lambda w: pl.BlockSpec(w.shape, lambda i: (0,) * w.ndim, pipeline_mode=pl.Buffered(1))
    return pl.pallas_call(
        _out_ffn_kernel,
        grid=(rows // tile,),
        in_specs=[tok(D_MODEL), tok(WIDTH), tok(WIDTH)] + [resident(w) for w in weights],
        out_specs=tok(D_MODEL),
        out_shape=jax.ShapeDtypeStruct((rows, D_MODEL), F32),
        compiler_params=pltpu.CompilerParams(
            dimension_semantics=("arbitrary",), vmem_limit_bytes=VMEM_LIMIT),
        name="out_ffn",
    )(x, att, rw, *weights)


def _bias_selectors():
    part = jnp.arange(3)[:, None, None]
    src = jnp.arange(LANES)[None, :, None]
    dst = jnp.arange(LANES)[None, None, :]
    is_head = src < N_HEADS
    q_sel = (is_head & (dst == src * BIAS_GROUP + part)).astype(F32)
    k_sel = -(is_head & (dst == src * BIAS_GROUP + 3 + part)).astype(F32)
    sel = jnp.concatenate([q_sel, k_sel], axis=2).reshape(3 * LANES, 2 * LANES)
    slot = jnp.arange(LANES) % BIAS_GROUP
    ones = jnp.concatenate([(slot >= 3) & (slot < 6), slot < 3]).astype(F32)[None, :]
    return sel.astype(BF16), ones


def kernel(x_prompt, x_sample, cache_k, cache_v, cache_logf, state_wkv, state_shift, page_table,
           meta_tokens, ln1_g, w_in, b_f, mu_shift, w0, w2, a0, a2, g2, k_k, k_a, r_k,
           gn_g, gn_b, w_out, ln2_g, w_gate, w_up, w_down, lnf_g):
    batch, seq, _ = x_prompt.shape
    n_seq, n_new, _ = x_sample.shape
    fox_cols = 3 * WIDTH + N_HEADS

    w_in0 = w_in[0]
    wqkv = w_in0[:, :3 * WIDTH].astype(BF16)
    wf = jnp.pad(w_in0[:, 3 * WIDTH:fox_cols], ((0, 0), (0, LANES - N_HEADS))).astype(BF16)
    bf = jnp.pad(b_f[0], (0, LANES - N_HEADS))[None, :]
    wrw = w_in0[:, fox_cols:].astype(BF16)
    sel, ones = _bias_selectors()
    w2p = jnp.concatenate([w2[0], jnp.zeros((ICLR_LORA, WIDTH), F32)], axis=0).astype(BF16)
    a2p = jnp.concatenate([jnp.zeros((DECAY_LORA, WIDTH), F32), a2[0]], axis=0).astype(BF16)
    inproj_w = (ln1_g[0][None, :], wqkv, wf, bf, wrw, mu_shift[0][None, :], sel, ones,
                w2p, a2p, g2[0].astype(BF16), w0[0][None, :], a0[0][None, :])
    wkv_p = tuple(t[0].reshape(1, WIDTH) for t in (k_k, k_a, r_k, gn_g, gn_b))
    ffn_w = (w_out[0][:WIDTH].astype(BF16), w_out[0][WIDTH:].astype(BF16), ln2_g[0][None, :],
             w_gate[0].astype(BF16), w_up[0].astype(BF16), w_down[0].astype(BF16), lnf_g[None, :])

    zrow = jnp.zeros((1, 1, RWKV_COLS), F32)
    zc = jnp.zeros((1, 1, LANES), F32)
    mo = _inproj_long(meta_tokens[None], zrow, zc, inproj_w, N_META)
    (_, mk, mkb, mv, mvb, mlf, _, mck, mr, mlw, mkr, mvr, ma, mg, _, mprow, mclast) = mo
    zstate = jnp.zeros((1, N_HEADS, HEAD_DIM, HEAD_DIM), F32)
    _, s_meta = _wkv(mr, mlw, mkr, mvr, ma, mg, zstate, wkv_p, N_META, 1)

    po = _inproj_long(x_prompt, mprow, mclast, inproj_w, ROW_TILE)
    (pq, pk, pkb, pv, pvb, plf, pcq, pck, pr, plw, pkr, pvr, pa, pg, pxl, _, _) = po
    att_p = _fox_prompt(pq, pcq, mkb, mck, mvb, pkb, pck, pvb)
    rw_p, wkv_prompt = _wkv(pr, plw, pkr, pvr, pa, pg, s_meta, wkv_p, WKV_CHUNK, WKV_PROMPT_SEQS)
    y_prompt = _out_ffn(x_prompt.reshape(batch * seq, D_MODEL), att_p.reshape(batch * seq, WIDTH),
                        rw_p.reshape(batch * seq, WIDTH), ffn_w).reshape(batch, seq, D_MODEL)

    rows_s = n_seq * n_new
    xprev = jnp.repeat(state_shift[0], n_new, axis=0)[None]
    so = _inproj_packed(x_sample.reshape(1, rows_s, D_MODEL), xprev, inproj_w, n_new)
    (sq, sk, skb, sv, svb, slf, scq, sck, sr, slw, skr, svr, sa, sg, sxn) = so
    seqs = lambda t: t.reshape(n_seq, n_new, t.shape[-1])
    att_s = _fox_sample(page_table, seqs(sq), seqs(scq), seqs(skb), seqs(sck), seqs(svb),
                        jnp.transpose(cache_k[0], (0, 2, 3, 1)), jnp.transpose(cache_v[0], (0, 2, 3, 1)),
                        jnp.transpose(cache_logf[0], (0, 2, 1)))
    rw_s, wkv_sample = _wkv(seqs(sr), seqs(slw), seqs(skr), seqs(svr), seqs(sa), seqs(sg),
                            state_wkv[0], wkv_p, n_new, WKV_SAMPLE_SEQS)
    y_sample = _out_ffn(x_sample.reshape(rows_s, D_MODEL), att_s.reshape(rows_s, WIDTH),
                        rw_s.reshape(rows_s, WIDTH), ffn_w).reshape(n_seq, n_new, D_MODEL)

    def with_meta(meta, main):
        return jnp.concatenate([jnp.broadcast_to(meta, (batch,) + meta.shape[1:]), main], axis=1)

    heads = lambda t: t.reshape(t.shape[:-1] + (N_HEADS, HEAD_DIM))
    k_prompt = heads(with_meta(mk, pk))[None]
    v_prompt = heads(with_meta(mv, pv))[None]
    logf_prompt = with_meta(mlf, plf)[None]
    shift_prompt = pxl.reshape(1, batch, D_MODEL)
    k_sample = heads(seqs(sk))[None]
    v_sample = heads(seqs(sv))[None]
    logf_sample = seqs(slf)[None]
    shift_sample = seqs(sxn)[:, -1][None]
    return (y_prompt, y_sample, k_prompt, v_prompt, logf_prompt, wkv_prompt[None], shift_prompt,
            k_sample, v_sample, logf_sample, wkv_sample[None], shift_sample)
```

```python
import functools

import jax
import jax.numpy as jnp
from jax import lax
from jax.experimental import pallas as pl
from jax.experimental.pallas import tpu as pltpu

D_MODEL = 1024
N_META = 16
HEAD_DIM = 64
N_HEADS = 8
WIDTH = N_HEADS * HEAD_DIM
PAGE_SIZE = 128
DECAY_LORA = 64
ICLR_LORA = 64
GATE_LORA = 128
RWKV_COLS = 3 * WIDTH + DECAY_LORA + ICLR_LORA + GATE_LORA
RMS_EPS = 1e-6
GN_EPS = 64e-5
NEG_BIG = -1e30

LANES = 128
BIAS_GROUP = 16
HEAD_SHIFT = HEAD_DIM.bit_length() - 1
GROUP_SHIFT = BIAS_GROUP.bit_length() - 1
VMEM_LIMIT = 56 * 1024 * 1024

ROW_TILE = 512
FFN_ROW_TILE = 512
ATT_TILE = 1024
ATT_K_TILE = 512
ATT_SUB = 256
ATT_GROUP = 8
WKV_CHUNK = 64
WKV_PROMPT_SEQS = 8
WKV_SAMPLE_SEQS = 4
PAGES_PER_STEP = 16
FF_CHUNK = 256

BF16 = jnp.bfloat16
F32 = jnp.float32


def _dot(a, b):
    return jnp.dot(a, b, preferred_element_type=F32)


def _dot_nt(a, b):
    return lax.dot_general(a, b, (((1,), (1,)), ((), ())), preferred_element_type=F32)


def _dot_tn(a, b):
    return lax.dot_general(a, b, (((0,), (0,)), ((), ())), preferred_element_type=F32)


def _rms(x, g):
    return x * lax.rsqrt(jnp.mean(x * x, axis=-1, keepdims=True) + RMS_EPS) * g


def _softplus(z):
    return jnp.maximum(z, 0.0) + jnp.log(1.0 + jnp.exp(-jnp.abs(z)))


def _sigmoid(z):
    return 1.0 / (1.0 + jnp.exp(-z))


def _mask_bf16(cond):
    return cond.astype(F32).astype(BF16)


def _split3(c):
    hi = c.astype(BF16)
    r1 = c - hi.astype(F32)
    mid = r1.astype(BF16)
    lo = (r1 - mid.astype(F32)).astype(BF16)
    return hi, mid, lo


def _select_sum(sel, x):
    n = x.shape[1]
    out = _dot(sel.astype(BF16), jnp.concatenate(_split3(x), axis=1))
    return out[:, :n] + out[:, n:2 * n] + out[:, 2 * n:]


def _sum_select(x, sel):
    m = x.shape[0]
    out = _dot(jnp.concatenate(_split3(x), axis=0), sel.astype(BF16))
    return out[:m] + out[m:2 * m] + out[2 * m:]


def _inproj_outputs(xn, qkv, logf, c, p, p_prev, w, outs):
    (mu_ref, sel_ref, ones_ref, w2_ref, a2_ref, g2_ref, w0_ref, a0_ref) = w
    (q_ref, k_ref, kb_ref, v_ref, vb_ref, lf_ref, cq_ref, ck_ref,
     r_ref, lw_ref, kr_ref, vr_ref, a_ref, g_ref) = outs
    q_ref[0] = (qkv[:, :WIDTH] * (HEAD_DIM ** -0.5)).astype(BF16)
    k = qkv[:, WIDTH:2 * WIDTH]
    k_ref[0] = k
    kb_ref[0] = k.astype(BF16)
    v = qkv[:, 2 * WIDTH:]
    v_ref[0] = v
    vb_ref[0] = v.astype(BF16)
    lf_ref[0] = logf[:, :N_HEADS]
    cparts = jnp.concatenate(_split3(c), axis=1)
    cc = _dot(cparts, sel_ref[...]) + ones_ref[...]
    cq_ref[0] = cc[:, :LANES].astype(BF16)
    ck_ref[0] = cc[:, LANES:].astype(BF16)
    rw = p + (p_prev - p) * mu_ref[...]
    r_ref[0] = rw[:, :WIDTH]
    kr_ref[0] = rw[:, WIDTH:2 * WIDTH]
    vr_ref[0] = rw[:, 2 * WIDTH:3 * WIDTH]
    z = rw[:, 3 * WIDTH:3 * WIDTH + LANES]
    gl = rw[:, 3 * WIDTH + LANES:]
    w_log = -_softplus(-(w0_ref[...] + _dot(jnp.tanh(z).astype(BF16), w2_ref[...]))) - 0.5
    lw_ref[0] = -jnp.exp(w_log)
    a_ref[0] = _sigmoid(a0_ref[...] + _dot(z.astype(BF16), a2_ref[...]))
    g_ref[0] = _dot(_sigmoid(gl).astype(BF16), g2_ref[...])


def _inproj_long_kernel(x_ref, prow_ref, c0_ref, ln_ref, wqkv_ref, wf_ref, bf_ref, wrw_ref,
                        mu_ref, sel_ref, ones_ref, w2_ref, a2_ref, g2_ref, w0_ref, a0_ref,
                        q_ref, k_ref, kb_ref, v_ref, vb_ref, lf_ref, cq_ref, ck_ref,
                        r_ref, lw_ref, kr_ref, vr_ref, a_ref, g_ref, xl_ref, pl_ref, cl_ref,
                        pcar_ref, ccar_ref):
    @pl.when(pl.program_id(1) == 0)
    def _():
        pcar_ref[...] = prow_ref[0]
        ccar_ref[...] = c0_ref[0]

    rows = x_ref.shape[1]
    xn = _rms(x_ref[0], ln_ref[...])
    xb = xn.astype(BF16)
    qkv = _dot(xb, wqkv_ref[...])
    logf = -_softplus(-(_dot(xb, wf_ref[...]) + bf_ref[...]))
    ri = lax.broadcasted_iota(jnp.int32, (rows, rows), 0)
    ci = lax.broadcasted_iota(jnp.int32, (rows, rows), 1)
    c = _select_sum((ri >= ci).astype(F32), logf) + ccar_ref[...]
    ccar_ref[...] = c[rows - 1:rows, :]
    p = _dot(xb, wrw_ref[...])
    first = lax.broadcasted_iota(jnp.int32, p.shape, 0) == 0
    p_prev = jnp.where(first, pcar_ref[...], pltpu.roll(p, 1, 0))
    pcar_ref[...] = p[rows - 1:rows, :]
    xl_ref[0] = xn[rows - 1:rows, :]
    pl_ref[0] = p[rows - 1:rows, :]
    cl_ref[0] = c[rows - 1:rows, :]
    _inproj_outputs(xn, qkv, logf, c, p, p_prev,
                    (mu_ref, sel_ref, ones_ref, w2_ref, a2_ref, g2_ref, w0_ref, a0_ref),
                    (q_ref, k_ref, kb_ref, v_ref, vb_ref, lf_ref, cq_ref, ck_ref,
                     r_ref, lw_ref, kr_ref, vr_ref, a_ref, g_ref))


def _inproj_packed_kernel(x_ref, xprev_ref, ln_ref, wqkv_ref, wf_ref, bf_ref, wrw_ref,
                          mu_ref, sel_ref, ones_ref, w2_ref, a2_ref, g2_ref, w0_ref, a0_ref,
                          q_ref, k_ref, kb_ref, v_ref, vb_ref, lf_ref, cq_ref, ck_ref,
                          r_ref, lw_ref, kr_ref, vr_ref, a_ref, g_ref, xn_ref, *, seq_len):
    rows = x_ref.shape[1]
    shift = seq_len.bit_length() - 1
    xn = _rms(x_ref[0], ln_ref[...])
    xn_ref[0] = xn
    xb = xn.astype(BF16)
    qkv = _dot(xb, wqkv_ref[...])
    logf = -_softplus(-(_dot(xb, wf_ref[...]) + bf_ref[...]))
    ri = lax.broadcasted_iota(jnp.int32, (rows, rows), 0)
    ci = lax.broadcasted_iota(jnp.int32, (rows, rows), 1)
    same_seq = (ri >> shift) == (ci >> shift)
    c = _select_sum(((ri >= ci) & same_seq).astype(F32), logf)
    p = _dot(xb, wrw_ref[...])
    p_first = _dot(xprev_ref[0].astype(BF16), wrw_ref[...])
    first = (lax.broadcasted_iota(jnp.int32, p.shape, 0) & (seq_len - 1)) == 0
    p_prev = jnp.where(first, p_first, pltpu.roll(p, 1, 0))
    _inproj_outputs(xn, qkv, logf, c, p, p_prev,
                    (mu_ref, sel_ref, ones_ref, w2_ref, a2_ref, g2_ref, w0_ref, a0_ref),
                    (q_ref, k_ref, kb_ref, v_ref, vb_ref, lf_ref, cq_ref, ck_ref,
                     r_ref, lw_ref, kr_ref, vr_ref, a_ref, g_ref))


def _const_spec(arr):
    return pl.BlockSpec(arr.shape, lambda *_: (0,) * arr.ndim)


def _inproj_out_shapes(batch, rows_total):
    def s(width, dtype):
        return jax.ShapeDtypeStruct((batch, rows_total, width), dtype)
    return [s(WIDTH, BF16), s(WIDTH, F32), s(WIDTH, BF16), s(WIDTH, F32), s(WIDTH, BF16),
            s(N_HEADS, F32), s(LANES, BF16), s(LANES, BF16)] + [s(WIDTH, F32)] * 6


def _inproj_out_specs(tile):
    def s(width):
        return pl.BlockSpec((1, tile, width), lambda b, i: (b, i, 0))
    return [s(WIDTH)] * 5 + [s(N_HEADS), s(LANES), s(LANES)] + [s(WIDTH)] * 6


def _inproj_long(x, prow, c0, weights, tile):
    batch, seq, _ = x.shape
    n_tiles = seq // tile
    row = lambda width: pl.BlockSpec((1, 1, width), lambda b, i: (b, 0, 0))
    shapes = _inproj_out_shapes(batch, seq) + [
        jax.ShapeDtypeStruct((batch, 1, D_MODEL), F32),
        jax.ShapeDtypeStruct((batch, 1, RWKV_COLS), F32),
        jax.ShapeDtypeStruct((batch, 1, LANES), F32)]
    specs = _inproj_out_specs(tile) + [row(D_MODEL), row(RWKV_COLS), row(LANES)]
    return pl.pallas_call(
        _inproj_long_kernel,
        grid=(batch, n_tiles),
        in_specs=[pl.BlockSpec((1, tile, D_MODEL), lambda b, i: (b, i, 0)),
                  _const_spec(prow), _const_spec(c0)] + [_const_spec(w) for w in weights],
        out_specs=specs,
        out_shape=shapes,
        scratch_shapes=[pltpu.VMEM((1, RWKV_COLS), F32), pltpu.VMEM((1, LANES), F32)],
        compiler_params=pltpu.CompilerParams(
            dimension_semantics=("arbitrary", "arbitrary"), vmem_limit_bytes=VMEM_LIMIT),
        name="inproj_long",
    )(x, prow, c0, *weights)


def _inproj_packed(x, xprev, weights, seq_len):
    _, rows, _ = x.shape
    full = pl.BlockSpec((1, rows, D_MODEL), lambda b, i: (0, 0, 0))
    return pl.pallas_call(
        functools.partial(_inproj_packed_kernel, seq_len=seq_len),
        grid=(1, 1),
        in_specs=[full, full] + [_const_spec(w) for w in weights],
        out_specs=_inproj_out_specs(rows) + [full],
        out_shape=_inproj_out_shapes(1, rows) + [jax.ShapeDtypeStruct((1, rows, D_MODEL), F32)],
        compiler_params=pltpu.CompilerParams(
            dimension_semantics=("arbitrary", "arbitrary"), vmem_limit_bytes=VMEM_LIMIT),
        name="inproj_packed",
    )(x, xprev, *weights)


def _fox_prompt_kernel(q_ref, cq_ref, km_ref, ckm_ref, vm_ref, k_ref, ck_ref, v_ref, o_ref):
    pair = pl.program_id(1)
    qi = pl.program_id(2)
    tile = q_ref.shape[1]
    lane = lax.broadcasted_iota(jnp.int32, (1, LANES), 1)
    q2 = q_ref[0]
    cq = cq_ref[0]
    n_sub = tile // ATT_SUB
    lhs = []
    for hh in range(2):
        head_lanes = (lane >= hh * HEAD_DIM) & (lane < (hh + 1) * HEAD_DIM)
        g0 = (2 * pair + hh) * BIAS_GROUP
        group_lanes = (lane >= g0) & (lane < g0 + BIAS_GROUP)
        full = jnp.concatenate([q2 * _mask_bf16(head_lanes), cq * _mask_bf16(group_lanes)], axis=1)
        lhs.append([full[r * ATT_SUB:(r + 1) * ATT_SUB, :] for r in range(n_sub)])
    query_minus_key = (lax.broadcasted_iota(jnp.int32, (ATT_K_TILE, ATT_SUB), 1)
                       - lax.broadcasted_iota(jnp.int32, (ATT_K_TILE, ATT_SUB), 0))

    def block(carry, kk, vv, first_col):
        out = list(carry)
        live = [(hh, r) for hh in range(2) for r in range(n_sub)
                if first_col is None or first_col <= (r + 1) * ATT_SUB - 1]
        for g0 in range(0, len(live), ATT_GROUP):
            group = live[g0:g0 + ATT_GROUP]
            scores, m_news = [], []
            for hh, r in group:
                s = _dot_nt(kk, lhs[hh][r])
                if first_col is not None and first_col + kk.shape[0] - 1 > r * ATT_SUB:
                    s = jnp.where(query_minus_key >= first_col - r * ATT_SUB, s, NEG_BIG)
                scores.append(s)
            for (hh, r), s in zip(group, scores):
                m_news.append(jnp.maximum(out[hh * n_sub + r][0], jnp.max(s, axis=0, keepdims=True)))
            for (hh, r), s, m_new in zip(group, scores, m_news):
                m, l, acc = out[hh * n_sub + r]
                alpha = jnp.exp(m - m_new)
                pe = jnp.exp(s - m_new)
                l_new = alpha * l + jnp.sum(pe, axis=0, keepdims=True)
                acc_new = alpha * acc + _dot_tn(vv, pe.astype(BF16))
                out[hh * n_sub + r] = (m_new, l_new, acc_new)
        return tuple(out)

    init = tuple((jnp.full((1, ATT_SUB), NEG_BIG, F32), jnp.zeros((1, ATT_SUB), F32),
                  jnp.zeros((LANES, ATT_SUB), F32)) for _ in range(2 * n_sub))
    carry = block(init, jnp.concatenate([km_ref[0], ckm_ref[0]], axis=1), vm_ref[0], None)

    def keys(j):
        start = pl.multiple_of(j * ATT_K_TILE, ATT_K_TILE)
        rows = pl.ds(start, ATT_K_TILE)
        return jnp.concatenate([k_ref[0, rows, :], ck_ref[0, rows, :]], axis=1), v_ref[0, rows, :]

    per_q = tile // ATT_K_TILE
    carry = lax.fori_loop(0, qi * per_q, lambda j, c: block(c, *keys(j), None), carry)
    for d in range(per_q):
        carry = block(carry, *keys(qi * per_q + d), d * ATT_K_TILE)
    first_head = lax.broadcasted_iota(jnp.int32, (LANES, ATT_SUB), 0) < HEAD_DIM
    for r in range(n_sub):
        (_, l0, acc0), (_, l1, acc1) = carry[r], carry[n_sub + r]
        o_t = jnp.where(first_head, acc0 / l0, acc1 / l1)
        o_ref[0, r * ATT_SUB:(r + 1) * ATT_SUB, :] = o_t.T.astype(BF16)


def _fox_prompt(q, cq, k_meta, ck_meta, v_meta, k, ck, v):
    batch, seq, _ = q.shape
    n_pairs = N_HEADS // 2
    return pl.pallas_call(
        _fox_prompt_kernel,
        grid=(batch, n_pairs, seq // ATT_TILE),
        in_specs=[
            pl.BlockSpec((1, ATT_TILE, LANES), lambda b, p, i: (b, i, p)),
            pl.BlockSpec((1, ATT_TILE, LANES), lambda b, p, i: (b, i, 0)),
            pl.BlockSpec((1, N_META, LANES), lambda b, p, i: (0, 0, p)),
            pl.BlockSpec((1, N_META, LANES), lambda b, p, i: (0, 0, 0)),
            pl.BlockSpec((1, N_META, LANES), lambda b, p, i: (0, 0, p)),
            pl.BlockSpec((1, seq, LANES), lambda b, p, i: (b, 0, p)),
            pl.BlockSpec((1, seq, LANES), lambda b, p, i: (b, 0, 0)),
            pl.BlockSpec((1, seq, LANES), lambda b, p, i: (b, 0, p)),
        ],
        out_specs=pl.BlockSpec((1, ATT_TILE, LANES), lambda b, p, i: (b, i, p)),
        out_shape=jax.ShapeDtypeStruct((batch, seq, WIDTH), BF16),
        compiler_params=pltpu.CompilerParams(
            dimension_semantics=("arbitrary", "arbitrary", "arbitrary"),
            vmem_limit_bytes=VMEM_LIMIT),
        name="fox_prompt",
    )(q, cq, k_meta, ck_meta, v_meta, k, ck, v)


def _fox_sample_kernel(pt_ref, q_ref, cq_ref, kn_ref, ckn_ref, vn_ref, *rest, n_new):
    del pt_ref
    g_pages = PAGES_PER_STEP
    k_refs, v_refs, lf_refs = rest[:g_pages], rest[g_pages:2 * g_pages], rest[2 * g_pages:3 * g_pages]
    o_ref, q_sc, cqm_sc, roff_sc, carry_sc, m_sc, l_sc, acc_sc = rest[3 * g_pages:]
    step = pl.program_id(1)
    rows = N_HEADS * n_new
    row_head = lax.broadcasted_iota(jnp.int32, (rows, 1), 0) & (N_HEADS - 1)

    def per_head_rows(x):
        return jnp.concatenate([jnp.broadcast_to(x[i:i + 1, :], (N_HEADS, x.shape[1]))
                                for i in range(n_new)], axis=0)

    @pl.when(step == 0)
    def _():
        q = per_head_rows(q_ref[0].astype(F32))
        lane_head = lax.broadcasted_iota(jnp.int32, (1, WIDTH), 1) >> HEAD_SHIFT
        q_sc[...] = jnp.where(row_head == lane_head, q, 0.0).astype(BF16)
        cq = per_head_rows(cq_ref[0].astype(F32))
        lane = lax.broadcasted_iota(jnp.int32, (1, LANES), 1)
        cqm = jnp.where((lane >> GROUP_SHIFT) == row_head, cq, 0.0)
        cqm_sc[...] = cqm.astype(BF16)
        slot = lax.broadcasted_iota(jnp.int32, cqm.shape, 1) & (BIAS_GROUP - 1)
        roff_sc[...] = jnp.sum(jnp.where(slot < 3, cqm, 0.0), axis=-1, keepdims=True)
        carry_sc[...] = jnp.zeros(carry_sc.shape, F32)
        m_sc[...] = jnp.full(m_sc.shape, NEG_BIG, F32)
        l_sc[...] = jnp.zeros(l_sc.shape, F32)
        acc_sc[...] = jnp.zeros(acc_sc.shape, F32)

    q_bd = q_sc[...]
    ti = lax.broadcasted_iota(jnp.int32, (PAGE_SIZE, PAGE_SIZE), 0)
    tj = lax.broadcasted_iota(jnp.int32, (PAGE_SIZE, PAGE_SIZE), 1)
    later_and_all = jnp.concatenate([(ti > tj).astype(F32), jnp.ones((PAGE_SIZE, PAGE_SIZE), F32)],
                                    axis=1)
    lf_all = jnp.concatenate([lf_refs[g][0] for g in range(g_pages)], axis=0)
    sums = _sum_select(lf_all, later_and_all)

    carry = carry_sc[...]
    roff = roff_sc[...]
    def pages_t(refs, g0):
        return jnp.concatenate([refs[g][0].reshape(WIDTH, PAGE_SIZE) for g in (g0, g0 + 1)],
                               axis=1).astype(BF16)

    scores = []
    for g0 in range(0, g_pages, 2):
        bias = []
        for g in (g0, g0 + 1):
            page_sums = sums[g * N_HEADS:(g + 1) * N_HEADS, :]
            bias.append(jnp.concatenate([page_sums[:, :PAGE_SIZE] + carry] * n_new, axis=0))
            carry = carry + page_sums[:, PAGE_SIZE:]
        scores.append(_dot(q_bd, pages_t(k_refs, g0)) + jnp.concatenate(bias, axis=1) + roff)
    carry_sc[...] = carry

    m = m_sc[...]
    m_new = m
    for s in scores:
        m_new = jnp.maximum(m_new, jnp.max(s, axis=-1, keepdims=True))
    alpha = jnp.exp(m - m_new)
    l_new = alpha * l_sc[...]
    acc = alpha * acc_sc[...]
    for i, s in enumerate(scores):
        pe = jnp.exp(s - m_new)
        l_new = l_new + jnp.sum(pe, axis=-1, keepdims=True)
        acc = acc + _dot_nt(pe.astype(BF16), pages_t(v_refs, 2 * i))
    m_sc[...] = m_new
    l_sc[...] = l_new
    acc_sc[...] = acc

    @pl.when(step == pl.num_programs(1) - 1)
    def _():
        s = _dot_nt(q_bd, kn_ref[0]) + _dot_nt(cqm_sc[...], ckn_ref[0])
        row_tok = lax.broadcasted_iota(jnp.int32, (rows, 1), 0) >> (N_HEADS.bit_length() - 1)
        s = jnp.where(lax.broadcasted_iota(jnp.int32, (1, n_new), 1) <= row_tok, s, NEG_BIG)
        m_fin = jnp.maximum(m_new, jnp.max(s, axis=-1, keepdims=True))
        a_fin = jnp.exp(m_new - m_fin)
        pe = jnp.exp(s - m_fin)
        l_fin = a_fin * l_new + jnp.sum(pe, axis=-1, keepdims=True)
        out = (a_fin * acc + _dot(pe.astype(BF16), vn_ref[0])) / l_fin
        own = (lax.broadcasted_iota(jnp.int32, (N_HEADS, WIDTH), 0)
               == lax.broadcasted_iota(jnp.int32, (N_HEADS, WIDTH), 1) >> HEAD_SHIFT)
        o_ref[0] = jnp.concatenate(
            [jnp.sum(jnp.where(own, out[i * N_HEADS:(i + 1) * N_HEADS, :], 0.0), axis=0, keepdims=True)
             for i in range(n_new)], axis=0).astype(BF16)


def _fox_sample(page_table, q, cq, k_new, ck_new, v_new, cache_k, cache_v, cache_logf):
    n_seq, n_new, _ = q.shape
    n_pages = page_table.shape[1]
    g_pages = PAGES_PER_STEP
    rows = N_HEADS * n_new

    def per_seq(width):
        return pl.BlockSpec((1, n_new, width), lambda b, j, pt: (b, 0, 0))

    def page_spec(g, arr):
        block = (1,) + arr.shape[1:]
        return pl.BlockSpec(block, lambda b, j, pt: (pt[b, n_pages - 1 - (j * g_pages + g)],)
                            + (0,) * (arr.ndim - 1))

    grid_spec = pltpu.PrefetchScalarGridSpec(
        num_scalar_prefetch=1,
        grid=(n_seq, n_pages // g_pages),
        in_specs=([per_seq(WIDTH), per_seq(LANES), per_seq(WIDTH), per_seq(LANES), per_seq(WIDTH)]
                  + [page_spec(g, cache_k) for g in range(g_pages)]
                  + [page_spec(g, cache_v) for g in range(g_pages)]
                  + [page_spec(g, cache_logf) for g in range(g_pages)]),
        out_specs=pl.BlockSpec((1, n_new, WIDTH), lambda b, j, pt: (b, 0, 0)),
        scratch_shapes=[pltpu.VMEM((rows, WIDTH), BF16), pltpu.VMEM((rows, LANES), BF16),
                        pltpu.VMEM((rows, 1), F32), pltpu.VMEM((N_HEADS, PAGE_SIZE), F32),
                        pltpu.VMEM((rows, 1), F32), pltpu.VMEM((rows, 1), F32),
                        pltpu.VMEM((rows, WIDTH), F32)],
    )
    return pl.pallas_call(
        functools.partial(_fox_sample_kernel, n_new=n_new),
        grid_spec=grid_spec,
        out_shape=jax.ShapeDtypeStruct((n_seq, n_new, WIDTH), BF16),
        compiler_params=pltpu.CompilerParams(
            dimension_semantics=("arbitrary", "arbitrary"), vmem_limit_bytes=VMEM_LIMIT),
        name="fox_sample",
    )(page_table, q, cq, k_new, ck_new, v_new,
      *([cache_k] * g_pages), *([cache_v] * g_pages), *([cache_logf] * g_pages))


def _wkv_kernel(r_ref, lw_ref, kr_ref, vr_ref, a_ref, g_ref, s0_ref,
                kk_ref, ka_ref, rk_ref, gg_ref, gb_ref, o_ref, so_ref, s_sc):
    nb, C, _ = r_ref.shape
    n_pairs = N_HEADS // 2
    units = [(bb, pair) for bb in range(nb) for pair in range(n_pairs)]
    lo_state = ((lax.broadcasted_iota(jnp.int32, (LANES, LANES), 0) < HEAD_DIM)
                == (lax.broadcasted_iota(jnp.int32, (LANES, LANES), 1) < HEAD_DIM))

    @pl.when(pl.program_id(1) == 0)
    def _():
        zeros = jnp.zeros((HEAD_DIM, HEAD_DIM), F32)
        for bb, pair in units:
            sb = bb if s0_ref.shape[0] == nb else 0
            top = jnp.concatenate([s0_ref[sb, 2 * pair], zeros], axis=1)
            bottom = jnp.concatenate([zeros, s0_ref[sb, 2 * pair + 1]], axis=1)
            s_sc[bb, pair] = jnp.concatenate([top, bottom], axis=0)

    ri = lax.broadcasted_iota(jnp.int32, (C, C), 0)
    ci = lax.broadcasted_iota(jnp.int32, (C, C), 1)
    incl = ri >= ci
    strict = ri > ci
    eye = (ri == ci).astype(F32)
    lo = lax.broadcasted_iota(jnp.int32, (C, LANES), 1) < HEAD_DIM
    lo2 = lax.broadcasted_iota(jnp.int32, (2 * C, LANES), 1) < HEAD_DIM
    levels = max(1, (C - 1).bit_length())
    lcum = [_select_sum(incl.astype(F32), lw_ref[bb]) for bb in range(nb)]

    def per_head(x):
        first = jnp.sum(jnp.where(lo, x, 0.0), axis=-1, keepdims=True)
        second = jnp.sum(jnp.where(lo, 0.0, x), axis=-1, keepdims=True)
        return jnp.where(lo, first, second)

    xs, x_all, bts, kts, bgs, kgs, vbs, kps, e_ends = [], [], [], [], [], [], [], [], []
    for bb, pair in units:
        lanes = slice(pair * LANES, (pair + 1) * LANES)
        r2, lw2, kr2, a2 = (ref[bb, :, lanes] for ref in (r_ref, lw_ref, kr_ref, a_ref))
        lc = lcum[bb][:, lanes]
        kkr = kr2 * kk_ref[:, lanes]
        kk = kkr / jnp.maximum(jnp.sqrt(per_head(kkr * kkr)), 1e-12)
        kp = kr2 * (1.0 + (a2 - 1.0) * ka_ref[:, lanes])
        bv = kk * a2
        e_neg = jnp.exp(-lc)
        l_end = lc[C - 1:C, :]
        e_rem = jnp.exp(l_end - lc)
        x2 = jnp.concatenate([-kk * jnp.exp(lc - lw2), r2 * jnp.exp(lc)], axis=0)
        x_all.append(x2.astype(BF16))
        xs.append((jnp.where(lo2, x2, 0.0).astype(BF16), jnp.where(lo2, 0.0, x2).astype(BF16)))
        bts.append((bv * e_neg).astype(BF16))
        kts.append((kp * e_neg).astype(BF16))
        bgs.append((bv * e_rem).astype(BF16))
        kgs.append((kp * e_rem).astype(BF16))
        vbs.append(vr_ref[bb, :, lanes].astype(BF16))
        kps.append(kp)
        e_ends.append(jnp.exp(l_end))

    chains = [(u, hh) for u in range(len(units)) for hh in range(2)]
    sc_b = [_dot_nt(xs[u][hh], bts[u]) for u, hh in chains]
    sc_k = [_dot_nt(xs[u][hh], kts[u]) for u, hh in chains]
    n_ch = range(len(chains))
    a_ab = [jnp.where(strict, sc_b[i][:C], 0.0) for i in n_ch]
    a_ak = [jnp.where(strict, sc_k[i][:C], 0.0).astype(BF16) for i in n_ch]
    a_rb = [jnp.where(incl, sc_b[i][C:], 0.0).astype(BF16) for i in n_ch]
    a_rk = [jnp.where(incl, sc_k[i][C:], 0.0).astype(BF16) for i in n_ch]

    tinv = [eye + a_ab[i] for i in n_ch]
    if levels > 1:
        lb = [a_ab[i].astype(BF16) for i in n_ch]
        lpow = [_dot(lb[i], lb[i]) for i in n_ch]
        for level in range(1, levels):
            qb = [lpow[i].astype(BF16) for i in n_ch]
            if level == levels - 1:
                tinv = [tinv[i] + _dot(tinv[i].astype(BF16), qb[i]) for i in n_ch]
            else:
                both = [_dot(jnp.concatenate([tinv[i], lpow[i]], axis=0).astype(BF16), qb[i])
                        for i in n_ch]
                tinv = [tinv[i] + both[i][:C] for i in n_ch]
                lpow = [both[i][C:] for i in n_ch]
    tb = [tinv[i].astype(BF16) for i in n_ch]

    def both_heads(u, f):
        return jnp.where(lo, f(2 * u), f(2 * u + 1))

    n_u = range(len(units))
    s_old = [s_sc[bb, pair] for bb, pair in units]
    ps = [_dot_nt(x_all[u], s_old[u].astype(BF16)) for u in n_u]
    w1 = [(ps[u][:C] + both_heads(u, lambda i, u=u: _dot(a_ak[i], vbs[u]))).astype(BF16) for u in n_u]
    ub = [both_heads(u, lambda i, u=u: _dot(tb[i], w1[u])).astype(BF16) for u in n_u]
    ys = [ps[u][C:] + both_heads(u, lambda i, u=u: _dot(a_rb[i], ub[u]) + _dot(a_rk[i], vbs[u]))
          for u in n_u]
    for u, (bb, pair) in enumerate(units):
        cross = _dot_tn(ub[u], bgs[u]) + _dot_tn(vbs[u], kgs[u])
        s_sc[bb, pair] = s_old[u] * e_ends[u] + jnp.where(lo_state, cross, 0.0)

    for u, (bb, pair) in enumerate(units):
        lanes = slice(pair * LANES, (pair + 1) * LANES)
        y = ys[u]
        yc = y - per_head(y) * (1.0 / HEAD_DIM)
        var = per_head(yc * yc) * (1.0 / HEAD_DIM)
        yn = yc * lax.rsqrt(var + GN_EPS) * gg_ref[:, lanes] + gb_ref[:, lanes]
        bonus = per_head(r_ref[bb, :, lanes] * kps[u] * rk_ref[:, lanes]) * vr_ref[bb, :, lanes]
        o_ref[bb, :, lanes] = ((yn + bonus) * g_ref[bb, :, lanes]).astype(BF16)

    @pl.when(pl.program_id(1) == pl.num_programs(1) - 1)
    def _():
        for bb, pair in units:
            s2 = s_sc[bb, pair]
            so_ref[bb, 2 * pair] = s2[:HEAD_DIM, :HEAD_DIM]
            so_ref[bb, 2 * pair + 1] = pltpu.roll(s2[HEAD_DIM:, :], HEAD_DIM, 1)[:, :HEAD_DIM]


def _wkv(r, lw, kr, vr, a, g, s0, params, chunk, batch_tile):
    batch, seq, _ = r.shape
    shared_state = s0.shape[0] == 1
    tok = pl.BlockSpec((batch_tile, chunk, WIDTH), lambda b, c: (b, c, 0))
    state_out = pl.BlockSpec((batch_tile, N_HEADS, HEAD_DIM, HEAD_DIM), lambda b, c: (b, 0, 0, 0))
    state_in = (pl.BlockSpec((1, N_HEADS, HEAD_DIM, HEAD_DIM), lambda b, c: (0, 0, 0, 0))
                if shared_state else state_out)
    return pl.pallas_call(
        _wkv_kernel,
        grid=(batch // batch_tile, seq // chunk),
        in_specs=[tok] * 6 + [state_in] + [_const_spec(p) for p in params],
        out_specs=[tok, state_out],
        out_shape=[jax.ShapeDtypeStruct((batch, seq, WIDTH), BF16),
                   jax.ShapeDtypeStruct((batch, N_HEADS, HEAD_DIM, HEAD_DIM), F32)],
        scratch_shapes=[pltpu.VMEM((batch_tile, N_HEADS // 2, LANES, LANES), F32)],
        compiler_params=pltpu.CompilerParams(
            dimension_semantics=("arbitrary", "arbitrary"), vmem_limit_bytes=VMEM_LIMIT),
        name="wkv",
    )(r, lw, kr, vr, a, g, s0, *params)


def _out_ffn_kernel(x_ref, att_ref, rw_ref, woa_ref, wor_ref, ln2_ref, wg_ref, wu_ref, wd_ref,
                    lnf_ref, y_ref):
    h = x_ref[...] + _dot(att_ref[...], woa_ref[...]) + _dot(rw_ref[...], wor_ref[...])
    hb = _rms(h, ln2_ref[...]).astype(BF16)
    ffn = jnp.zeros(h.shape, F32)
    d_ff = wg_ref.shape[1]
    for c0 in range(0, d_ff, FF_CHUNK):
        gate = _dot(hb, wg_ref[:, c0:c0 + FF_CHUNK])
        up = _dot(hb, wu_ref[:, c0:c0 + FF_CHUNK])
        act = gate * _sigmoid(gate) * up
        ffn = ffn + _dot(act.astype(BF16), wd_ref[c0:c0 + FF_CHUNK, :])
    y_ref[...] = _rms(h + ffn, lnf_ref[...])


def _out_ffn(x, att, rw, weights):
    rows = x.shape[0]
    tile = min(FFN_ROW_TILE, rows)
    tok = lambda width: pl.BlockSpec((tile, width), lambda i: (i, 0))
    resident = lambda w: pl.BlockSpec(w.shape, lambda i: (0,) * w.ndim, pipeline_mode=pl.Buffered(1))
    return pl.pallas_call(
        _out_ffn_kernel,
        grid=(rows // tile,),
        in_specs=[tok(D_MODEL), tok(WIDTH), tok(WIDTH)] + [resident(w) for w in weights],
        out_specs=tok(D_MODEL),
        out_shape=jax.ShapeDtypeStruct((rows, D_MODEL), F32),
        compiler_params=pltpu.CompilerParams(
            dimension_semantics=("arbitrary",), vmem_limit_bytes=VMEM_LIMIT),
        name="out_ffn",
    )(x, att, rw, *weights)


def _bias_selectors():
    part = jnp.arange(3)[:, None, None]
    src = jnp.arange(LANES)[None, :, None]
    dst = jnp.arange(LANES)[None, None, :]
    is_head = src < N_HEADS
    q_sel = (is_head & (dst == src * BIAS_GROUP + part)).astype(F32)
    k_sel = -(is_head & (dst == src * BIAS_GROUP + 3 + part)).astype(F32)
    sel = jnp.concatenate([q_sel, k_sel], axis=2).reshape(3 * LANES, 2 * LANES)
    slot = jnp.arange(LANES) % BIAS_GROUP
    ones = jnp.concatenate([(slot >= 3) & (slot < 6), slot < 3]).astype(F32)[None, :]
    return sel.astype(BF16), ones


def kernel(x_prompt, x_sample, cache_k, cache_v, cache_logf, state_wkv, state_shift, page_table,
           meta_tokens, ln1_g, w_in, b_f, mu_shift, w0, w2, a0, a2, g2, k_k, k_a, r_k,
           gn_g, gn_b, w_out, ln2_g, w_gate, w_up, w_down, lnf_g):
    batch, seq, _ = x_prompt.shape
    n_seq, n_new, _ = x_sample.shape
    fox_cols = 3 * WIDTH + N_HEADS

    w_in0 = w_in[0]
    wqkv = w_in0[:, :3 * WIDTH].astype(BF16)
    wf = jnp.pad(w_in0[:, 3 * WIDTH:fox_cols], ((0, 0), (0, LANES - N_HEADS))).astype(BF16)
    bf = jnp.pad(b_f[0], (0, LANES - N_HEADS))[None, :]
    wrw = w_in0[:, fox_cols:].astype(BF16)
    sel, ones = _bias_selectors()
    w2p = jnp.concatenate([w2[0], jnp.zeros((ICLR_LORA, WIDTH), F32)], axis=0).astype(BF16)
    a2p = jnp.concatenate([jnp.zeros((DECAY_LORA, WIDTH), F32), a2[0]], axis=0).astype(BF16)
    inproj_w = (ln1_g[0][None, :], wqkv, wf, bf, wrw, mu_shift[0][None, :], sel, ones,
                w2p, a2p, g2[0].astype(BF16), w0[0][None, :], a0[0][None, :])
    wkv_p = tuple(t[0].reshape(1, WIDTH) for t in (k_k, k_a, r_k, gn_g, gn_b))
    ffn_w = (w_out[0][:WIDTH].astype(BF16), w_out[0][WIDTH:].astype(BF16), ln2_g[0][None, :],
             w_gate[0].astype(BF16), w_up[0].astype(BF16), w_down[0].astype(BF16), lnf_g[None, :])

    zrow = jnp.zeros((1, 1, RWKV_COLS), F32)
    zc = jnp.zeros((1, 1, LANES), F32)
    mo = _inproj_long(meta_tokens[None], zrow, zc, inproj_w, N_META)
    (_, mk, mkb, mv, mvb, mlf, _, mck, mr, mlw, mkr, mvr, ma, mg, _, mprow, mclast) = mo
    zstate = jnp.zeros((1, N_HEADS, HEAD_DIM, HEAD_DIM), F32)
    _, s_meta = _wkv(mr, mlw, mkr, mvr, ma, mg, zstate, wkv_p, N_META, 1)

    po = _inproj_long(x_prompt, mprow, mclast, inproj_w, ROW_TILE)
    (pq, pk, pkb, pv, pvb, plf, pcq, pck, pr, plw, pkr, pvr, pa, pg, pxl, _, _) = po
    att_p = _fox_prompt(pq, pcq, mkb, mck, mvb, pkb, pck, pvb)
    rw_p, wkv_prompt = _wkv(pr, plw, pkr, pvr, pa, pg, s_meta, wkv_p, WKV_CHUNK, WKV_PROMPT_SEQS)
    y_prompt = _out_ffn(x_prompt.reshape(batch * seq, D_MODEL), att_p.reshape(batch * seq, WIDTH),
                        rw_p.reshape(batch * seq, WIDTH), ffn_w).reshape(batch, seq, D_MODEL)

    rows_s = n_seq * n_new
    xprev = jnp.repeat(state_shift[0], n_new, axis=0)[None]
    so = _inproj_packed(x_sample.reshape(1, rows_s, D_MODEL), xprev, inproj_w, n_new)
    (sq, sk, skb, sv, svb, slf, scq, sck, sr, slw, skr, svr, sa, sg, sxn) = so
    seqs = lambda t: t.reshape(n_seq, n_new, t.shape[-1])
    att_s = _fox_sample(page_table, seqs(sq), seqs(scq), seqs(skb), seqs(sck), seqs(svb),
                        jnp.transpose(cache_k[0], (0, 2, 3, 1)), jnp.transpose(cache_v[0], (0, 2, 3, 1)),
                        jnp.transpose(cache_logf[0], (0, 2, 1)))
    rw_s, wkv_sample = _wkv(seqs(sr), seqs(slw), seqs(skr), seqs(svr), seqs(sa), seqs(sg),
                            state_wkv[0], wkv_p, n_new, WKV_SAMPLE_SEQS)
    y_sample = _out_ffn(x_sample.reshape(rows_s, D_MODEL), att_s.reshape(rows_s, WIDTH),
                        rw_s.reshape(rows_s, WIDTH), ffn_w).reshape(n_seq, n_new, D_MODEL)

    def with_meta(meta, main):
        return jnp.concatenate([jnp.broadcast_to(meta, (batch,) + meta.shape[1:]), main], axis=1)

    heads = lambda t: t.reshape(t.shape[:-1] + (N_HEADS, HEAD_DIM))
    k_prompt = heads(with_meta(mk, pk))[None]
    v_prompt = heads(with_meta(mv, pv))[None]
    logf_prompt = with_meta(mlf, plf)[None]
    shift_prompt = pxl.reshape(1, batch, D_MODEL)
    k_sample = heads(seqs(sk))[None]
    v_sample = heads(seqs(sv))[None]
    logf_sample = seqs(slf)[None]
    shift_sample = seqs(sxn)[:, -1][None]
    return (y_prompt, y_sample, k_prompt, v_prompt, logf_prompt, wkv_prompt[None], shift_prompt,
            k_sample, v_sample, logf_sample, wkv_sample[None], shift_sample)
```

```python
import functools

import jax
import jax.numpy as jnp
from jax import lax
from jax.experimental import pallas as pl
from jax.experimental.pallas import tpu as pltpu

D_MODEL = 1024
N_META = 16
HEAD_DIM = 64
N_HEADS = 8
WIDTH = N_HEADS * HEAD_DIM
PAGE_SIZE = 128
DECAY_LORA = 64
ICLR_LORA = 64
GATE_LORA = 128
RWKV_COLS = 3 * WIDTH + DECAY_LORA + ICLR_LORA + GATE_LORA
RMS_EPS = 1e-6
GN_EPS = 64e-5
NEG_BIG = -1e30

LANES = 128
BIAS_GROUP = 16
BIAS_PARTS = 3
HEAD_SHIFT = HEAD_DIM.bit_length() - 1
GROUP_SHIFT = BIAS_GROUP.bit_length() - 1
VMEM_LIMIT = 56 * 1024 * 1024

ROW_TILE = 512
FFN_ROW_TILE = 512
ATT_TILE = 1024
ATT_K_TILE = 512
ATT_SUB = 256
ATT_GROUP = 8
WKV_CHUNK = 64
WKV_PROMPT_SEQS = 8
WKV_SAMPLE_SEQS = 4
PAGES_PER_STEP = 32
FF_CHUNK = 256

BF16 = jnp.bfloat16
F32 = jnp.float32


def _dot(a, b):
    return jnp.dot(a, b, preferred_element_type=F32)


def _dot_nt(a, b):
    return lax.dot_general(a, b, (((1,), (1,)), ((), ())), preferred_element_type=F32)


def _dot_tn(a, b):
    return lax.dot_general(a, b, (((0,), (0,)), ((), ())), preferred_element_type=F32)


def _rms(x, g):
    return x * lax.rsqrt(jnp.mean(x * x, axis=-1, keepdims=True) + RMS_EPS) * g


def _softplus(z):
    return jnp.maximum(z, 0.0) + jnp.log(1.0 + jnp.exp(-jnp.abs(z)))


def _sigmoid(z):
    return 1.0 / (1.0 + jnp.exp(-z))


def _mask_bf16(cond):
    return cond.astype(F32).astype(BF16)


def _split3(c):
    hi = c.astype(BF16)
    r1 = c - hi.astype(F32)
    mid = r1.astype(BF16)
    lo = (r1 - mid.astype(F32)).astype(BF16)
    return hi, mid, lo


def _select_sum(sel, x):
    n = x.shape[1]
    out = _dot(sel.astype(BF16), jnp.concatenate(_split3(x), axis=1))
    return out[:, :n] + out[:, n:2 * n] + out[:, 2 * n:]


def _sum_select(x, sel):
    m = x.shape[0]
    out = _dot(jnp.concatenate(_split3(x), axis=0), sel.astype(BF16))
    return out[:m] + out[m:2 * m] + out[2 * m:]


def _inproj_outputs(xn, qkv, logf, c, p, p_prev, w, outs):
    (mu_ref, sel_ref, ones_ref, w2_ref, a2_ref, g2_ref, w0_ref, a0_ref) = w
    (q_ref, k_ref, kb_ref, v_ref, vb_ref, lf_ref, cq_ref, ck_ref,
     r_ref, lw_ref, kr_ref, vr_ref, a_ref, g_ref) = outs
    q_ref[0] = (qkv[:, :WIDTH] * (HEAD_DIM ** -0.5)).astype(BF16)
    k = qkv[:, WIDTH:2 * WIDTH]
    k_ref[0] = k
    kb_ref[0] = k.astype(BF16)
    v = qkv[:, 2 * WIDTH:]
    v_ref[0] = v
    vb_ref[0] = v.astype(BF16)
    lf_ref[0] = logf.T[:N_HEADS, :]
    cparts = jnp.concatenate(_split3(c), axis=1)
    cc = _dot(cparts, sel_ref[...]) + ones_ref[...]
    cq_ref[0] = cc[:, :LANES].astype(BF16)
    ck_ref[0] = cc[:, LANES:].astype(BF16)
    rw = p + (p_prev - p) * mu_ref[...]
    r_ref[0] = rw[:, :WIDTH]
    kr_ref[0] = rw[:, WIDTH:2 * WIDTH]
    vr_ref[0] = rw[:, 2 * WIDTH:3 * WIDTH]
    z = rw[:, 3 * WIDTH:3 * WIDTH + LANES]
    gl = rw[:, 3 * WIDTH + LANES:]
    w_log = -_softplus(-(w0_ref[...] + _dot(jnp.tanh(z).astype(BF16), w2_ref[...]))) - 0.5
    lw_ref[0] = -jnp.exp(w_log)
    a_ref[0] = _sigmoid(a0_ref[...] + _dot(z.astype(BF16), a2_ref[...]))
    g_ref[0] = _dot(_sigmoid(gl).astype(BF16), g2_ref[...])


def _inproj_long_kernel(x_ref, prow_ref, c0_ref, ln_ref, wqkv_ref, wf_ref, bf_ref, wrw_ref,
                        mu_ref, sel_ref, ones_ref, w2_ref, a2_ref, g2_ref, w0_ref, a0_ref,
                        q_ref, k_ref, kb_ref, v_ref, vb_ref, lf_ref, cq_ref, ck_ref,
                        r_ref, lw_ref, kr_ref, vr_ref, a_ref, g_ref, xl_ref, pl_ref, cl_ref,
                        pcar_ref, ccar_ref):
    @pl.when(pl.program_id(1) == 0)
    def _():
        pcar_ref[...] = prow_ref[0]
        ccar_ref[...] = c0_ref[0]

    rows = x_ref.shape[1]
    xn = _rms(x_ref[0], ln_ref[...])
    xb = xn.astype(BF16)
    qkv = _dot(xb, wqkv_ref[...])
    logf = -_softplus(-(_dot(xb, wf_ref[...]) + bf_ref[...]))
    ri = lax.broadcasted_iota(jnp.int32, (rows, rows), 0)
    ci = lax.broadcasted_iota(jnp.int32, (rows, rows), 1)
    c = _select_sum((ri >= ci).astype(F32), logf) + ccar_ref[...]
    ccar_ref[...] = c[rows - 1:rows, :]
    p = _dot(xb, wrw_ref[...])
    first = lax.broadcasted_iota(jnp.int32, p.shape, 0) == 0
    p_prev = jnp.where(first, pcar_ref[...], pltpu.roll(p, 1, 0))
    pcar_ref[...] = p[rows - 1:rows, :]
    xl_ref[0] = xn[rows - 1:rows, :]
    pl_ref[0] = p[rows - 1:rows, :]
    cl_ref[0] = c[rows - 1:rows, :]
    _inproj_outputs(xn, qkv, logf, c, p, p_prev,
                    (mu_ref, sel_ref, ones_ref, w2_ref, a2_ref, g2_ref, w0_ref, a0_ref),
                    (q_ref, k_ref, kb_ref, v_ref, vb_ref, lf_ref, cq_ref, ck_ref,
                     r_ref, lw_ref, kr_ref, vr_ref, a_ref, g_ref))


def _inproj_packed_kernel(x_ref, xprev_ref, ln_ref, wqkv_ref, wf_ref, bf_ref, wrw_ref,
                          mu_ref, sel_ref, ones_ref, w2_ref, a2_ref, g2_ref, w0_ref, a0_ref,
                          q_ref, k_ref, kb_ref, v_ref, vb_ref, lf_ref, cq_ref, ck_ref,
                          r_ref, lw_ref, kr_ref, vr_ref, a_ref, g_ref, xn_ref, *, seq_len):
    rows = x_ref.shape[1]
    shift = seq_len.bit_length() - 1
    xn = _rms(x_ref[0], ln_ref[...])
    xn_ref[0] = xn
    xb = xn.astype(BF16)
    qkv = _dot(xb, wqkv_ref[...])
    logf = -_softplus(-(_dot(xb, wf_ref[...]) + bf_ref[...]))
    ri = lax.broadcasted_iota(jnp.int32, (rows, rows), 0)
    ci = lax.broadcasted_iota(jnp.int32, (rows, rows), 1)
    same_seq = (ri >> shift) == (ci >> shift)
    c = _select_sum(((ri >= ci) & same_seq).astype(F32), logf)
    p = _dot(xb, wrw_ref[...])
    p_first = _dot(xprev_ref[0].astype(BF16), wrw_ref[...])
    first = (lax.broadcasted_iota(jnp.int32, p.shape, 0) & (seq_len - 1)) == 0
    p_prev = jnp.where(first, p_first, pltpu.roll(p, 1, 0))
    _inproj_outputs(xn, qkv, logf, c, p, p_prev,
                    (mu_ref, sel_ref, ones_ref, w2_ref, a2_ref, g2_ref, w0_ref, a0_ref),
                    (q_ref, k_ref, kb_ref, v_ref, vb_ref, lf_ref, cq_ref, ck_ref,
                     r_ref, lw_ref, kr_ref, vr_ref, a_ref, g_ref))


def _const_spec(arr):
    return pl.BlockSpec(arr.shape, lambda *_: (0,) * arr.ndim)


def _inproj_out_shapes(batch, rows_total):
    def s(width, dtype):
        return jax.ShapeDtypeStruct((batch, rows_total, width), dtype)
    logf_t = jax.ShapeDtypeStruct((batch, N_HEADS, rows_total), F32)
    return [s(WIDTH, BF16), s(WIDTH, F32), s(WIDTH, BF16), s(WIDTH, F32), s(WIDTH, BF16),
            logf_t, s(LANES, BF16), s(LANES, BF16)] + [s(WIDTH, F32)] * 6


def _inproj_out_specs(tile):
    def s(width):
        return pl.BlockSpec((1, tile, width), lambda b, i: (b, i, 0))
    logf_t = pl.BlockSpec((1, N_HEADS, tile), lambda b, i: (b, 0, i))
    return [s(WIDTH)] * 5 + [logf_t, s(LANES), s(LANES)] + [s(WIDTH)] * 6


def _inproj_long(x, prow, c0, weights, tile):
    batch, seq, _ = x.shape
    n_tiles = seq // tile
    row = lambda width: pl.BlockSpec((1, 1, width), lambda b, i: (b, 0, 0))
    shapes = _inproj_out_shapes(batch, seq) + [
        jax.ShapeDtypeStruct((batch, 1, D_MODEL), F32),
        jax.ShapeDtypeStruct((batch, 1, RWKV_COLS), F32),
        jax.ShapeDtypeStruct((batch, 1, LANES), F32)]
    specs = _inproj_out_specs(tile) + [row(D_MODEL), row(RWKV_COLS), row(LANES)]
    return pl.pallas_call(
        _inproj_long_kernel,
        grid=(batch, n_tiles),
        in_specs=[pl.BlockSpec((1, tile, D_MODEL), lambda b, i: (b, i, 0)),
                  _const_spec(prow), _const_spec(c0)] + [_const_spec(w) for w in weights],
        out_specs=specs,
        out_shape=shapes,
        scratch_shapes=[pltpu.VMEM((1, RWKV_COLS), F32), pltpu.VMEM((1, LANES), F32)],
        compiler_params=pltpu.CompilerParams(
            dimension_semantics=("arbitrary", "arbitrary"), vmem_limit_bytes=VMEM_LIMIT),
        name="inproj_long",
    )(x, prow, c0, *weights)


def _inproj_packed(x, xprev, weights, seq_len):
    _, rows, _ = x.shape
    full = pl.BlockSpec((1, rows, D_MODEL), lambda b, i: (0, 0, 0))
    return pl.pallas_call(
        functools.partial(_inproj_packed_kernel, seq_len=seq_len),
        grid=(1, 1),
        in_specs=[full, full] + [_const_spec(w) for w in weights],
        out_specs=_inproj_out_specs(rows) + [full],
        out_shape=_inproj_out_shapes(1, rows) + [jax.ShapeDtypeStruct((1, rows, D_MODEL), F32)],
        compiler_params=pltpu.CompilerParams(
            dimension_semantics=("arbitrary", "arbitrary"), vmem_limit_bytes=VMEM_LIMIT),
        name="inproj_packed",
    )(x, xprev, *weights)


def _fox_prompt_kernel(q_ref, cq_ref, km_ref, ckm_ref, vm_ref, k_ref, ck_ref, v_ref, o_ref):
    pair = pl.program_id(1)
    qi = pl.program_id(2)
    tile = q_ref.shape[1]
    lane = lax.broadcasted_iota(jnp.int32, (1, LANES), 1)
    q2 = q_ref[0]
    cq = cq_ref[0]
    n_sub = tile // ATT_SUB
    lhs = []
    for hh in range(2):
        head_lanes = (lane >= hh * HEAD_DIM) & (lane < (hh + 1) * HEAD_DIM)
        g0 = (2 * pair + hh) * BIAS_GROUP
        group_lanes = (lane >= g0) & (lane < g0 + BIAS_GROUP)
        full = jnp.concatenate([q2 * _mask_bf16(head_lanes), cq * _mask_bf16(group_lanes)], axis=1)
        lhs.append([full[r * ATT_SUB:(r + 1) * ATT_SUB, :] for r in range(n_sub)])
    query_minus_key = (lax.broadcasted_iota(jnp.int32, (ATT_K_TILE, ATT_SUB), 1)
                       - lax.broadcasted_iota(jnp.int32, (ATT_K_TILE, ATT_SUB), 0))

    def block(carry, kk, vv, first_col):
        out = list(carry)
        live = [(hh, r) for hh in range(2) for r in range(n_sub)
                if first_col is None or first_col <= (r + 1) * ATT_SUB - 1]
        for g0 in range(0, len(live), ATT_GROUP):
            group = live[g0:g0 + ATT_GROUP]
            scores, m_news = [], []
            for hh, r in group:
                s = _dot_nt(kk, lhs[hh][r])
                if first_col is not None and first_col + kk.shape[0] - 1 > r * ATT_SUB:
                    s = jnp.where(query_minus_key >= first_col - r * ATT_SUB, s, NEG_BIG)
                scores.append(s)
            for (hh, r), s in zip(group, scores):
                m_news.append(jnp.maximum(out[hh * n_sub + r][0], jnp.max(s, axis=0, keepdims=True)))
            for (hh, r), s, m_new in zip(group, scores, m_news):
                m, l, acc = out[hh * n_sub + r]
                alpha = jnp.exp(m - m_new)
                pe = jnp.exp(s - m_new)
                l_new = alpha * l + jnp.sum(pe, axis=0, keepdims=True)
                acc_new = alpha * acc + _dot_tn(vv, pe.astype(BF16))
                out[hh * n_sub + r] = (m_new, l_new, acc_new)
        return tuple(out)

    init = tuple((jnp.full((1, ATT_SUB), NEG_BIG, F32), jnp.zeros((1, ATT_SUB), F32),
                  jnp.zeros((LANES, ATT_SUB), F32)) for _ in range(2 * n_sub))
    carry = block(init, jnp.concatenate([km_ref[0], ckm_ref[0]], axis=1), vm_ref[0], None)

    def keys(j):
        start = pl.multiple_of(j * ATT_K_TILE, ATT_K_TILE)
        rows = pl.ds(start, ATT_K_TILE)
        return jnp.concatenate([k_ref[0, rows, :], ck_ref[0, rows, :]], axis=1), v_ref[0, rows, :]

    per_q = tile // ATT_K_TILE
    carry = lax.fori_loop(0, qi * per_q, lambda j, c: block(c, *keys(j), None), carry)
    for d in range(per_q):
        carry = block(carry, *keys(qi * per_q + d), d * ATT_K_TILE)
    first_head = lax.broadcasted_iota(jnp.int32, (LANES, ATT_SUB), 0) < HEAD_DIM
    for r in range(n_sub):
        (_, l0, acc0), (_, l1, acc1) = carry[r], carry[n_sub + r]
        o_t = jnp.where(first_head, acc0 / l0, acc1 / l1)
        o_ref[0, r * ATT_SUB:(r + 1) * ATT_SUB, :] = o_t.T.astype(BF16)


def _fox_prompt(q, cq, k_meta, ck_meta, v_meta, k, ck, v):
    batch, seq, _ = q.shape
    n_pairs = N_HEADS // 2
    return pl.pallas_call(
        _fox_prompt_kernel,
        grid=(batch, n_pairs, seq // ATT_TILE),
        in_specs=[
            pl.BlockSpec((1, ATT_TILE, LANES), lambda b, p, i: (b, i, p)),
            pl.BlockSpec((1, ATT_TILE, LANES), lambda b, p, i: (b, i, 0)),
            pl.BlockSpec((1, N_META, LANES), lambda b, p, i: (0, 0, p)),
            pl.BlockSpec((1, N_META, LANES), lambda b, p, i: (0, 0, 0)),
            pl.BlockSpec((1, N_META, LANES), lambda b, p, i: (0, 0, p)),
            pl.BlockSpec((1, seq, LANES), lambda b, p, i: (b, 0, p)),
            pl.BlockSpec((1, seq, LANES), lambda b, p, i: (b, 0, 0)),
            pl.BlockSpec((1, seq, LANES), lambda b, p, i: (b, 0, p)),
        ],
        out_specs=pl.BlockSpec((1, ATT_TILE, LANES), lambda b, p, i: (b, i, p)),
        out_shape=jax.ShapeDtypeStruct((batch, seq, WIDTH), BF16),
        compiler_params=pltpu.CompilerParams(
            dimension_semantics=("arbitrary", "arbitrary", "arbitrary"),
            vmem_limit_bytes=VMEM_LIMIT),
        name="fox_prompt",
    )(q, cq, k_meta, ck_meta, v_meta, k, ck, v)


def _fox_sample_kernel(pt_ref, q_ref, cq_ref, kn_ref, ckn_ref, vn_ref, *rest, n_new):
    del pt_ref
    g_pages = PAGES_PER_STEP
    k_refs, v_refs, lf_refs = rest[:g_pages], rest[g_pages:2 * g_pages], rest[2 * g_pages:3 * g_pages]
    o_ref, q_sc, cqm_sc, roff_sc, carry_sc, m_sc, l_sc, acc_sc = rest[3 * g_pages:]
    step = pl.program_id(1)
    rows = N_HEADS * n_new
    row_head = lax.broadcasted_iota(jnp.int32, (rows, 1), 0) & (N_HEADS - 1)

    def per_head_rows(x):
        return jnp.concatenate([jnp.broadcast_to(x[i:i + 1, :], (N_HEADS, x.shape[1]))
                                for i in range(n_new)], axis=0)

    @pl.when(step == 0)
    def _():
        q = per_head_rows(q_ref[0].astype(F32))
        lane_head = lax.broadcasted_iota(jnp.int32, (1, WIDTH), 1) >> HEAD_SHIFT
        q_sc[...] = jnp.where(row_head == lane_head, q, 0.0).astype(BF16)
        cq = per_head_rows(cq_ref[0].astype(F32))
        lane = lax.broadcasted_iota(jnp.int32, (1, LANES), 1)
        cqm = jnp.where((lane >> GROUP_SHIFT) == row_head, cq, 0.0)
        cqm_sc[...] = cqm.astype(BF16)
        slot = lax.broadcasted_iota(jnp.int32, cqm.shape, 1) & (BIAS_GROUP - 1)
        roff_sc[...] = jnp.sum(jnp.where(slot < BIAS_PARTS, cqm, 0.0), axis=-1, keepdims=True)
        carry_sc[...] = jnp.zeros(carry_sc.shape, F32)
        m_sc[...] = jnp.full(m_sc.shape, NEG_BIG, F32)
        l_sc[...] = jnp.zeros(l_sc.shape, F32)
        acc_sc[...] = jnp.zeros(acc_sc.shape, F32)

    q_bd = q_sc[...]
    ti = lax.broadcasted_iota(jnp.int32, (PAGE_SIZE, PAGE_SIZE), 0)
    tj = lax.broadcasted_iota(jnp.int32, (PAGE_SIZE, PAGE_SIZE), 1)
    later_and_all = jnp.concatenate([(ti > tj).astype(F32), jnp.ones((PAGE_SIZE, PAGE_SIZE), F32)],
                                    axis=1)
    lf_all = jnp.concatenate([lf_refs[g][0] for g in range(g_pages)], axis=0)
    sums = _sum_select(lf_all, later_and_all)

    carry = carry_sc[...]
    roff = roff_sc[...]
    def pages_t(refs, g0):
        return jnp.concatenate([refs[g][0].reshape(WIDTH, PAGE_SIZE) for g in (g0, g0 + 1)],
                               axis=1).astype(BF16)

    scores = []
    for g0 in range(0, g_pages, 2):
        bias = []
        for g in (g0, g0 + 1):
            page_sums = sums[g * N_HEADS:(g + 1) * N_HEADS, :]
            bias.append(jnp.concatenate([page_sums[:, :PAGE_SIZE] + carry] * n_new, axis=0))
            carry = carry + page_sums[:, PAGE_SIZE:]
        scores.append(_dot(q_bd, pages_t(k_refs, g0)) + jnp.concatenate(bias, axis=1) + roff)
    carry_sc[...] = carry

    m = m_sc[...]
    m_new = m
    for s in scores:
        m_new = jnp.maximum(m_new, jnp.max(s, axis=-1, keepdims=True))
    alpha = jnp.exp(m - m_new)
    l_new = alpha * l_sc[...]
    acc = alpha * acc_sc[...]
    for i, s in enumerate(scores):
        pe = jnp.exp(s - m_new)
        l_new = l_new + jnp.sum(pe, axis=-1, keepdims=True)
        acc = acc + _dot_nt(pe.astype(BF16), pages_t(v_refs, 2 * i))
    m_sc[...] = m_new
    l_sc[...] = l_new
    acc_sc[...] = acc

    @pl.when(step == pl.num_programs(1) - 1)
    def _():
        s = _dot_nt(q_bd, kn_ref[0]) + _dot_nt(cqm_sc[...], ckn_ref[0])
        row_tok = lax.broadcasted_iota(jnp.int32, (rows, 1), 0) >> (N_HEADS.bit_length() - 1)
        s = jnp.where(lax.broadcasted_iota(jnp.int32, (1, n_new), 1) <= row_tok, s, NEG_BIG)
        m_fin = jnp.maximum(m_new, jnp.max(s, axis=-1, keepdims=True))
        a_fin = jnp.exp(m_new - m_fin)
        pe = jnp.exp(s - m_fin)
        l_fin = a_fin * l_new + jnp.sum(pe, axis=-1, keepdims=True)
        out = (a_fin * acc + _dot(pe.astype(BF16), vn_ref[0])) / l_fin
        own = (lax.broadcasted_iota(jnp.int32, (N_HEADS, WIDTH), 0)
               == lax.broadcasted_iota(jnp.int32, (N_HEADS, WIDTH), 1) >> HEAD_SHIFT)
        o_ref[0] = jnp.concatenate(
            [jnp.sum(jnp.where(own, out[i * N_HEADS:(i + 1) * N_HEADS, :], 0.0), axis=0, keepdims=True)
             for i in range(n_new)], axis=0).astype(BF16)


def _fox_sample(page_table, q, cq, k_new, ck_new, v_new, cache_k, cache_v, cache_logf):
    n_seq, n_new, _ = q.shape
    n_pages = page_table.shape[1]
    g_pages = PAGES_PER_STEP
    rows = N_HEADS * n_new

    def per_seq(width):
        return pl.BlockSpec((1, n_new, width), lambda b, j, pt: (b, 0, 0))

    def page_spec(g, arr):
        block = (1,) + arr.shape[1:]
        return pl.BlockSpec(block, lambda b, j, pt: (pt[b, n_pages - 1 - (j * g_pages + g)],)
                            + (0,) * (arr.ndim - 1))

    grid_spec = pltpu.PrefetchScalarGridSpec(
        num_scalar_prefetch=1,
        grid=(n_seq, n_pages // g_pages),
        in_specs=([per_seq(WIDTH), per_seq(LANES), per_seq(WIDTH), per_seq(LANES), per_seq(WIDTH)]
                  + [page_spec(g, cache_k) for g in range(g_pages)]
                  + [page_spec(g, cache_v) for g in range(g_pages)]
                  + [page_spec(g, cache_logf) for g in range(g_pages)]),
        out_specs=pl.BlockSpec((1, n_new, WIDTH), lambda b, j, pt: (b, 0, 0)),
        scratch_shapes=[pltpu.VMEM((rows, WIDTH), BF16), pltpu.VMEM((rows, LANES), BF16),
                        pltpu.VMEM((rows, 1), F32), pltpu.VMEM((N_HEADS, PAGE_SIZE), F32),
                        pltpu.VMEM((rows, 1), F32), pltpu.VMEM((rows, 1), F32),
                        pltpu.VMEM((rows, WIDTH), F32)],
    )
    return pl.pallas_call(
        functools.partial(_fox_sample_kernel, n_new=n_new),
        grid_spec=grid_spec,
        out_shape=jax.ShapeDtypeStruct((n_seq, n_new, WIDTH), BF16),
        compiler_params=pltpu.CompilerParams(
            dimension_semantics=("arbitrary", "arbitrary"), vmem_limit_bytes=VMEM_LIMIT),
        name="fox_sample",
    )(page_table, q, cq, k_new, ck_new, v_new,
      *([cache_k] * g_pages), *([cache_v] * g_pages), *([cache_logf] * g_pages))


def _wkv_kernel(r_ref, lw_ref, kr_ref, vr_ref, a_ref, g_ref, s0_ref,
                kk_ref, ka_ref, rk_ref, gg_ref, gb_ref, o_ref, so_ref, s_sc):
    nb, C, _ = r_ref.shape
    n_pairs = N_HEADS // 2
    units = [(bb, pair) for bb in range(nb) for pair in range(n_pairs)]
    lo_state = ((lax.broadcasted_iota(jnp.int32, (LANES, LANES), 0) < HEAD_DIM)
                == (lax.broadcasted_iota(jnp.int32, (LANES, LANES), 1) < HEAD_DIM))

    @pl.when(pl.program_id(1) == 0)
    def _():
        zeros = jnp.zeros((HEAD_DIM, HEAD_DIM), F32)
        for bb, pair in units:
            sb = bb if s0_ref.shape[0] == nb else 0
            top = jnp.concatenate([s0_ref[sb, 2 * pair], zeros], axis=1)
            bottom = jnp.concatenate([zeros, s0_ref[sb, 2 * pair + 1]], axis=1)
            s_sc[bb, pair] = jnp.concatenate([top, bottom], axis=0)

    ri = lax.broadcasted_iota(jnp.int32, (C, C), 0)
    ci = lax.broadcasted_iota(jnp.int32, (C, C), 1)
    incl = ri >= ci
    strict = ri > ci
    eye = (ri == ci).astype(F32)
    lo = lax.broadcasted_iota(jnp.int32, (C, LANES), 1) < HEAD_DIM
    lo2 = lax.broadcasted_iota(jnp.int32, (2 * C, LANES), 1) < HEAD_DIM
    levels = max(1, (C - 1).bit_length())
    lcum = [_select_sum(incl.astype(F32), lw_ref[bb]) for bb in range(nb)]

    def per_head(x):
        first = jnp.sum(jnp.where(lo, x, 0.0), axis=-1, keepdims=True)
        second = jnp.sum(jnp.where(lo, 0.0, x), axis=-1, keepdims=True)
        return jnp.where(lo, first, second)

    xs, x_all, bts, kts, bgs, kgs, vbs, kps, e_ends = [], [], [], [], [], [], [], [], []
    for bb, pair in units:
        lanes = slice(pair * LANES, (pair + 1) * LANES)
        r2, lw2, kr2, a2 = (ref[bb, :, lanes] for ref in (r_ref, lw_ref, kr_ref, a_ref))
        lc = lcum[bb][:, lanes]
        kkr = kr2 * kk_ref[:, lanes]
        kk = kkr / jnp.maximum(jnp.sqrt(per_head(kkr * kkr)), 1e-12)
        kp = kr2 * (1.0 + (a2 - 1.0) * ka_ref[:, lanes])
        bv = kk * a2
        e_neg = jnp.exp(-lc)
        l_end = lc[C - 1:C, :]
        e_rem = jnp.exp(l_end - lc)
        x2 = jnp.concatenate([-kk * jnp.exp(lc - lw2), r2 * jnp.exp(lc)], axis=0)
        x_all.append(x2.astype(BF16))
        xs.append((jnp.where(lo2, x2, 0.0).astype(BF16), jnp.where(lo2, 0.0, x2).astype(BF16)))
        bts.append((bv * e_neg).astype(BF16))
        kts.append((kp * e_neg).astype(BF16))
        bgs.append((bv * e_rem).astype(BF16))
        kgs.append((kp * e_rem).astype(BF16))
        vbs.append(vr_ref[bb, :, lanes].astype(BF16))
        kps.append(kp)
        e_ends.append(jnp.exp(l_end))

    chains = [(u, hh) for u in range(len(units)) for hh in range(2)]
    sc_b = [_dot_nt(xs[u][hh], bts[u]) for u, hh in chains]
    sc_k = [_dot_nt(xs[u][hh], kts[u]) for u, hh in chains]
    n_ch = range(len(chains))
    a_ab = [jnp.where(strict, sc_b[i][:C], 0.0) for i in n_ch]
    a_ak = [jnp.where(strict, sc_k[i][:C], 0.0).astype(BF16) for i in n_ch]
    a_rb = [jnp.where(incl, sc_b[i][C:], 0.0).astype(BF16) for i in n_ch]
    a_rk = [jnp.where(incl, sc_k[i][C:], 0.0).astype(BF16) for i in n_ch]

    tinv = [eye + a_ab[i] for i in n_ch]
    if levels > 1:
        lb = [a_ab[i].astype(BF16) for i in n_ch]
        lpow = [_dot(lb[i], lb[i]) for i in n_ch]
        for level in range(1, levels):
            qb = [lpow[i].astype(BF16) for i in n_ch]
            if level == levels - 1:
                tinv = [tinv[i] + _dot(tinv[i].astype(BF16), qb[i]) for i in n_ch]
            else:
                both = [_dot(jnp.concatenate([tinv[i], lpow[i]], axis=0).astype(BF16), qb[i])
                        for i in n_ch]
                tinv = [tinv[i] + both[i][:C] for i in n_ch]
                lpow = [both[i][C:] for i in n_ch]
    tb = [tinv[i].astype(BF16) for i in n_ch]

    def both_heads(u, f):
        return jnp.where(lo, f(2 * u), f(2 * u + 1))

    n_u = range(len(units))
    s_old = [s_sc[bb, pair] for bb, pair in units]
    ps = [_dot_nt(x_all[u], s_old[u].astype(BF16)) for u in n_u]
    w1 = [(ps[u][:C] + both_heads(u, lambda i, u=u: _dot(a_ak[i], vbs[u]))).astype(BF16) for u in n_u]
    ub = [both_heads(u, lambda i, u=u: _dot(tb[i], w1[u])).astype(BF16) for u in n_u]
    ys = [ps[u][C:] + both_heads(u, lambda i, u=u: _dot(a_rb[i], ub[u]) + _dot(a_rk[i], vbs[u]))
          for u in n_u]
    for u, (bb, pair) in enumerate(units):
        cross = _dot_tn(ub[u], bgs[u]) + _dot_tn(vbs[u], kgs[u])
        s_sc[bb, pair] = s_old[u] * e_ends[u] + jnp.where(lo_state, cross, 0.0)

    for u, (bb, pair) in enumerate(units):
        lanes = slice(pair * LANES, (pair + 1) * LANES)
        y = ys[u]
        yc = y - per_head(y) * (1.0 / HEAD_DIM)
        var = per_head(yc * yc) * (1.0 / HEAD_DIM)
        yn = yc * lax.rsqrt(var + GN_EPS) * gg_ref[:, lanes] + gb_ref[:, lanes]
        bonus = per_head(r_ref[bb, :, lanes] * kps[u] * rk_ref[:, lanes]) * vr_ref[bb, :, lanes]
        o_ref[bb, :, lanes] = ((yn + bonus) * g_ref[bb, :, lanes]).astype(BF16)

    @pl.when(pl.program_id(1) == pl.num_programs(1) - 1)
    def _():
        for bb, pair in units:
            s2 = s_sc[bb, pair]
            so_ref[bb, 2 * pair] = s2[:HEAD_DIM, :HEAD_DIM]
            so_ref[bb, 2 * pair + 1] = pltpu.roll(s2[HEAD_DIM:, :], HEAD_DIM, 1)[:, :HEAD_DIM]


def _wkv(r, lw, kr, vr, a, g, s0, params, chunk, batch_tile):
    batch, seq, _ = r.shape
    shared_state = s0.shape[0] == 1
    tok = pl.BlockSpec((batch_tile, chunk, WIDTH), lambda b, c: (b, c, 0))
    state_out = pl.BlockSpec((batch_tile, N_HEADS, HEAD_DIM, HEAD_DIM), lambda b, c: (b, 0, 0, 0))
    state_in = (pl.BlockSpec((1, N_HEADS, HEAD_DIM, HEAD_DIM), lambda b, c: (0, 0, 0, 0))
                if shared_state else state_out)
    return pl.pallas_call(
        _wkv_kernel,
        grid=(batch // batch_tile, seq // chunk),
        in_specs=[tok] * 6 + [state_in] + [_const_spec(p) for p in params],
        out_specs=[tok, state_out],
        out_shape=[jax.ShapeDtypeStruct((batch, seq, WIDTH), BF16),
                   jax.ShapeDtypeStruct((batch, N_HEADS, HEAD_DIM, HEAD_DIM), F32)],
        scratch_shapes=[pltpu.VMEM((batch_tile, N_HEADS // 2, LANES, LANES), F32)],
        compiler_params=pltpu.CompilerParams(
            dimension_semantics=("arbitrary", "arbitrary"), vmem_limit_bytes=VMEM_LIMIT),
        name="wkv",
    )(r, lw, kr, vr, a, g, s0, *params)


def _out_ffn_kernel(x_ref, att_ref, rw_ref, woa_ref, wor_ref, ln2_ref, wg_ref, wu_ref, wd_ref,
                    lnf_ref, y_ref):
    h = x_ref[...] + _dot(att_ref[...], woa_ref[...]) + _dot(rw_ref[...], wor_ref[...])
    hb = _rms(h, ln2_ref[...]).astype(BF16)
    ffn = jnp.zeros(h.shape, F32)
    d_ff = wg_ref.shape[1]
    for c0 in range(0, d_ff, FF_CHUNK):
        gate = _dot(hb, wg_ref[:, c0:c0 + FF_CHUNK])
        up = _dot(hb, wu_ref[:, c0:c0 + FF_CHUNK])
        act = gate * _sigmoid(gate) * up
        ffn = ffn + _dot(act.astype(BF16), wd_ref[c0:c0 + FF_CHUNK, :])
    y_ref[...] = _rms(h + ffn, lnf_ref[...])


def _out_ffn(x, att, rw, weights):
    rows = x.shape[0]
    tile = min(FFN_ROW_TILE, rows)
    tok = lambda width: pl.BlockSpec((tile, width), lambda i: (i, 0))
    resident = lambda w: pl.BlockSpec(w.shape, lambda i: (0,) * w.ndim, pipeline_mode=pl.Buffered(1))
    return pl.pallas_call(
        _out_ffn_kernel,
        grid=(rows // tile,),
        in_specs=[tok(D_MODEL), tok(WIDTH), tok(WIDTH)] + [resident(w) for w in weights],
        out_specs=tok(D_MODEL),
        out_shape=jax.ShapeDtypeStruct((rows, D_MODEL), F32),
        compiler_params=pltpu.CompilerParams(
            dimension_semantics=("arbitrary",), vmem_limit_bytes=VMEM_LIMIT),
        name="out_ffn",
    )(x, att, rw, *weights)


def _bias_selectors():
    part = jnp.arange(BIAS_PARTS)[:, None, None]
    src = jnp.arange(LANES)[None, :, None]
    dst = jnp.arange(LANES)[None, None, :]
    is_head = src < N_HEADS
    q_sel = (is_head & (dst == src * BIAS_GROUP + part)).astype(F32)
    k_sel = -(is_head & (dst == src * BIAS_GROUP + BIAS_PARTS + part)).astype(F32)
    sel = jnp.concatenate([q_sel, k_sel], axis=2).reshape(BIAS_PARTS * LANES, 2 * LANES)
    slot = jnp.arange(LANES) % BIAS_GROUP
    ones = jnp.concatenate([(slot >= BIAS_PARTS) & (slot < 2 * BIAS_PARTS),
                            slot < BIAS_PARTS]).astype(F32)[None, :]
    return sel.astype(BF16), ones


def kernel(x_prompt, x_sample, cache_k, cache_v, cache_logf, state_wkv, state_shift, page_table,
           meta_tokens, ln1_g, w_in, b_f, mu_shift, w0, w2, a0, a2, g2, k_k, k_a, r_k,
           gn_g, gn_b, w_out, ln2_g, w_gate, w_up, w_down, lnf_g):
    batch, seq, _ = x_prompt.shape
    n_seq, n_new, _ = x_sample.shape
    fox_cols = 3 * WIDTH + N_HEADS

    w_in0 = w_in[0]
    wqkv = w_in0[:, :3 * WIDTH].astype(BF16)
    wf = jnp.pad(w_in0[:, 3 * WIDTH:fox_cols], ((0, 0), (0, LANES - N_HEADS))).astype(BF16)
    bf = jnp.pad(b_f[0], (0, LANES - N_HEADS))[None, :]
    wrw = w_in0[:, fox_cols:].astype(BF16)
    sel, ones = _bias_selectors()
    w2p = jnp.concatenate([w2[0], jnp.zeros((ICLR_LORA, WIDTH), F32)], axis=0).astype(BF16)
    a2p = jnp.concatenate([jnp.zeros((DECAY_LORA, WIDTH), F32), a2[0]], axis=0).astype(BF16)
    inproj_w = (ln1_g[0][None, :], wqkv, wf, bf, wrw, mu_shift[0][None, :], sel, ones,
                w2p, a2p, g2[0].astype(BF16), w0[0][None, :], a0[0][None, :])
    wkv_p = tuple(t[0].reshape(1, WIDTH) for t in (k_k, k_a, r_k, gn_g, gn_b))
    ffn_w = (w_out[0][:WIDTH].astype(BF16), w_out[0][WIDTH:].astype(BF16), ln2_g[0][None, :],
             w_gate[0].astype(BF16), w_up[0].astype(BF16), w_down[0].astype(BF16), lnf_g[None, :])

    zrow = jnp.zeros((1, 1, RWKV_COLS), F32)
    zc = jnp.zeros((1, 1, LANES), F32)
    mo = _inproj_long(meta_tokens[None], zrow, zc, inproj_w, N_META)
    (_, mk, mkb, mv, mvb, mlf, _, mck, mr, mlw, mkr, mvr, ma, mg, _, mprow, mclast) = mo
    zstate = jnp.zeros((1, N_HEADS, HEAD_DIM, HEAD_DIM), F32)
    _, s_meta = _wkv(mr, mlw, mkr, mvr, ma, mg, zstate, wkv_p, N_META, 1)

    po = _inproj_long(x_prompt, mprow, mclast, inproj_w, ROW_TILE)
    (pq, pk, pkb, pv, pvb, plf, pcq, pck, pr, plw, pkr, pvr, pa, pg, pxl, _, _) = po
    att_p = _fox_prompt(pq, pcq, mkb, mck, mvb, pkb, pck, pvb)
    rw_p, wkv_prompt = _wkv(pr, plw, pkr, pvr, pa, pg, s_meta, wkv_p, WKV_CHUNK, WKV_PROMPT_SEQS)
    y_prompt = _out_ffn(x_prompt.reshape(batch * seq, D_MODEL), att_p.reshape(batch * seq, WIDTH),
                        rw_p.reshape(batch * seq, WIDTH), ffn_w).reshape(batch, seq, D_MODEL)

    rows_s = n_seq * n_new
    xprev = jnp.repeat(state_shift[0], n_new, axis=0)[None]
    so = _inproj_packed(x_sample.reshape(1, rows_s, D_MODEL), xprev, inproj_w, n_new)
    (sq, sk, skb, sv, svb, slf, scq, sck, sr, slw, skr, svr, sa, sg, sxn) = so
    seqs = lambda t: t.reshape(n_seq, n_new, t.shape[-1])
    att_s = _fox_sample(page_table, seqs(sq), seqs(scq), seqs(skb), seqs(sck), seqs(svb),
                        jnp.transpose(cache_k[0], (0, 2, 3, 1)), jnp.transpose(cache_v[0], (0, 2, 3, 1)),
                        jnp.transpose(cache_logf[0], (0, 2, 1)))
    rw_s, wkv_sample = _wkv(seqs(sr), seqs(slw), seqs(skr), seqs(svr), seqs(sa), seqs(sg),
                            state_wkv[0], wkv_p, n_new, WKV_SAMPLE_SEQS)
    y_sample = _out_ffn(x_sample.reshape(rows_s, D_MODEL), att_s.reshape(rows_s, WIDTH),
                        rw_s.reshape(rows_s, WIDTH), ffn_w).reshape(n_seq, n_new, D_MODEL)

    def with_meta(meta, main):
        return jnp.concatenate([jnp.broadcast_to(meta, (batch,) + meta.shape[1:]), main], axis=1)

    heads = lambda t: t.reshape(t.shape[:-1] + (N_HEADS, HEAD_DIM))
    k_prompt = heads(with_meta(mk, pk))[None]
    v_prompt = heads(with_meta(mv, pv))[None]
    logf_prompt = jnp.concatenate([jnp.broadcast_to(mlf, (batch,) + mlf.shape[1:]), plf], axis=2)
    logf_prompt = jnp.transpose(logf_prompt, (0, 2, 1))[None]
    shift_prompt = pxl.reshape(1, batch, D_MODEL)
    k_sample = heads(seqs(sk))[None]
    v_sample = heads(seqs(sv))[None]
    logf_sample = jnp.transpose(slf.reshape(N_HEADS, n_seq, n_new), (1, 2, 0))[None]
    shift_sample = seqs(sxn)[:, -1][None]
    return (y_prompt, y_sample, k_prompt, v_prompt, logf_prompt, wkv_prompt[None], shift_prompt,
            k_sample, v_sample, logf_sample, wkv_sample[None], shift_sample)
```

```python
import functools

import jax
import jax.numpy as jnp
from jax import lax
from jax.experimental import pallas as pl
from jax.experimental.pallas import tpu as pltpu

D_MODEL = 1024
N_META = 16
HEAD_DIM = 64
N_HEADS = 8
WIDTH = N_HEADS * HEAD_DIM
PAGE_SIZE = 128
DECAY_LORA = 64
ICLR_LORA = 64
GATE_LORA = 128
RWKV_COLS = 3 * WIDTH + DECAY_LORA + ICLR_LORA + GATE_LORA
RMS_EPS = 1e-6
GN_EPS = 64e-5
NEG_BIG = -1e30

LANES = 128
BIAS_GROUP = 16
BIAS_PARTS = 3
HEAD_SHIFT = HEAD_DIM.bit_length() - 1
GROUP_SHIFT = BIAS_GROUP.bit_length() - 1
VMEM_LIMIT = 56 * 1024 * 1024

ROW_TILE = 512
FFN_ROW_TILE = 512
ATT_TILE = 2048
ATT_K_TILE = 512
ATT_SUB = 256
ATT_GROUP = 8
WKV_CHUNK = 64
WKV_PROMPT_SEQS = 8
WKV_SAMPLE_SEQS = 4
PAGES_PER_STEP = 32
FF_CHUNK = 256

BF16 = jnp.bfloat16
F32 = jnp.float32


def _dot(a, b):
    return jnp.dot(a, b, preferred_element_type=F32)


def _dot_nt(a, b):
    return lax.dot_general(a, b, (((1,), (1,)), ((), ())), preferred_element_type=F32)


def _dot_tn(a, b):
    return lax.dot_general(a, b, (((0,), (0,)), ((), ())), preferred_element_type=F32)


def _rms(x, g):
    return x * lax.rsqrt(jnp.mean(x * x, axis=-1, keepdims=True) + RMS_EPS) * g


def _softplus(z):
    return jnp.maximum(z, 0.0) + jnp.log(1.0 + jnp.exp(-jnp.abs(z)))


def _sigmoid(z):
    return 1.0 / (1.0 + jnp.exp(-z))


def _mask_bf16(cond):
    return cond.astype(F32).astype(BF16)


def _split3(c):
    hi = c.astype(BF16)
    r1 = c - hi.astype(F32)
    mid = r1.astype(BF16)
    lo = (r1 - mid.astype(F32)).astype(BF16)
    return hi, mid, lo


def _select_sum(sel, x):
    n = x.shape[1]
    out = _dot(sel.astype(BF16), jnp.concatenate(_split3(x), axis=1))
    return out[:, :n] + out[:, n:2 * n] + out[:, 2 * n:]


def _sum_select(x, sel):
    m = x.shape[0]
    out = _dot(jnp.concatenate(_split3(x), axis=0), sel.astype(BF16))
    return out[:m] + out[m:2 * m] + out[2 * m:]


def _inproj_outputs(xn, qkv, logf, c, p, p_prev, w, outs):
    (mu_ref, sel_ref, ones_ref, w2_ref, a2_ref, g2_ref, w0_ref, a0_ref) = w
    (q_ref, k_ref, kb_ref, v_ref, vb_ref, lf_ref, cq_ref, ck_ref,
     r_ref, lw_ref, kr_ref, vr_ref, a_ref, g_ref) = outs
    q_ref[0] = (qkv[:, :WIDTH] * (HEAD_DIM ** -0.5)).astype(BF16)
    k = qkv[:, WIDTH:2 * WIDTH]
    k_ref[0] = k
    kb_ref[0] = k.astype(BF16)
    v = qkv[:, 2 * WIDTH:]
    v_ref[0] = v
    vb_ref[0] = v.astype(BF16)
    lf_ref[0] = logf.T[:N_HEADS, :]
    cparts = jnp.concatenate(_split3(c), axis=1)
    cc = _dot(cparts, sel_ref[...]) + ones_ref[...]
    cq_ref[0] = cc[:, :LANES].astype(BF16)
    ck_ref[0] = cc[:, LANES:].astype(BF16)
    rw = p + (p_prev - p) * mu_ref[...]
    r_ref[0] = rw[:, :WIDTH]
    kr_ref[0] = rw[:, WIDTH:2 * WIDTH]
    vr_ref[0] = rw[:, 2 * WIDTH:3 * WIDTH]
    z = rw[:, 3 * WIDTH:3 * WIDTH + LANES]
    gl = rw[:, 3 * WIDTH + LANES:]
    w_log = -_softplus(-(w0_ref[...] + _dot(jnp.tanh(z).astype(BF16), w2_ref[...]))) - 0.5
    lw_ref[0] = -jnp.exp(w_log)
    a_ref[0] = _sigmoid(a0_ref[...] + _dot(z.astype(BF16), a2_ref[...]))
    g_ref[0] = _dot(_sigmoid(gl).astype(BF16), g2_ref[...])


def _inproj_long_kernel(x_ref, prow_ref, c0_ref, ln_ref, wqkv_ref, wf_ref, bf_ref, wrw_ref,
                        mu_ref, sel_ref, ones_ref, w2_ref, a2_ref, g2_ref, w0_ref, a0_ref,
                        q_ref, k_ref, kb_ref, v_ref, vb_ref, lf_ref, cq_ref, ck_ref,
                        r_ref, lw_ref, kr_ref, vr_ref, a_ref, g_ref, xl_ref, pl_ref, cl_ref,
                        pcar_ref, ccar_ref):
    @pl.when(pl.program_id(1) == 0)
    def _():
        pcar_ref[...] = prow_ref[0]
        ccar_ref[...] = c0_ref[0]

    rows = x_ref.shape[1]
    xn = _rms(x_ref[0], ln_ref[...])
    xb = xn.astype(BF16)
    qkv = _dot(xb, wqkv_ref[...])
    logf = -_softplus(-(_dot(xb, wf_ref[...]) + bf_ref[...]))
    ri = lax.broadcasted_iota(jnp.int32, (rows, rows), 0)
    ci = lax.broadcasted_iota(jnp.int32, (rows, rows), 1)
    c = _select_sum((ri >= ci).astype(F32), logf) + ccar_ref[...]
    ccar_ref[...] = c[rows - 1:rows, :]
    p = _dot(xb, wrw_ref[...])
    first = lax.broadcasted_iota(jnp.int32, p.shape, 0) == 0
    p_prev = jnp.where(first, pcar_ref[...], pltpu.roll(p, 1, 0))
    pcar_ref[...] = p[rows - 1:rows, :]
    xl_ref[0] = xn[rows - 1:rows, :]
    pl_ref[0] = p[rows - 1:rows, :]
    cl_ref[0] = c[rows - 1:rows, :]
    _inproj_outputs(xn, qkv, logf, c, p, p_prev,
                    (mu_ref, sel_ref, ones_ref, w2_ref, a2_ref, g2_ref, w0_ref, a0_ref),
                    (q_ref, k_ref, kb_ref, v_ref, vb_ref, lf_ref, cq_ref, ck_ref,
                     r_ref, lw_ref, kr_ref, vr_ref, a_ref, g_ref))


def _inproj_packed_kernel(x_ref, xprev_ref, ln_ref, wqkv_ref, wf_ref, bf_ref, wrw_ref,
                          mu_ref, sel_ref, ones_ref, w2_ref, a2_ref, g2_ref, w0_ref, a0_ref,
                          q_ref, k_ref, kb_ref, v_ref, vb_ref, lf_ref, cq_ref, ck_ref,
                          r_ref, lw_ref, kr_ref, vr_ref, a_ref, g_ref, xn_ref, *, seq_len):
    rows = x_ref.shape[1]
    shift = seq_len.bit_length() - 1
    xn = _rms(x_ref[0], ln_ref[...])
    xn_ref[0] = xn
    xb = xn.astype(BF16)
    qkv = _dot(xb, wqkv_ref[...])
    logf = -_softplus(-(_dot(xb, wf_ref[...]) + bf_ref[...]))
    ri = lax.broadcasted_iota(jnp.int32, (rows, rows), 0)
    ci = lax.broadcasted_iota(jnp.int32, (rows, rows), 1)
    same_seq = (ri >> shift) == (ci >> shift)
    c = _select_sum(((ri >= ci) & same_seq).astype(F32), logf)
    p = _dot(xb, wrw_ref[...])
    p_first = _dot(xprev_ref[0].astype(BF16), wrw_ref[...])
    first = (lax.broadcasted_iota(jnp.int32, p.shape, 0) & (seq_len - 1)) == 0
    p_prev = jnp.where(first, p_first, pltpu.roll(p, 1, 0))
    _inproj_outputs(xn, qkv, logf, c, p, p_prev,
                    (mu_ref, sel_ref, ones_ref, w2_ref, a2_ref, g2_ref, w0_ref, a0_ref),
                    (q_ref, k_ref, kb_ref, v_ref, vb_ref, lf_ref, cq_ref, ck_ref,
                     r_ref, lw_ref, kr_ref, vr_ref, a_ref, g_ref))


def _const_spec(arr):
    return pl.BlockSpec(arr.shape, lambda *_: (0,) * arr.ndim)


def _inproj_out_shapes(batch, rows_total):
    def s(width, dtype):
        return jax.ShapeDtypeStruct((batch, rows_total, width), dtype)
    logf_t = jax.ShapeDtypeStruct((batch, N_HEADS, rows_total), F32)
    return [s(WIDTH, BF16), s(WIDTH, F32), s(WIDTH, BF16), s(WIDTH, F32), s(WIDTH, BF16),
            logf_t, s(LANES, BF16), s(LANES, BF16)] + [s(WIDTH, F32)] * 6


def _inproj_out_specs(tile):
    def s(width):
        return pl.BlockSpec((1, tile, width), lambda b, i: (b, i, 0))
    logf_t = pl.BlockSpec((1, N_HEADS, tile), lambda b, i: (b, 0, i))
    return [s(WIDTH)] * 5 + [logf_t, s(LANES), s(LANES)] + [s(WIDTH)] * 6


def _inproj_long(x, prow, c0, weights, tile):
    batch, seq, _ = x.shape
    n_tiles = seq // tile
    row = lambda width: pl.BlockSpec((1, 1, width), lambda b, i: (b, 0, 0))
    shapes = _inproj_out_shapes(batch, seq) + [
        jax.ShapeDtypeStruct((batch, 1, D_MODEL), F32),
        jax.ShapeDtypeStruct((batch, 1, RWKV_COLS), F32),
        jax.ShapeDtypeStruct((batch, 1, LANES), F32)]
    specs = _inproj_out_specs(tile) + [row(D_MODEL), row(RWKV_COLS), row(LANES)]
    return pl.pallas_call(
        _inproj_long_kernel,
        grid=(batch, n_tiles),
        in_specs=[pl.BlockSpec((1, tile, D_MODEL), lambda b, i: (b, i, 0)),
                  _const_spec(prow), _const_spec(c0)] + [_const_spec(w) for w in weights],
        out_specs=specs,
        out_shape=shapes,
        scratch_shapes=[pltpu.VMEM((1, RWKV_COLS), F32), pltpu.VMEM((1, LANES), F32)],
        compiler_params=pltpu.CompilerParams(
            dimension_semantics=("arbitrary", "arbitrary"), vmem_limit_bytes=VMEM_LIMIT),
        name="inproj_long",
    )(x, prow, c0, *weights)


def _inproj_packed(x, xprev, weights, seq_len):
    _, rows, _ = x.shape
    full = pl.BlockSpec((1, rows, D_MODEL), lambda b, i: (0, 0, 0))
    return pl.pallas_call(
        functools.partial(_inproj_packed_kernel, seq_len=seq_len),
        grid=(1, 1),
        in_specs=[full, full] + [_const_spec(w) for w in weights],
        out_specs=_inproj_out_specs(rows) + [full],
        out_shape=_inproj_out_shapes(1, rows) + [jax.ShapeDtypeStruct((1, rows, D_MODEL), F32)],
        compiler_params=pltpu.CompilerParams(
            dimension_semantics=("arbitrary", "arbitrary"), vmem_limit_bytes=VMEM_LIMIT),
        name="inproj_packed",
    )(x, xprev, *weights)


def _fox_prompt_kernel(q_ref, cq_ref, km_ref, ckm_ref, vm_ref, k_ref, ck_ref, v_ref, o_ref):
    pair = pl.program_id(1)
    qi = pl.program_id(2)
    tile = q_ref.shape[1]
    lane = lax.broadcasted_iota(jnp.int32, (1, LANES), 1)
    q2 = q_ref[0]
    cq = cq_ref[0]
    n_sub = tile // ATT_SUB
    lhs = []
    for hh in range(2):
        head_lanes = (lane >= hh * HEAD_DIM) & (lane < (hh + 1) * HEAD_DIM)
        g0 = (2 * pair + hh) * BIAS_GROUP
        group_lanes = (lane >= g0) & (lane < g0 + BIAS_GROUP)
        full = jnp.concatenate([q2 * _mask_bf16(head_lanes), cq * _mask_bf16(group_lanes)], axis=1)
        lhs.append([full[r * ATT_SUB:(r + 1) * ATT_SUB, :] for r in range(n_sub)])
    query_minus_key = (lax.broadcasted_iota(jnp.int32, (ATT_K_TILE, ATT_SUB), 1)
                       - lax.broadcasted_iota(jnp.int32, (ATT_K_TILE, ATT_SUB), 0))

    def block(carry, kk, vv, first_col):
        out = list(carry)
        live = [(hh, r) for hh in range(2) for r in range(n_sub)
                if first_col is None or first_col <= (r + 1) * ATT_SUB - 1]
        for g0 in range(0, len(live), ATT_GROUP):
            group = live[g0:g0 + ATT_GROUP]
            scores, m_news = [], []
            for hh, r in group:
                s = _dot_nt(kk, lhs[hh][r])
                if first_col is not None and first_col + kk.shape[0] - 1 > r * ATT_SUB:
                    s = jnp.where(query_minus_key >= first_col - r * ATT_SUB, s, NEG_BIG)
                scores.append(s)
            for (hh, r), s in zip(group, scores):
                m_news.append(jnp.maximum(out[hh * n_sub + r][0], jnp.max(s, axis=0, keepdims=True)))
            for (hh, r), s, m_new in zip(group, scores, m_news):
                m, l, acc = out[hh * n_sub + r]
                alpha = jnp.exp(m - m_new)
                pe = jnp.exp(s - m_new)
                l_new = alpha * l + jnp.sum(pe, axis=0, keepdims=True)
                acc_new = alpha * acc + _dot_tn(vv, pe.astype(BF16))
                out[hh * n_sub + r] = (m_new, l_new, acc_new)
        return tuple(out)

    init = tuple((jnp.full((1, ATT_SUB), NEG_BIG, F32), jnp.zeros((1, ATT_SUB), F32),
                  jnp.zeros((LANES, ATT_SUB), F32)) for _ in range(2 * n_sub))
    carry = block(init, jnp.concatenate([km_ref[0], ckm_ref[0]], axis=1), vm_ref[0], None)

    def keys(j):
        start = pl.multiple_of(j * ATT_K_TILE, ATT_K_TILE)
        rows = pl.ds(start, ATT_K_TILE)
        return jnp.concatenate([k_ref[0, rows, :], ck_ref[0, rows, :]], axis=1), v_ref[0, rows, :]

    per_q = tile // ATT_K_TILE
    carry = lax.fori_loop(0, qi * per_q, lambda j, c: block(c, *keys(j), None), carry)
    for d in range(per_q):
        carry = block(carry, *keys(qi * per_q + d), d * ATT_K_TILE)
    first_head = lax.broadcasted_iota(jnp.int32, (LANES, ATT_SUB), 0) < HEAD_DIM
    for r in range(n_sub):
        (_, l0, acc0), (_, l1, acc1) = carry[r], carry[n_sub + r]
        o_t = jnp.where(first_head, acc0 / l0, acc1 / l1)
        o_ref[0, r * ATT_SUB:(r + 1) * ATT_SUB, :] = o_t.T.astype(BF16)


def _fox_prompt(q, cq, k_meta, ck_meta, v_meta, k, ck, v):
    batch, seq, _ = q.shape
    n_pairs = N_HEADS // 2
    return pl.pallas_call(
        _fox_prompt_kernel,
        grid=(batch, n_pairs, seq // ATT_TILE),
        in_specs=[
            pl.BlockSpec((1, ATT_TILE, LANES), lambda b, p, i: (b, i, p)),
            pl.BlockSpec((1, ATT_TILE, LANES), lambda b, p, i: (b, i, 0)),
            pl.BlockSpec((1, N_META, LANES), lambda b, p, i: (0, 0, p)),
            pl.BlockSpec((1, N_META, LANES), lambda b, p, i: (0, 0, 0)),
            pl.BlockSpec((1, N_META, LANES), lambda b, p, i: (0, 0, p)),
            pl.BlockSpec((1, seq, LANES), lambda b, p, i: (b, 0, p)),
            pl.BlockSpec((1, seq, LANES), lambda b, p, i: (b, 0, 0)),
            pl.BlockSpec((1, seq, LANES), lambda b, p, i: (b, 0, p)),
        ],
        out_specs=pl.BlockSpec((1, ATT_TILE, LANES), lambda b, p, i: (b, i, p)),
        out_shape=jax.ShapeDtypeStruct((batch, seq, WIDTH), BF16),
        compiler_params=pltpu.CompilerParams(
            dimension_semantics=("arbitrary", "arbitrary", "arbitrary"),
            vmem_limit_bytes=VMEM_LIMIT),
        name="fox_prompt",
    )(q, cq, k_meta, ck_meta, v_meta, k, ck, v)


def _fox_sample_kernel(pt_ref, q_ref, cq_ref, kn_ref, ckn_ref, vn_ref, *rest, n_new):
    del pt_ref
    g_pages = PAGES_PER_STEP
    k_refs, v_refs, lf_refs = rest[:g_pages], rest[g_pages:2 * g_pages], rest[2 * g_pages:3 * g_pages]
    o_ref, q_sc, cqm_sc, roff_sc, carry_sc, m_sc, l_sc, acc_sc = rest[3 * g_pages:]
    step = pl.program_id(1)
    rows = N_HEADS * n_new
    row_head = lax.broadcasted_iota(jnp.int32, (rows, 1), 0) & (N_HEADS - 1)

    def per_head_rows(x):
        return jnp.concatenate([jnp.broadcast_to(x[i:i + 1, :], (N_HEADS, x.shape[1]))
                                for i in range(n_new)], axis=0)

    @pl.when(step == 0)
    def _():
        q = per_head_rows(q_ref[0].astype(F32))
        lane_head = lax.broadcasted_iota(jnp.int32, (1, WIDTH), 1) >> HEAD_SHIFT
        q_sc[...] = jnp.where(row_head == lane_head, q, 0.0).astype(BF16)
        cq = per_head_rows(cq_ref[0].astype(F32))
        lane = lax.broadcasted_iota(jnp.int32, (1, LANES), 1)
        cqm = jnp.where((lane >> GROUP_SHIFT) == row_head, cq, 0.0)
        cqm_sc[...] = cqm.astype(BF16)
        slot = lax.broadcasted_iota(jnp.int32, cqm.shape, 1) & (BIAS_GROUP - 1)
        roff_sc[...] = jnp.sum(jnp.where(slot < BIAS_PARTS, cqm, 0.0), axis=-1, keepdims=True)
        carry_sc[...] = jnp.zeros(carry_sc.shape, F32)
        m_sc[...] = jnp.full(m_sc.shape, NEG_BIG, F32)
        l_sc[...] = jnp.zeros(l_sc.shape, F32)
        acc_sc[...] = jnp.zeros(acc_sc.shape, F32)

    q_bd = q_sc[...]
    ti = lax.broadcasted_iota(jnp.int32, (PAGE_SIZE, PAGE_SIZE), 0)
    tj = lax.broadcasted_iota(jnp.int32, (PAGE_SIZE, PAGE_SIZE), 1)
    later_and_all = jnp.concatenate([(ti > tj).astype(F32), jnp.ones((PAGE_SIZE, PAGE_SIZE), F32)],
                                    axis=1)
    lf_all = jnp.concatenate([lf_refs[g][0] for g in range(g_pages)], axis=0)
    sums = _sum_select(lf_all, later_and_all)

    carry = carry_sc[...]
    roff = roff_sc[...]
    def pages_t(refs, g0):
        return jnp.concatenate([refs[g][0].reshape(WIDTH, PAGE_SIZE) for g in (g0, g0 + 1)],
                               axis=1).astype(BF16)

    scores = []
    for g0 in range(0, g_pages, 2):
        bias = []
        for g in (g0, g0 + 1):
            page_sums = sums[g * N_HEADS:(g + 1) * N_HEADS, :]
            bias.append(jnp.concatenate([page_sums[:, :PAGE_SIZE] + carry] * n_new, axis=0))
            carry = carry + page_sums[:, PAGE_SIZE:]
        scores.append(_dot(q_bd, pages_t(k_refs, g0)) + jnp.concatenate(bias, axis=1) + roff)
    carry_sc[...] = carry

    m = m_sc[...]
    m_new = m
    for s in scores:
        m_new = jnp.maximum(m_new, jnp.max(s, axis=-1, keepdims=True))
    alpha = jnp.exp(m - m_new)
    l_new = alpha * l_sc[...]
    acc = alpha * acc_sc[...]
    for i, s in enumerate(scores):
        pe = jnp.exp(s - m_new)
        l_new = l_new + jnp.sum(pe, axis=-1, keepdims=True)
        acc = acc + _dot_nt(pe.astype(BF16), pages_t(v_refs, 2 * i))
    m_sc[...] = m_new
    l_sc[...] = l_new
    acc_sc[...] = acc

    @pl.when(step == pl.num_programs(1) - 1)
    def _():
        s = _dot_nt(q_bd, kn_ref[0]) + _dot_nt(cqm_sc[...], ckn_ref[0])
        row_tok = lax.broadcasted_iota(jnp.int32, (rows, 1), 0) >> (N_HEADS.bit_length() - 1)
        s = jnp.where(lax.broadcasted_iota(jnp.int32, (1, n_new), 1) <= row_tok, s, NEG_BIG)
        m_fin = jnp.maximum(m_new, jnp.max(s, axis=-1, keepdims=True))
        a_fin = jnp.exp(m_new - m_fin)
        pe = jnp.exp(s - m_fin)
        l_fin = a_fin * l_new + jnp.sum(pe, axis=-1, keepdims=True)
        out = (a_fin * acc + _dot(pe.astype(BF16), vn_ref[0])) / l_fin
        own = (lax.broadcasted_iota(jnp.int32, (N_HEADS, WIDTH), 0)
               == lax.broadcasted_iota(jnp.int32, (N_HEADS, WIDTH), 1) >> HEAD_SHIFT)
        o_ref[0] = jnp.concatenate(
            [jnp.sum(jnp.where(own, out[i * N_HEADS:(i + 1) * N_HEADS, :], 0.0), axis=0, keepdims=True)
             for i in range(n_new)], axis=0).astype(BF16)


def _fox_sample(page_table, q, cq, k_new, ck_new, v_new, cache_k, cache_v, cache_logf):
    n_seq, n_new, _ = q.shape
    n_pages = page_table.shape[1]
    g_pages = PAGES_PER_STEP
    rows = N_HEADS * n_new

    def per_seq(width):
        return pl.BlockSpec((1, n_new, width), lambda b, j, pt: (b, 0, 0))

    def page_spec(g, arr):
        block = (1,) + arr.shape[1:]
        return pl.BlockSpec(block, lambda b, j, pt: (pt[b, n_pages - 1 - (j * g_pages + g)],)
                            + (0,) * (arr.ndim - 1))

    grid_spec = pltpu.PrefetchScalarGridSpec(
        num_scalar_prefetch=1,
        grid=(n_seq, n_pages // g_pages),
        in_specs=([per_seq(WIDTH), per_seq(LANES), per_seq(WIDTH), per_seq(LANES), per_seq(WIDTH)]
                  + [page_spec(g, cache_k) for g in range(g_pages)]
                  + [page_spec(g, cache_v) for g in range(g_pages)]
                  + [page_spec(g, cache_logf) for g in range(g_pages)]),
        out_specs=pl.BlockSpec((1, n_new, WIDTH), lambda b, j, pt: (b, 0, 0)),
        scratch_shapes=[pltpu.VMEM((rows, WIDTH), BF16), pltpu.VMEM((rows, LANES), BF16),
                        pltpu.VMEM((rows, 1), F32), pltpu.VMEM((N_HEADS, PAGE_SIZE), F32),
                        pltpu.VMEM((rows, 1), F32), pltpu.VMEM((rows, 1), F32),
                        pltpu.VMEM((rows, WIDTH), F32)],
    )
    return pl.pallas_call(
        functools.partial(_fox_sample_kernel, n_new=n_new),
        grid_spec=grid_spec,
        out_shape=jax.ShapeDtypeStruct((n_seq, n_new, WIDTH), BF16),
        compiler_params=pltpu.CompilerParams(
            dimension_semantics=("arbitrary", "arbitrary"), vmem_limit_bytes=VMEM_LIMIT),
        name="fox_sample",
    )(page_table, q, cq, k_new, ck_new, v_new,
      *([cache_k] * g_pages), *([cache_v] * g_pages), *([cache_logf] * g_pages))


def _wkv_kernel(r_ref, lw_ref, kr_ref, vr_ref, a_ref, g_ref, s0_ref,
                kk_ref, ka_ref, rk_ref, gg_ref, gb_ref, o_ref, so_ref, s_sc):
    nb, C, _ = r_ref.shape
    n_pairs = N_HEADS // 2
    units = [(bb, pair) for bb in range(nb) for pair in range(n_pairs)]
    lo_state = ((lax.broadcasted_iota(jnp.int32, (LANES, LANES), 0) < HEAD_DIM)
                == (lax.broadcasted_iota(jnp.int32, (LANES, LANES), 1) < HEAD_DIM))

    @pl.when(pl.program_id(1) == 0)
    def _():
        zeros = jnp.zeros((HEAD_DIM, HEAD_DIM), F32)
        for bb, pair in units:
            sb = bb if s0_ref.shape[0] == nb else 0
            top = jnp.concatenate([s0_ref[sb, 2 * pair], zeros], axis=1)
            bottom = jnp.concatenate([zeros, s0_ref[sb, 2 * pair + 1]], axis=1)
            s_sc[bb, pair] = jnp.concatenate([top, bottom], axis=0)

    ri = lax.broadcasted_iota(jnp.int32, (C, C), 0)
    ci = lax.broadcasted_iota(jnp.int32, (C, C), 1)
    incl = ri >= ci
    strict = ri > ci
    eye = (ri == ci).astype(F32)
    lo = lax.broadcasted_iota(jnp.int32, (C, LANES), 1) < HEAD_DIM
    lo2 = lax.broadcasted_iota(jnp.int32, (2 * C, LANES), 1) < HEAD_DIM
    levels = max(1, (C - 1).bit_length())
    lcum = [_select_sum(incl.astype(F32), lw_ref[bb]) for bb in range(nb)]

    def per_head(x):
        first = jnp.sum(jnp.where(lo, x, 0.0), axis=-1, keepdims=True)
        second = jnp.sum(jnp.where(lo, 0.0, x), axis=-1, keepdims=True)
        return jnp.where(lo, first, second)

    xs, x_all, bts, kts, bgs, kgs, vbs, kps, e_ends = [], [], [], [], [], [], [], [], []
    for bb, pair in units:
        lanes = slice(pair * LANES, (pair + 1) * LANES)
        r2, lw2, kr2, a2 = (ref[bb, :, lanes] for ref in (r_ref, lw_ref, kr_ref, a_ref))
        lc = lcum[bb][:, lanes]
        kkr = kr2 * kk_ref[:, lanes]
        kk = kkr / jnp.maximum(jnp.sqrt(per_head(kkr * kkr)), 1e-12)
        kp = kr2 * (1.0 + (a2 - 1.0) * ka_ref[:, lanes])
        bv = kk * a2
        e_neg = jnp.exp(-lc)
        l_end = lc[C - 1:C, :]
        e_rem = jnp.exp(l_end - lc)
        x2 = jnp.concatenate([-kk * jnp.exp(lc - lw2), r2 * jnp.exp(lc)], axis=0)
        x_all.append(x2.astype(BF16))
        xs.append((jnp.where(lo2, x2, 0.0).astype(BF16), jnp.where(lo2, 0.0, x2).astype(BF16)))
        bts.append((bv * e_neg).astype(BF16))
        kts.append((kp * e_neg).astype(BF16))
        bgs.append((bv * e_rem).astype(BF16))
        kgs.append((kp * e_rem).astype(BF16))
        vbs.append(vr_ref[bb, :, lanes].astype(BF16))
        kps.append(kp)
        e_ends.append(jnp.exp(l_end))

    chains = [(u, hh) for u in range(len(units)) for hh in range(2)]
    sc_b = [_dot_nt(xs[u][hh], bts[u]) for u, hh in chains]
    sc_k = [_dot_nt(xs[u][hh], kts[u]) for u, hh in chains]
    n_ch = range(len(chains))
    a_ab = [jnp.where(strict, sc_b[i][:C], 0.0) for i in n_ch]
    a_ak = [jnp.where(strict, sc_k[i][:C], 0.0).astype(BF16) for i in n_ch]
    a_rb = [jnp.where(incl, sc_b[i][C:], 0.0).astype(BF16) for i in n_ch]
    a_rk = [jnp.where(incl, sc_k[i][C:], 0.0).astype(BF16) for i in n_ch]

    tinv = [eye + a_ab[i] for i in n_ch]
    if levels > 1:
        lb = [a_ab[i].astype(BF16) for i in n_ch]
        lpow = [_dot(lb[i], lb[i]) for i in n_ch]
        for level in range(1, levels):
            qb = [lpow[i].astype(BF16) for i in n_ch]
            if level == levels - 1:
                tinv = [tinv[i] + _dot(tinv[i].astype(BF16), qb[i]) for i in n_ch]
            else:
                both = [_dot(jnp.concatenate([tinv[i], lpow[i]], axis=0).astype(BF16), qb[i])
                        for i in n_ch]
                tinv = [tinv[i] + both[i][:C] for i in n_ch]
                lpow = [both[i][C:] for i in n_ch]
    tb = [tinv[i].astype(BF16) for i in n_ch]

    def both_heads(u, f):
        return jnp.where(lo, f(2 * u), f(2 * u + 1))

    n_u = range(len(units))
    s_old = [s_sc[bb, pair] for bb, pair in units]
    ps = [_dot_nt(x_all[u], s_old[u].astype(BF16)) for u in n_u]
    w1 = [(ps[u][:C] + both_heads(u, lambda i, u=u: _dot(a_ak[i], vbs[u]))).astype(BF16) for u in n_u]
    ub = [both_heads(u, lambda i, u=u: _dot(tb[i], w1[u])).astype(BF16) for u in n_u]
    ys = [ps[u][C:] + both_heads(u, lambda i, u=u: _dot(a_rb[i], ub[u]) + _dot(a_rk[i], vbs[u]))
          for u in n_u]
    for u, (bb, pair) in enumerate(units):
        cross = _dot_tn(ub[u], bgs[u]) + _dot_tn(vbs[u], kgs[u])
        s_sc[bb, pair] = s_old[u] * e_ends[u] + jnp.where(lo_state, cross, 0.0)

    for u, (bb, pair) in enumerate(units):
        lanes = slice(pair * LANES, (pair + 1) * LANES)
        y = ys[u]
        yc = y - per_head(y) * (1.0 / HEAD_DIM)
        var = per_head(yc * yc) * (1.0 / HEAD_DIM)
        yn = yc * lax.rsqrt(var + GN_EPS) * gg_ref[:, lanes] + gb_ref[:, lanes]
        bonus = per_head(r_ref[bb, :, lanes] * kps[u] * rk_ref[:, lanes]) * vr_ref[bb, :, lanes]
        o_ref[bb, :, lanes] = ((yn + bonus) * g_ref[bb, :, lanes]).astype(BF16)

    @pl.when(pl.program_id(1) == pl.num_programs(1) - 1)
    def _():
        for bb, pair in units:
            s2 = s_sc[bb, pair]
            so_ref[bb, 2 * pair] = s2[:HEAD_DIM, :HEAD_DIM]
            so_ref[bb, 2 * pair + 1] = pltpu.roll(s2[HEAD_DIM:, :], HEAD_DIM, 1)[:, :HEAD_DIM]


def _wkv(r, lw, kr, vr, a, g, s0, params, chunk, batch_tile):
    batch, seq, _ = r.shape
    shared_state = s0.shape[0] == 1
    tok = pl.BlockSpec((batch_tile, chunk, WIDTH), lambda b, c: (b, c, 0))
    state_out = pl.BlockSpec((batch_tile, N_HEADS, HEAD_DIM, HEAD_DIM), lambda b, c: (b, 0, 0, 0))
    state_in = (pl.BlockSpec((1, N_HEADS, HEAD_DIM, HEAD_DIM), lambda b, c: (0, 0, 0, 0))
                if shared_state else state_out)
    return pl.pallas_call(
        _wkv_kernel,
        grid=(batch // batch_tile, seq // chunk),
        in_specs=[tok] * 6 + [state_in] + [_const_spec(p) for p in params],
        out_specs=[tok, state_out],
        out_shape=[jax.ShapeDtypeStruct((batch, seq, WIDTH), BF16),
                   jax.ShapeDtypeStruct((batch, N_HEADS, HEAD_DIM, HEAD_DIM), F32)],
        scratch_shapes=[pltpu.VMEM((batch_tile, N_HEADS // 2, LANES, LANES), F32)],
        compiler_params=pltpu.CompilerParams(
            dimension_semantics=("arbitrary", "arbitrary"), vmem_limit_bytes=VMEM_LIMIT),
        name="wkv",
    )(r, lw, kr, vr, a, g, s0, *params)


def _out_ffn_kernel(x_ref, att_ref, rw_ref, woa_ref, wor_ref, ln2_ref, wg_ref, wu_ref, wd_ref,
                    lnf_ref, y_ref):
    h = x_ref[...] + _dot(att_ref[...], woa_ref[...]) + _dot(rw_ref[...], wor_ref[...])
    hb = _rms(h, ln2_ref[...]).astype(BF16)
    ffn = jnp.zeros(h.shape, F32)
    d_ff = wg_ref.shape[1]
    for c0 in range(0, d_ff, FF_CHUNK):
        gate = _dot(hb, wg_ref[:, c0:c0 + FF_CHUNK])
        up = _dot(hb, wu_ref[:, c0:c0 + FF_CHUNK])
        act = gate * _sigmoid(gate) * up
        ffn = ffn + _dot(act.astype(BF16), wd_ref[c0:c0 + FF_CHUNK, :])
    y_ref[...] = _rms(h + ffn, lnf_ref[...])


def _out_ffn(x, att, rw, weights):
    rows = x.shape[0]
    tile = min(FFN_ROW_TILE, rows)
    tok = lambda width: pl.BlockSpec((tile, width), lambda i: (i, 0))
    resident = lambda w: pl.BlockSpec(w.shape, lambda i: (0,) * w.ndim, pipeline_mode=pl.Buffered(1))
    return pl.pallas_call(
        _out_ffn_kernel,
        grid=(rows // tile,),
        in_specs=[tok(D_MODEL), tok(WIDTH), tok(WIDTH)] + [resident(w) for w in weights],
        out_specs=tok(D_MODEL),
        out_shape=jax.ShapeDtypeStruct((rows, D_MODEL), F32),
        compiler_params=pltpu.CompilerParams(
            dimension_semantics=("arbitrary",), vmem_limit_bytes=VMEM_LIMIT),
        name="out_ffn",
    )(x, att, rw, *weights)


def _bias_selectors():
    part = jnp.arange(BIAS_PARTS)[:, None, None]
    src = jnp.arange(LANES)[None, :, None]
    dst = jnp.arange(LANES)[None, None, :]
    is_head = src < N_HEADS
    q_sel = (is_head & (dst == src * BIAS_GROUP + part)).astype(F32)
    k_sel = -(is_head & (dst == src * BIAS_GROUP + BIAS_PARTS + part)).astype(F32)
    sel = jnp.concatenate([q_sel, k_sel], axis=2).reshape(BIAS_PARTS * LANES, 2 * LANES)
    slot = jnp.arange(LANES) % BIAS_GROUP
    ones = jnp.concatenate([(slot >= BIAS_PARTS) & (slot < 2 * BIAS_PARTS),
                            slot < BIAS_PARTS]).astype(F32)[None, :]
    return sel.astype(BF16), ones


def kernel(x_prompt, x_sample, cache_k, cache_v, cache_logf, state_wkv, state_shift, page_table,
           meta_tokens, ln1_g, w_in, b_f, mu_shift, w0, w2, a0, a2, g2, k_k, k_a, r_k,
           gn_g, gn_b, w_out, ln2_g, w_gate, w_up, w_down, lnf_g):
    batch, seq, _ = x_prompt.shape
    n_seq, n_new, _ = x_sample.shape
    fox_cols = 3 * WIDTH + N_HEADS

    w_in0 = w_in[0]
    wqkv = w_in0[:, :3 * WIDTH].astype(BF16)
    wf = jnp.pad(w_in0[:, 3 * WIDTH:fox_cols], ((0, 0), (0, LANES - N_HEADS))).astype(BF16)
    bf = jnp.pad(b_f[0], (0, LANES - N_HEADS))[None, :]
    wrw = w_in0[:, fox_cols:].astype(BF16)
    sel, ones = _bias_selectors()
    w2p = jnp.concatenate([w2[0], jnp.zeros((ICLR_LORA, WIDTH), F32)], axis=0).astype(BF16)
    a2p = jnp.concatenate([jnp.zeros((DECAY_LORA, WIDTH), F32), a2[0]], axis=0).astype(BF16)
    inproj_w = (ln1_g[0][None, :], wqkv, wf, bf, wrw, mu_shift[0][None, :], sel, ones,
                w2p, a2p, g2[0].astype(BF16), w0[0][None, :], a0[0][None, :])
    wkv_p = tuple(t[0].reshape(1, WIDTH) for t in (k_k, k_a, r_k, gn_g, gn_b))
    ffn_w = (w_out[0][:WIDTH].astype(BF16), w_out[0][WIDTH:].astype(BF16), ln2_g[0][None, :],
             w_gate[0].astype(BF16), w_up[0].astype(BF16), w_down[0].astype(BF16), lnf_g[None, :])

    zrow = jnp.zeros((1, 1, RWKV_COLS), F32)
    zc = jnp.zeros((1, 1, LANES), F32)
    mo = _inproj_long(meta_tokens[None], zrow, zc, inproj_w, N_META)
    (_, mk, mkb, mv, mvb, mlf, _, mck, mr, mlw, mkr, mvr, ma, mg, _, mprow, mclast) = mo
    zstate = jnp.zeros((1, N_HEADS, HEAD_DIM, HEAD_DIM), F32)
    _, s_meta = _wkv(mr, mlw, mkr, mvr, ma, mg, zstate, wkv_p, N_META, 1)

    po = _inproj_long(x_prompt, mprow, mclast, inproj_w, ROW_TILE)
    (pq, pk, pkb, pv, pvb, plf, pcq, pck, pr, plw, pkr, pvr, pa, pg, pxl, _, _) = po
    att_p = _fox_prompt(pq, pcq, mkb, mck, mvb, pkb, pck, pvb)
    rw_p, wkv_prompt = _wkv(pr, plw, pkr, pvr, pa, pg, s_meta, wkv_p, WKV_CHUNK, WKV_PROMPT_SEQS)
    y_prompt = _out_ffn(x_prompt.reshape(batch * seq, D_MODEL), att_p.reshape(batch * seq, WIDTH),
                        rw_p.reshape(batch * seq, WIDTH), ffn_w).reshape(batch, seq, D_MODEL)

    rows_s = n_seq * n_new
    xprev = jnp.repeat(state_shift[0], n_new, axis=0)[None]
    so = _inproj_packed(x_sample.reshape(1, rows_s, D_MODEL), xprev, inproj_w, n_new)
    (sq, sk, skb, sv, svb, slf, scq, sck, sr, slw, skr, svr, sa, sg, sxn) = so
    seqs = lambda t: t.reshape(n_seq, n_new, t.shape[-1])
    att_s = _fox_sample(page_table, seqs(sq), seqs(scq), seqs(skb), seqs(sck), seqs(svb),
                        jnp.transpose(cache_k[0], (0, 2, 3, 1)), jnp.transpose(cache_v[0], (0, 2, 3, 1)),
                        jnp.transpose(cache_logf[0], (0, 2, 1)))
    rw_s, wkv_sample = _wkv(seqs(sr), seqs(slw), seqs(skr), seqs(svr), seqs(sa), seqs(sg),
                            state_wkv[0], wkv_p, n_new, WKV_SAMPLE_SEQS)
    y_sample = _out_ffn(x_sample.reshape(rows_s, D_MODEL), att_s.reshape(rows_s, WIDTH),
                        rw_s.reshape(rows_s, WIDTH), ffn_w).reshape(n_seq, n_new, D_MODEL)

    def with_meta(meta, main):
        return jnp.concatenate([jnp.broadcast_to(meta, (batch,) + meta.shape[1:]), main], axis=1)

    heads = lambda t: t.reshape(t.shape[:-1] + (N_HEADS, HEAD_DIM))
    k_prompt = heads(with_meta(mk, pk))[None]
    v_prompt = heads(with_meta(mv, pv))[None]
    logf_prompt = jnp.concatenate([jnp.broadcast_to(mlf, (batch,) + mlf.shape[1:]), plf], axis=2)
    logf_prompt = jnp.transpose(logf_prompt, (0, 2, 1))[None]
    shift_prompt = pxl.reshape(1, batch, D_MODEL)
    k_sample = heads(seqs(sk))[None]
    v_sample = heads(seqs(sv))[None]
    logf_sample = jnp.transpose(slf.reshape(N_HEADS, n_seq, n_new), (1, 2, 0))[None]
    shift_sample = seqs(sxn)[:, -1][None]
    return (y_prompt, y_sample, k_prompt, v_prompt, logf_prompt, wkv_prompt[None], shift_prompt,
            k_sample, v_sample, logf_sample, wkv_sample[None], shift_sample)
```

```python
import functools

import jax
import jax.numpy as jnp
from jax import lax
from jax.experimental import pallas as pl
from jax.experimental.pallas import tpu as pltpu

D_MODEL = 1024
N_META = 16
HEAD_DIM = 64
N_HEADS = 8
WIDTH = N_HEADS * HEAD_DIM
PAGE_SIZE = 128
DECAY_LORA = 64
ICLR_LORA = 64
GATE_LORA = 128
RWKV_COLS = 3 * WIDTH + DECAY_LORA + ICLR_LORA + GATE_LORA
RMS_EPS = 1e-6
GN_EPS = 64e-5
NEG_BIG = -1e30

LANES = 128
BIAS_GROUP = 16
BIAS_PARTS = 3
HEAD_SHIFT = HEAD_DIM.bit_length() - 1
GROUP_SHIFT = BIAS_GROUP.bit_length() - 1
VMEM_LIMIT = 56 * 1024 * 1024

ROW_TILE = 512
FFN_ROW_TILE = 512
ATT_TILE = 2048
ATT_K_TILE = 512
ATT_SUB = 256
ATT_GROUP = 16
WKV_CHUNK = 64
WKV_PROMPT_SEQS = 8
WKV_SAMPLE_SEQS = 8
PAGES_PER_STEP = 32
FF_CHUNK = 256

BF16 = jnp.bfloat16
F32 = jnp.float32


def _dot(a, b):
    return jnp.dot(a, b, preferred_element_type=F32)


def _dot_nt(a, b):
    return lax.dot_general(a, b, (((1,), (1,)), ((), ())), preferred_element_type=F32)


def _dot_tn(a, b):
    return lax.dot_general(a, b, (((0,), (0,)), ((), ())), preferred_element_type=F32)


def _rms(x, g):
    return x * lax.rsqrt(jnp.mean(x * x, axis=-1, keepdims=True) + RMS_EPS) * g


def _softplus(z):
    return jnp.maximum(z, 0.0) + jnp.log(1.0 + jnp.exp(-jnp.abs(z)))


def _sigmoid(z):
    return 1.0 / (1.0 + jnp.exp(-z))


def _mask_bf16(cond):
    return cond.astype(F32).astype(BF16)


def _split3(c):
    hi = c.astype(BF16)
    r1 = c - hi.astype(F32)
    mid = r1.astype(BF16)
    lo = (r1 - mid.astype(F32)).astype(BF16)
    return hi, mid, lo


def _select_sum(sel, x):
    n = x.shape[1]
    out = _dot(sel.astype(BF16), jnp.concatenate(_split3(x), axis=1))
    return out[:, :n] + out[:, n:2 * n] + out[:, 2 * n:]


def _sum_select(x, sel):
    m = x.shape[0]
    out = _dot(jnp.concatenate(_split3(x), axis=0), sel.astype(BF16))
    return out[:m] + out[m:2 * m] + out[2 * m:]


def _inproj_outputs(xn, qkv, logf, c, p, p_prev, w, outs):
    (mu_ref, sel_ref, ones_ref, w2_ref, a2_ref, g2_ref, w0_ref, a0_ref) = w
    (q_ref, k_ref, kb_ref, v_ref, vb_ref, lf_ref, cq_ref, ck_ref,
     r_ref, lw_ref, kr_ref, vr_ref, a_ref, g_ref) = outs
    q_ref[0] = (qkv[:, :WIDTH] * (HEAD_DIM ** -0.5)).astype(BF16)
    k = qkv[:, WIDTH:2 * WIDTH]
    k_ref[0] = k
    kb_ref[0] = k.astype(BF16)
    v = qkv[:, 2 * WIDTH:]
    v_ref[0] = v
    vb_ref[0] = v.astype(BF16)
    lf_ref[0] = logf.T[:N_HEADS, :]
    cparts = jnp.concatenate(_split3(c), axis=1)
    cc = _dot(cparts, sel_ref[...]) + ones_ref[...]
    cq_ref[0] = cc[:, :LANES].astype(BF16)
    ck_ref[0] = cc[:, LANES:].astype(BF16)
    rw = p + (p_prev - p) * mu_ref[...]
    r_ref[0] = rw[:, :WIDTH]
    kr_ref[0] = rw[:, WIDTH:2 * WIDTH]
    vr_ref[0] = rw[:, 2 * WIDTH:3 * WIDTH]
    z = rw[:, 3 * WIDTH:3 * WIDTH + LANES]
    gl = rw[:, 3 * WIDTH + LANES:]
    w_log = -_softplus(-(w0_ref[...] + _dot(jnp.tanh(z).astype(BF16), w2_ref[...]))) - 0.5
    lw_ref[0] = -jnp.exp(w_log)
    a_ref[0] = _sigmoid(a0_ref[...] + _dot(z.astype(BF16), a2_ref[...]))
    g_ref[0] = _dot(_sigmoid(gl).astype(BF16), g2_ref[...])


def _inproj_long_kernel(x_ref, prow_ref, c0_ref, ln_ref, wqkv_ref, wf_ref, bf_ref, wrw_ref,
                        mu_ref, sel_ref, ones_ref, w2_ref, a2_ref, g2_ref, w0_ref, a0_ref,
                        q_ref, k_ref, kb_ref, v_ref, vb_ref, lf_ref, cq_ref, ck_ref,
                        r_ref, lw_ref, kr_ref, vr_ref, a_ref, g_ref, xl_ref, pl_ref, cl_ref,
                        pcar_ref, ccar_ref):
    @pl.when(pl.program_id(1) == 0)
    def _():
        pcar_ref[...] = prow_ref[0]
        ccar_ref[...] = c0_ref[0]

    rows = x_ref.shape[1]
    xn = _rms(x_ref[0], ln_ref[...])
    xb = xn.astype(BF16)
    qkv = _dot(xb, wqkv_ref[...])
    logf = -_softplus(-(_dot(xb, wf_ref[...]) + bf_ref[...]))
    ri = lax.broadcasted_iota(jnp.int32, (rows, rows), 0)
    ci = lax.broadcasted_iota(jnp.int32, (rows, rows), 1)
    c = _select_sum((ri >= ci).astype(F32), logf) + ccar_ref[...]
    ccar_ref[...] = c[rows - 1:rows, :]
    p = _dot(xb, wrw_ref[...])
    first = lax.broadcasted_iota(jnp.int32, p.shape, 0) == 0
    p_prev = jnp.where(first, pcar_ref[...], pltpu.roll(p, 1, 0))
    pcar_ref[...] = p[rows - 1:rows, :]
    xl_ref[0] = xn[rows - 1:rows, :]
    pl_ref[0] = p[rows - 1:rows, :]
    cl_ref[0] = c[rows - 1:rows, :]
    _inproj_outputs(xn, qkv, logf, c, p, p_prev,
                    (mu_ref, sel_ref, ones_ref, w2_ref, a2_ref, g2_ref, w0_ref, a0_ref),
                    (q_ref, k_ref, kb_ref, v_ref, vb_ref, lf_ref, cq_ref, ck_ref,
                     r_ref, lw_ref, kr_ref, vr_ref, a_ref, g_ref))


def _inproj_packed_kernel(x_ref, xprev_ref, ln_ref, wqkv_ref, wf_ref, bf_ref, wrw_ref,
                          mu_ref, sel_ref, ones_ref, w2_ref, a2_ref, g2_ref, w0_ref, a0_ref,
                          q_ref, k_ref, kb_ref, v_ref, vb_ref, lf_ref, cq_ref, ck_ref,
                          r_ref, lw_ref, kr_ref, vr_ref, a_ref, g_ref, xn_ref, *, seq_len):
    rows = x_ref.shape[1]
    shift = seq_len.bit_length() - 1
    xn = _rms(x_ref[0], ln_ref[...])
    xn_ref[0] = xn
    xb = xn.astype(BF16)
    qkv = _dot(xb, wqkv_ref[...])
    logf = -_softplus(-(_dot(xb, wf_ref[...]) + bf_ref[...]))
    ri = lax.broadcasted_iota(jnp.int32, (rows, rows), 0)
    ci = lax.broadcasted_iota(jnp.int32, (rows, rows), 1)
    same_seq = (ri >> shift) == (ci >> shift)
    c = _select_sum(((ri >= ci) & same_seq).astype(F32), logf)
    p = _dot(xb, wrw_ref[...])
    p_first = _dot(xprev_ref[0].astype(BF16), wrw_ref[...])
    first = (lax.broadcasted_iota(jnp.int32, p.shape, 0) & (seq_len - 1)) == 0
    p_prev = jnp.where(first, p_first, pltpu.roll(p, 1, 0))
    _inproj_outputs(xn, qkv, logf, c, p, p_prev,
                    (mu_ref, sel_ref, ones_ref, w2_ref, a2_ref, g2_ref, w0_ref, a0_ref),
                    (q_ref, k_ref, kb_ref, v_ref, vb_ref, lf_ref, cq_ref, ck_ref,
                     r_ref, lw_ref, kr_ref, vr_ref, a_ref, g_ref))


def _const_spec(arr):
    return pl.BlockSpec(arr.shape, lambda *_: (0,) * arr.ndim)


def _inproj_out_shapes(batch, rows_total):
    def s(width, dtype):
        return jax.ShapeDtypeStruct((batch, rows_total, width), dtype)
    logf_t = jax.ShapeDtypeStruct((batch, N_HEADS, rows_total), F32)
    return [s(WIDTH, BF16), s(WIDTH, F32), s(WIDTH, BF16), s(WIDTH, F32), s(WIDTH, BF16),
            logf_t, s(LANES, BF16), s(LANES, BF16)] + [s(WIDTH, F32)] * 6


def _inproj_out_specs(tile):
    def s(width):
        return pl.BlockSpec((1, tile, width), lambda b, i: (b, i, 0))
    logf_t = pl.BlockSpec((1, N_HEADS, tile), lambda b, i: (b, 0, i))
    return [s(WIDTH)] * 5 + [logf_t, s(LANES), s(LANES)] + [s(WIDTH)] * 6


def _inproj_long(x, prow, c0, weights, tile):
    batch, seq, _ = x.shape
    n_tiles = seq // tile
    row = lambda width: pl.BlockSpec((1, 1, width), lambda b, i: (b, 0, 0))
    shapes = _inproj_out_shapes(batch, seq) + [
        jax.ShapeDtypeStruct((batch, 1, D_MODEL), F32),
        jax.ShapeDtypeStruct((batch, 1, RWKV_COLS), F32),
        jax.ShapeDtypeStruct((batch, 1, LANES), F32)]
    specs = _inproj_out_specs(tile) + [row(D_MODEL), row(RWKV_COLS), row(LANES)]
    return pl.pallas_call(
        _inproj_long_kernel,
        grid=(batch, n_tiles),
        in_specs=[pl.BlockSpec((1, tile, D_MODEL), lambda b, i: (b, i, 0)),
                  _const_spec(prow), _const_spec(c0)] + [_const_spec(w) for w in weights],
        out_specs=specs,
        out_shape=shapes,
        scratch_shapes=[pltpu.VMEM((1, RWKV_COLS), F32), pltpu.VMEM((1, LANES), F32)],
        compiler_params=pltpu.CompilerParams(
            dimension_semantics=("arbitrary", "arbitrary"), vmem_limit_bytes=VMEM_LIMIT),
        name="inproj_long",
    )(x, prow, c0, *weights)


def _inproj_packed(x, xprev, weights, seq_len):
    _, rows, _ = x.shape
    full = pl.BlockSpec((1, rows, D_MODEL), lambda b, i: (0, 0, 0))
    return pl.pallas_call(
        functools.partial(_inproj_packed_kernel, seq_len=seq_len),
        grid=(1, 1),
        in_specs=[full, full] + [_const_spec(w) for w in weights],
        out_specs=_inproj_out_specs(rows) + [full],
        out_shape=_inproj_out_shapes(1, rows) + [jax.ShapeDtypeStruct((1, rows, D_MODEL), F32)],
        compiler_params=pltpu.CompilerParams(
            dimension_semantics=("arbitrary", "arbitrary"), vmem_limit_bytes=VMEM_LIMIT),
        name="inproj_packed",
    )(x, xprev, *weights)


def _fox_prompt_kernel(q_ref, cq_ref, km_ref, ckm_ref, vm_ref, k_ref, ck_ref, v_ref, o_ref):
    pair = pl.program_id(1)
    qi = pl.program_id(2)
    tile = q_ref.shape[1]
    lane = lax.broadcasted_iota(jnp.int32, (1, LANES), 1)
    q2 = q_ref[0]
    cq = cq_ref[0]
    n_sub = tile // ATT_SUB
    lhs = []
    for hh in range(2):
        head_lanes = (lane >= hh * HEAD_DIM) & (lane < (hh + 1) * HEAD_DIM)
        g0 = (2 * pair + hh) * BIAS_GROUP
        group_lanes = (lane >= g0) & (lane < g0 + BIAS_GROUP)
        full = jnp.concatenate([q2 * _mask_bf16(head_lanes), cq * _mask_bf16(group_lanes)], axis=1)
        lhs.append([full[r * ATT_SUB:(r + 1) * ATT_SUB, :] for r in range(n_sub)])
    query_minus_key = (lax.broadcasted_iota(jnp.int32, (ATT_K_TILE, ATT_SUB), 1)
                       - lax.broadcasted_iota(jnp.int32, (ATT_K_TILE, ATT_SUB), 0))

    def block(carry, kk, vv, first_col):
        out = list(carry)
        live = [(hh, r) for hh in range(2) for r in range(n_sub)
                if first_col is None or first_col <= (r + 1) * ATT_SUB - 1]
        for g0 in range(0, len(live), ATT_GROUP):
            group = live[g0:g0 + ATT_GROUP]
            scores, m_news = [], []
            for hh, r in group:
                s = _dot_nt(kk, lhs[hh][r])
                if first_col is not None and first_col + kk.shape[0] - 1 > r * ATT_SUB:
                    s = jnp.where(query_minus_key >= first_col - r * ATT_SUB, s, NEG_BIG)
                scores.append(s)
            for (hh, r), s in zip(group, scores):
                m_news.append(jnp.maximum(out[hh * n_sub + r][0], jnp.max(s, axis=0, keepdims=True)))
            for (hh, r), s, m_new in zip(group, scores, m_news):
                m, l, acc = out[hh * n_sub + r]
                alpha = jnp.exp(m - m_new)
                pe = jnp.exp(s - m_new)
                l_new = alpha * l + jnp.sum(pe, axis=0, keepdims=True)
                acc_new = alpha * acc + _dot_tn(vv, pe.astype(BF16))
                out[hh * n_sub + r] = (m_new, l_new, acc_new)
        return tuple(out)

    init = tuple((jnp.full((1, ATT_SUB), NEG_BIG, F32), jnp.zeros((1, ATT_SUB), F32),
                  jnp.zeros((LANES, ATT_SUB), F32)) for _ in range(2 * n_sub))
    carry = block(init, jnp.concatenate([km_ref[0], ckm_ref[0]], axis=1), vm_ref[0], None)

    def keys(j):
        start = pl.multiple_of(j * ATT_K_TILE, ATT_K_TILE)
        rows = pl.ds(start, ATT_K_TILE)
        return jnp.concatenate([k_ref[0, rows, :], ck_ref[0, rows, :]], axis=1), v_ref[0, rows, :]

    per_q = tile // ATT_K_TILE
    carry = lax.fori_loop(0, qi * per_q, lambda j, c: block(c, *keys(j), None), carry)
    for d in range(per_q):
        carry = block(carry, *keys(qi * per_q + d), d * ATT_K_TILE)
    first_head = lax.broadcasted_iota(jnp.int32, (LANES, ATT_SUB), 0) < HEAD_DIM
    for r in range(n_sub):
        (_, l0, acc0), (_, l1, acc1) = carry[r], carry[n_sub + r]
        o_t = jnp.where(first_head, acc0 / l0, acc1 / l1)
        o_ref[0, r * ATT_SUB:(r + 1) * ATT_SUB, :] = o_t.T.astype(BF16)


def _fox_prompt(q, cq, k_meta, ck_meta, v_meta, k, ck, v):
    batch, seq, _ = q.shape
    n_pairs = N_HEADS // 2
    return pl.pallas_call(
        _fox_prompt_kernel,
        grid=(batch, n_pairs, seq // ATT_TILE),
        in_specs=[
            pl.BlockSpec((1, ATT_TILE, LANES), lambda b, p, i: (b, i, p)),
            pl.BlockSpec((1, ATT_TILE, LANES), lambda b, p, i: (b, i, 0)),
            pl.BlockSpec((1, N_META, LANES), lambda b, p, i: (0, 0, p)),
            pl.BlockSpec((1, N_META, LANES), lambda b, p, i: (0, 0, 0)),
            pl.BlockSpec((1, N_META, LANES), lambda b, p, i: (0, 0, p)),
            pl.BlockSpec((1, seq, LANES), lambda b, p, i: (b, 0, p)),
            pl.BlockSpec((1, seq, LANES), lambda b, p, i: (b, 0, 0)),
            pl.BlockSpec((1, seq, LANES), lambda b, p, i: (b, 0, p)),
        ],
        out_specs=pl.BlockSpec((1, ATT_TILE, LANES), lambda b, p, i: (b, i, p)),
        out_shape=jax.ShapeDtypeStruct((batch, seq, WIDTH), BF16),
        compiler_params=pltpu.CompilerParams(
            dimension_semantics=("arbitrary", "arbitrary", "arbitrary"),
            vmem_limit_bytes=VMEM_LIMIT),
        name="fox_prompt",
    )(q, cq, k_meta, ck_meta, v_meta, k, ck, v)


def _fox_sample_kernel(pt_ref, q_ref, cq_ref, kn_ref, ckn_ref, vn_ref, *rest, n_new):
    del pt_ref
    g_pages = PAGES_PER_STEP
    k_refs, v_refs, lf_refs = rest[:g_pages], rest[g_pages:2 * g_pages], rest[2 * g_pages:3 * g_pages]
    o_ref, q_sc, cqm_sc, roff_sc, carry_sc, m_sc, l_sc, acc_sc = rest[3 * g_pages:]
    step = pl.program_id(1)
    rows = N_HEADS * n_new
    row_head = lax.broadcasted_iota(jnp.int32, (rows, 1), 0) & (N_HEADS - 1)

    def per_head_rows(x):
        return jnp.concatenate([jnp.broadcast_to(x[i:i + 1, :], (N_HEADS, x.shape[1]))
                                for i in range(n_new)], axis=0)

    @pl.when(step == 0)
    def _():
        q = per_head_rows(q_ref[0].astype(F32))
        lane_head = lax.broadcasted_iota(jnp.int32, (1, WIDTH), 1) >> HEAD_SHIFT
        q_sc[...] = jnp.where(row_head == lane_head, q, 0.0).astype(BF16)
        cq = per_head_rows(cq_ref[0].astype(F32))
        lane = lax.broadcasted_iota(jnp.int32, (1, LANES), 1)
        cqm = jnp.where((lane >> GROUP_SHIFT) == row_head, cq, 0.0)
        cqm_sc[...] = cqm.astype(BF16)
        slot = lax.broadcasted_iota(jnp.int32, cqm.shape, 1) & (BIAS_GROUP - 1)
        roff_sc[...] = jnp.sum(jnp.where(slot < BIAS_PARTS, cqm, 0.0), axis=-1, keepdims=True)
        carry_sc[...] = jnp.zeros(carry_sc.shape, F32)
        m_sc[...] = jnp.full(m_sc.shape, NEG_BIG, F32)
        l_sc[...] = jnp.zeros(l_sc.shape, F32)
        acc_sc[...] = jnp.zeros(acc_sc.shape, F32)

    q_bd = q_sc[...]
    ti = lax.broadcasted_iota(jnp.int32, (PAGE_SIZE, PAGE_SIZE), 0)
    tj = lax.broadcasted_iota(jnp.int32, (PAGE_SIZE, PAGE_SIZE), 1)
    later_and_all = jnp.concatenate([(ti > tj).astype(F32), jnp.ones((PAGE_SIZE, PAGE_SIZE), F32)],
                                    axis=1)
    lf_all = jnp.concatenate([lf_refs[g][0] for g in range(g_pages)], axis=0)
    sums = _sum_select(lf_all, later_and_all)

    carry = carry_sc[...]
    roff = roff_sc[...]
    def pages_t(refs, g0):
        return jnp.concatenate([refs[g][0].reshape(WIDTH, PAGE_SIZE) for g in (g0, g0 + 1)],
                               axis=1).astype(BF16)

    scores = []
    for g0 in range(0, g_pages, 2):
        bias = []
        for g in (g0, g0 + 1):
            page_sums = sums[g * N_HEADS:(g + 1) * N_HEADS, :]
            bias.append(jnp.concatenate([page_sums[:, :PAGE_SIZE] + carry] * n_new, axis=0))
            carry = carry + page_sums[:, PAGE_SIZE:]
        scores.append(_dot(q_bd, pages_t(k_refs, g0)) + jnp.concatenate(bias, axis=1) + roff)
    carry_sc[...] = carry

    m = m_sc[...]
    m_new = m
    for s in scores:
        m_new = jnp.maximum(m_new, jnp.max(s, axis=-1, keepdims=True))
    alpha = jnp.exp(m - m_new)
    l_new = alpha * l_sc[...]
    acc = alpha * acc_sc[...]
    for i, s in enumerate(scores):
        pe = jnp.exp(s - m_new)
        l_new = l_new + jnp.sum(pe, axis=-1, keepdims=True)
        acc = acc + _dot_nt(pe.astype(BF16), pages_t(v_refs, 2 * i))
    m_sc[...] = m_new
    l_sc[...] = l_new
    acc_sc[...] = acc

    @pl.when(step == pl.num_programs(1) - 1)
    def _():
        s = _dot_nt(q_bd, kn_ref[0]) + _dot_nt(cqm_sc[...], ckn_ref[0])
        row_tok = lax.broadcasted_iota(jnp.int32, (rows, 1), 0) >> (N_HEADS.bit_length() - 1)
        s = jnp.where(lax.broadcasted_iota(jnp.int32, (1, n_new), 1) <= row_tok, s, NEG_BIG)
        m_fin = jnp.maximum(m_new, jnp.max(s, axis=-1, keepdims=True))
        a_fin = jnp.exp(m_new - m_fin)
        pe = jnp.exp(s - m_fin)
        l_fin = a_fin * l_new + jnp.sum(pe, axis=-1, keepdims=True)
        out = (a_fin * acc + _dot(pe.astype(BF16), vn_ref[0])) / l_fin
        own = (lax.broadcasted_iota(jnp.int32, (N_HEADS, WIDTH), 0)
               == lax.broadcasted_iota(jnp.int32, (N_HEADS, WIDTH), 1) >> HEAD_SHIFT)
        o_ref[0] = jnp.concatenate(
            [jnp.sum(jnp.where(own, out[i * N_HEADS:(i + 1) * N_HEADS, :], 0.0), axis=0, keepdims=True)
             for i in range(n_new)], axis=0).astype(BF16)


def _fox_sample(page_table, q, cq, k_new, ck_new, v_new, cache_k, cache_v, cache_logf):
    n_seq, n_new, _ = q.shape
    n_pages = page_table.shape[1]
    g_pages = PAGES_PER_STEP
    rows = N_HEADS * n_new

    def per_seq(width):
        return pl.BlockSpec((1, n_new, width), lambda b, j, pt: (b, 0, 0))

    def page_spec(g, arr):
        block = (1,) + arr.shape[1:]
        return pl.BlockSpec(block, lambda b, j, pt: (pt[b * n_pages + j * g_pages + g],)
                            + (0,) * (arr.ndim - 1))

    visit_order = page_table[:, ::-1].reshape(-1)

    grid_spec = pltpu.PrefetchScalarGridSpec(
        num_scalar_prefetch=1,
        grid=(n_seq, n_pages // g_pages),
        in_specs=([per_seq(WIDTH), per_seq(LANES), per_seq(WIDTH), per_seq(LANES), per_seq(WIDTH)]
                  + [page_spec(g, cache_k) for g in range(g_pages)]
                  + [page_spec(g, cache_v) for g in range(g_pages)]
                  + [page_spec(g, cache_logf) for g in range(g_pages)]),
        out_specs=pl.BlockSpec((1, n_new, WIDTH), lambda b, j, pt: (b, 0, 0)),
        scratch_shapes=[pltpu.VMEM((rows, WIDTH), BF16), pltpu.VMEM((rows, LANES), BF16),
                        pltpu.VMEM((rows, 1), F32), pltpu.VMEM((N_HEADS, PAGE_SIZE), F32),
                        pltpu.VMEM((rows, 1), F32), pltpu.VMEM((rows, 1), F32),
                        pltpu.VMEM((rows, WIDTH), F32)],
    )
    return pl.pallas_call(
        functools.partial(_fox_sample_kernel, n_new=n_new),
        grid_spec=grid_spec,
        out_shape=jax.ShapeDtypeStruct((n_seq, n_new, WIDTH), BF16),
        compiler_params=pltpu.CompilerParams(
            dimension_semantics=("arbitrary", "arbitrary"), vmem_limit_bytes=VMEM_LIMIT),
        name="fox_sample",
    )(visit_order, q, cq, k_new, ck_new, v_new,
      *([cache_k] * g_pages), *([cache_v] * g_pages), *([cache_logf] * g_pages))


def _wkv_kernel(r_ref, lw_ref, kr_ref, vr_ref, a_ref, g_ref, s0_ref,
                kk_ref, ka_ref, rk_ref, gg_ref, gb_ref, o_ref, so_ref, s_sc):
    nb, C, _ = r_ref.shape
    n_pairs = N_HEADS // 2
    units = [(bb, pair) for bb in range(nb) for pair in range(n_pairs)]
    lo_state = ((lax.broadcasted_iota(jnp.int32, (LANES, LANES), 0) < HEAD_DIM)
                == (lax.broadcasted_iota(jnp.int32, (LANES, LANES), 1) < HEAD_DIM))

    @pl.when(pl.program_id(1) == 0)
    def _():
        zeros = jnp.zeros((HEAD_DIM, HEAD_DIM), F32)
        for bb, pair in units:
            sb = bb if s0_ref.shape[0] == nb else 0
            top = jnp.concatenate([s0_ref[sb, 2 * pair], zeros], axis=1)
            bottom = jnp.concatenate([zeros, s0_ref[sb, 2 * pair + 1]], axis=1)
            s_sc[bb, pair] = jnp.concatenate([top, bottom], axis=0)

    ri = lax.broadcasted_iota(jnp.int32, (C, C), 0)
    ci = lax.broadcasted_iota(jnp.int32, (C, C), 1)
    incl = ri >= ci
    eye = (ri == ci).astype(F32)
    lo = lax.broadcasted_iota(jnp.int32, (C, LANES), 1) < HEAD_DIM
    lo2 = lax.broadcasted_iota(jnp.int32, (2 * C, LANES), 1) < HEAD_DIM
    levels = max(1, (C - 1).bit_length())
    lcum = [_select_sum(incl.astype(F32), lw_ref[bb]) for bb in range(nb)]

    def per_head(x):
        first = jnp.sum(jnp.where(lo, x, 0.0), axis=-1, keepdims=True)
        second = jnp.sum(jnp.where(lo, 0.0, x), axis=-1, keepdims=True)
        return jnp.where(lo, first, second)

    xs, x_all, btkts, bgs, kgs, vfs, vbs, vvs, kps, e_ends = [], [], [], [], [], [], [], [], [], []
    for bb, pair in units:
        lanes = slice(pair * LANES, (pair + 1) * LANES)
        r2, lw2, kr2, a2 = (ref[bb, :, lanes] for ref in (r_ref, lw_ref, kr_ref, a_ref))
        lc = lcum[bb][:, lanes]
        kkr = kr2 * kk_ref[:, lanes]
        kk = kkr / jnp.maximum(jnp.sqrt(per_head(kkr * kkr)), 1e-12)
        kp = kr2 * (1.0 + (a2 - 1.0) * ka_ref[:, lanes])
        bv = kk * a2
        e_neg = jnp.exp(-lc)
        l_end = lc[C - 1:C, :]
        e_rem = jnp.exp(l_end - lc)
        x2 = jnp.concatenate([-kk * jnp.exp(lc - lw2), r2 * jnp.exp(lc)], axis=0)
        x_all.append(x2.astype(BF16))
        xs.append((jnp.where(lo2, x2, 0.0).astype(BF16), jnp.where(lo2, 0.0, x2).astype(BF16)))
        btkts.append(jnp.concatenate([bv * e_neg, kp * e_neg], axis=0).astype(BF16))
        bgs.append((bv * e_rem).astype(BF16))
        kgs.append((kp * e_rem).astype(BF16))
        v2 = vr_ref[bb, :, lanes]
        vfs.append(v2)
        vbs.append(v2.astype(BF16))
        vvs.append(jnp.concatenate([v2, v2], axis=0).astype(BF16))
        kps.append(kp)
        e_ends.append(jnp.exp(l_end))

    chains = [(u, hh) for u in range(len(units)) for hh in range(2)]
    sc = [_dot_nt(xs[u][hh], btkts[u]) for u, hh in chains]
    n_ch = range(len(chains))
    row2 = lax.broadcasted_iota(jnp.int32, (C, 2 * C), 0)
    col2 = lax.broadcasted_iota(jnp.int32, (C, 2 * C), 1)
    strict2 = row2 > (col2 & (C - 1))
    incl2 = row2 >= (col2 & (C - 1))
    top = [jnp.where(strict2, sc[i][:C], 0.0) for i in n_ch]
    a_ab = [top[i][:, :C] for i in n_ch]
    a_0k = [jnp.where(col2 < C, 0.0, top[i]).astype(BF16) for i in n_ch]
    a_rbk = [jnp.where(incl2, sc[i][C:], 0.0).astype(BF16) for i in n_ch]

    tinv = [eye + a_ab[i] for i in n_ch]
    if levels > 1:
        lb = [a_ab[i].astype(BF16) for i in n_ch]
        lpow = [_dot(lb[i], lb[i]) for i in n_ch]
        for level in range(1, levels):
            qb = [lpow[i].astype(BF16) for i in n_ch]
            if level == levels - 1:
                tinv = [tinv[i] + _dot(tinv[i].astype(BF16), qb[i]) for i in n_ch]
            else:
                both = [_dot(jnp.concatenate([tinv[i], lpow[i]], axis=0).astype(BF16), qb[i])
                        for i in n_ch]
                tinv = [tinv[i] + both[i][:C] for i in n_ch]
                lpow = [both[i][C:] for i in n_ch]
    tb = [tinv[i].astype(BF16) for i in n_ch]

    def both_heads(u, f):
        return jnp.where(lo, f(2 * u), f(2 * u + 1))

    n_u = range(len(units))
    s_old = [s_sc[bb, pair] for bb, pair in units]
    ps = [_dot_nt(x_all[u], s_old[u].astype(BF16)) for u in n_u]
    w1 = [(ps[u][:C] + both_heads(u, lambda i, u=u: _dot(a_0k[i], vvs[u]))).astype(BF16) for u in n_u]
    uf = [both_heads(u, lambda i, u=u: _dot(tb[i], w1[u])) for u in n_u]
    ub = [uf[u].astype(BF16) for u in n_u]
    uv = [jnp.concatenate([uf[u], vfs[u]], axis=0).astype(BF16) for u in n_u]
    ys = [ps[u][C:] + both_heads(u, lambda i, u=u: _dot(a_rbk[i], uv[u])) for u in n_u]
    for u, (bb, pair) in enumerate(units):
        cross = _dot_tn(ub[u], bgs[u]) + _dot_tn(vbs[u], kgs[u])
        s_sc[bb, pair] = s_old[u] * e_ends[u] + jnp.where(lo_state, cross, 0.0)

    for u, (bb, pair) in enumerate(units):
        lanes = slice(pair * LANES, (pair + 1) * LANES)
        y = ys[u]
        yc = y - per_head(y) * (1.0 / HEAD_DIM)
        var = per_head(yc * yc) * (1.0 / HEAD_DIM)
        yn = yc * lax.rsqrt(var + GN_EPS) * gg_ref[:, lanes] + gb_ref[:, lanes]
        bonus = per_head(r_ref[bb, :, lanes] * kps[u] * rk_ref[:, lanes]) * vr_ref[bb, :, lanes]
        o_ref[bb, :, lanes] = ((yn + bonus) * g_ref[bb, :, lanes]).astype(BF16)

    @pl.when(pl.program_id(1) == pl.num_programs(1) - 1)
    def _():
        for bb, pair in units:
            s2 = s_sc[bb, pair]
            so_ref[bb, 2 * pair] = s2[:HEAD_DIM, :HEAD_DIM]
            so_ref[bb, 2 * pair + 1] = pltpu.roll(s2[HEAD_DIM:, :], HEAD_DIM, 1)[:, :HEAD_DIM]


def _wkv(r, lw, kr, vr, a, g, s0, params, chunk, batch_tile):
    batch, seq, _ = r.shape
    assert chunk & (chunk - 1) == 0 and seq % chunk == 0 and batch % batch_tile == 0
    shared_state = s0.shape[0] == 1
    tok = pl.BlockSpec((batch_tile, chunk, WIDTH), lambda b, c: (b, c, 0))
    state_out = pl.BlockSpec((batch_tile, N_HEADS, HEAD_DIM, HEAD_DIM), lambda b, c: (b, 0, 0, 0))
    state_in = (pl.BlockSpec((1, N_HEADS, HEAD_DIM, HEAD_DIM), lambda b, c: (0, 0, 0, 0))
                if shared_state else state_out)
    return pl.pallas_call(
        _wkv_kernel,
        grid=(batch // batch_tile, seq // chunk),
        in_specs=[tok] * 6 + [state_in] + [_const_spec(p) for p in params],
        out_specs=[tok, state_out],
        out_shape=[jax.ShapeDtypeStruct((batch, seq, WIDTH), BF16),
                   jax.ShapeDtypeStruct((batch, N_HEADS, HEAD_DIM, HEAD_DIM), F32)],
        scratch_shapes=[pltpu.VMEM((batch_tile, N_HEADS // 2, LANES, LANES), F32)],
        compiler_params=pltpu.CompilerParams(
            dimension_semantics=("arbitrary", "arbitrary"), vmem_limit_bytes=VMEM_LIMIT),
        name="wkv",
    )(r, lw, kr, vr, a, g, s0, *params)


def _out_ffn_kernel(x_ref, att_ref, rw_ref, woa_ref, wor_ref, ln2_ref, wg_ref, wu_ref, wd_ref,
                    lnf_ref, y_ref):
    h = x_ref[...] + _dot(att_ref[...], woa_ref[...]) + _dot(rw_ref[...], wor_ref[...])
    hb = _rms(h, ln2_ref[...]).astype(BF16)
    ffn = jnp.zeros(h.shape, F32)
    d_ff = wg_ref.shape[1]
    for c0 in range(0, d_ff, FF_CHUNK):
        gate = _dot(hb, wg_ref[:, c0:c0 + FF_CHUNK])
        up = _dot(hb, wu_ref[:, c0:c0 + FF_CHUNK])
        act = gate * _sigmoid(gate) * up
        ffn = ffn + _dot(act.astype(BF16), wd_ref[c0:c0 + FF_CHUNK, :])
    y_ref[...] = _rms(h + ffn, lnf_ref[...])


def _out_ffn(x, att, rw, weights):
    rows = x.shape[0]
    tile = min(FFN_ROW_TILE, rows)
    tok = lambda width: pl.BlockSpec((tile, width), lambda i: (i, 0))
    resident = lambda w: pl.BlockSpec(w.shape, lambda i: (0,) * w.ndim, pipeline_mode=pl.Buffered(1))
    return pl.pallas_call(
        _out_ffn_kernel,
        grid=(rows // tile,),
        in_specs=[tok(D_MODEL), tok(WIDTH), tok(WIDTH)] + [resident(w) for w in weights],
        out_specs=tok(D_MODEL),
        out_shape=jax.ShapeDtypeStruct((rows, D_MODEL), F32),
        compiler_params=pltpu.CompilerParams(
            dimension_semantics=("arbitrary",), vmem_limit_bytes=VMEM_LIMIT),
        name="out_ffn",
    )(x, att, rw, *weights)


def _bias_selectors():
    part = jnp.arange(BIAS_PARTS)[:, None, None]
    src = jnp.arange(LANES)[None, :, None]
    dst = jnp.arange(LANES)[None, None, :]
    is_head = src < N_HEADS
    q_sel = (is_head & (dst == src * BIAS_GROUP + part)).astype(F32)
    k_sel = -(is_head & (dst == src * BIAS_GROUP + BIAS_PARTS + part)).astype(F32)
    sel = jnp.concatenate([q_sel, k_sel], axis=2).reshape(BIAS_PARTS * LANES, 2 * LANES)
    slot = jnp.arange(LANES) % BIAS_GROUP
    ones = jnp.concatenate([(slot >= BIAS_PARTS) & (slot < 2 * BIAS_PARTS),
                            slot < BIAS_PARTS]).astype(F32)[None, :]
    return sel.astype(BF16), ones


def kernel(x_prompt, x_sample, cache_k, cache_v, cache_logf, state_wkv, state_shift, page_table,
           meta_tokens, ln1_g, w_in, b_f, mu_shift, w0, w2, a0, a2, g2, k_k, k_a, r_k,
           gn_g, gn_b, w_out, ln2_g, w_gate, w_up, w_down, lnf_g):
    batch, seq, _ = x_prompt.shape
    n_seq, n_new, _ = x_sample.shape
    fox_cols = 3 * WIDTH + N_HEADS

    w_in0 = w_in[0]
    wqkv = w_in0[:, :3 * WIDTH].astype(BF16)
    wf = jnp.pad(w_in0[:, 3 * WIDTH:fox_cols], ((0, 0), (0, LANES - N_HEADS))).astype(BF16)
    bf = jnp.pad(b_f[0], (0, LANES - N_HEADS))[None, :]
    wrw = w_in0[:, fox_cols:].astype(BF16)
    sel, ones = _bias_selectors()
    w2p = jnp.concatenate([w2[0], jnp.zeros((ICLR_LORA, WIDTH), F32)], axis=0).astype(BF16)
    a2p = jnp.concatenate([jnp.zeros((DECAY_LORA, WIDTH), F32), a2[0]], axis=0).astype(BF16)
    inproj_w = (ln1_g[0][None, :], wqkv, wf, bf, wrw, mu_shift[0][None, :], sel, ones,
                w2p, a2p, g2[0].astype(BF16), w0[0][None, :], a0[0][None, :])
    wkv_p = tuple(t[0].reshape(1, WIDTH) for t in (k_k, k_a, r_k, gn_g, gn_b))
    ffn_w = (w_out[0][:WIDTH].astype(BF16), w_out[0][WIDTH:].astype(BF16), ln2_g[0][None, :],
             w_gate[0].astype(BF16), w_up[0].astype(BF16), w_down[0].astype(BF16), lnf_g[None, :])

    zrow = jnp.zeros((1, 1, RWKV_COLS), F32)
    zc = jnp.zeros((1, 1, LANES), F32)
    mo = _inproj_long(meta_tokens[None], zrow, zc, inproj_w, N_META)
    (_, mk, mkb, mv, mvb, mlf, _, mck, mr, mlw, mkr, mvr, ma, mg, _, mprow, mclast) = mo
    zstate = jnp.zeros((1, N_HEADS, HEAD_DIM, HEAD_DIM), F32)
    _, s_meta = _wkv(mr, mlw, mkr, mvr, ma, mg, zstate, wkv_p, N_META, 1)

    po = _inproj_long(x_prompt, mprow, mclast, inproj_w, ROW_TILE)
    (pq, pk, pkb, pv, pvb, plf, pcq, pck, pr, plw, pkr, pvr, pa, pg, pxl, _, _) = po
    att_p = _fox_prompt(pq, pcq, mkb, mck, mvb, pkb, pck, pvb)
    rw_p, wkv_prompt = _wkv(pr, plw, pkr, pvr, pa, pg, s_meta, wkv_p, WKV_CHUNK, WKV_PROMPT_SEQS)
    y_prompt = _out_ffn(x_prompt.reshape(batch * seq, D_MODEL), att_p.reshape(batch * seq, WIDTH),
                        rw_p.reshape(batch * seq, WIDTH), ffn_w).reshape(batch, seq, D_MODEL)

    rows_s = n_seq * n_new
    xprev = jnp.repeat(state_shift[0], n_new, axis=0)[None]
    so = _inproj_packed(x_sample.reshape(1, rows_s, D_MODEL), xprev, inproj_w, n_new)
    (sq, sk, skb, sv, svb, slf, scq, sck, sr, slw, skr, svr, sa, sg, sxn) = so
    seqs = lambda t: t.reshape(n_seq, n_new, t.shape[-1])
    att_s = _fox_sample(page_table, seqs(sq), seqs(scq), seqs(skb), seqs(sck), seqs(svb),
                        jnp.transpose(cache_k[0], (0, 2, 3, 1)), jnp.transpose(cache_v[0], (0, 2, 3, 1)),
                        jnp.transpose(cache_logf[0], (0, 2, 1)))
    rw_s, wkv_sample = _wkv(seqs(sr), seqs(slw), seqs(skr), seqs(svr), seqs(sa), seqs(sg),
                            state_wkv[0], wkv_p, n_new, WKV_SAMPLE_SEQS)
    y_sample = _out_ffn(x_sample.reshape(rows_s, D_MODEL), att_s.reshape(rows_s, WIDTH),
                        rw_s.reshape(rows_s, WIDTH), ffn_w).reshape(n_seq, n_new, D_MODEL)

    def with_meta(meta, main):
        return jnp.concatenate([jnp.broadcast_to(meta, (batch,) + meta.shape[1:]), main], axis=1)

    heads = lambda t: t.reshape(t.shape[:-1] + (N_HEADS, HEAD_DIM))
    k_prompt = heads(with_meta(mk, pk))[None]
    v_prompt = heads(with_meta(mv, pv))[None]
    logf_prompt = jnp.concatenate([jnp.broadcast_to(mlf, (batch,) + mlf.shape[1:]), plf], axis=2)
    logf_prompt = jnp.transpose(logf_prompt, (0, 2, 1))[None]
    shift_prompt = pxl.reshape(1, batch, D_MODEL)
    k_sample = heads(seqs(sk))[None]
    v_sample = heads(seqs(sv))[None]
    logf_sample = jnp.transpose(slf.reshape(N_HEADS, n_seq, n_new), (1, 2, 0))[None]
    shift_sample = seqs(sxn)[:, -1][None]
    return (y_prompt, y_sample, k_prompt, v_prompt, logf_prompt, wkv_prompt[None], shift_prompt,
            k_sample, v_sample, logf_sample, wkv_sample[None], shift_sample)
```

```python
import functools

import jax
import jax.numpy as jnp
from jax import lax
from jax.experimental import pallas as pl
from jax.experimental.pallas import tpu as pltpu

D_MODEL = 1024
N_META = 16
HEAD_DIM = 64
N_HEADS = 8
WIDTH = N_HEADS * HEAD_DIM
PAGE_SIZE = 128
DECAY_LORA = 64
ICLR_LORA = 64
GATE_LORA = 128
RWKV_COLS = 3 * WIDTH + DECAY_LORA + ICLR_LORA + GATE_LORA
RMS_EPS = 1e-6
GN_EPS = 64e-5
NEG_BIG = -1e30

LANES = 128
BIAS_GROUP = 16
BIAS_PARTS = 3
HEAD_SHIFT = HEAD_DIM.bit_length() - 1
GROUP_SHIFT = BIAS_GROUP.bit_length() - 1
VMEM_LIMIT = 56 * 1024 * 1024

ROW_TILE = 512
FFN_ROW_TILE = 512
ATT_TILE = 2048
ATT_K_TILE = 512
ATT_SUB = 256
ATT_GROUP = 16
WKV_CHUNK = 64
WKV_PROMPT_SEQS = 8
WKV_SAMPLE_SEQS = 8
PAGES_PER_STEP = 32
FF_CHUNK = 256

BF16 = jnp.bfloat16
F32 = jnp.float32


def _dot(a, b):
    return jnp.dot(a, b, preferred_element_type=F32)


def _dot_nt(a, b):
    return lax.dot_general(a, b, (((1,), (1,)), ((), ())), preferred_element_type=F32)


def _dot_tn(a, b):
    return lax.dot_general(a, b, (((0,), (0,)), ((), ())), preferred_element_type=F32)


def _rms(x, g):
    return x * lax.rsqrt(jnp.mean(x * x, axis=-1, keepdims=True) + RMS_EPS) * g


def _softplus(z):
    return jnp.maximum(z, 0.0) + jnp.log(1.0 + jnp.exp(-jnp.abs(z)))


def _sigmoid(z):
    return 1.0 / (1.0 + jnp.exp(-z))


def _mask_bf16(cond):
    return cond.astype(F32).astype(BF16)


def _split3(c):
    hi = c.astype(BF16)
    r1 = c - hi.astype(F32)
    mid = r1.astype(BF16)
    lo = (r1 - mid.astype(F32)).astype(BF16)
    return hi, mid, lo


def _select_sum(sel, x):
    n = x.shape[1]
    out = _dot(sel.astype(BF16), jnp.concatenate(_split3(x), axis=1))
    return out[:, :n] + out[:, n:2 * n] + out[:, 2 * n:]


def _sum_select(x, sel):
    m = x.shape[0]
    out = _dot(jnp.concatenate(_split3(x), axis=0), sel.astype(BF16))
    return out[:m] + out[m:2 * m] + out[2 * m:]


def _inproj_outputs(xn, qkv, logf, c, p, p_prev, w, outs):
    (mu_ref, sel_ref, ones_ref, w2_ref, a2_ref, g2_ref, w0_ref, a0_ref) = w
    (q_ref, k_ref, kb_ref, v_ref, vb_ref, lf_ref, cq_ref, ck_ref,
     r_ref, lw_ref, kr_ref, vr_ref, a_ref, g_ref) = outs
    q_ref[0] = (qkv[:, :WIDTH] * (HEAD_DIM ** -0.5)).astype(BF16)
    k = qkv[:, WIDTH:2 * WIDTH]
    k_ref[0] = k
    kb_ref[0] = k.astype(BF16)
    v = qkv[:, 2 * WIDTH:]
    v_ref[0] = v
    vb_ref[0] = v.astype(BF16)
    lf_ref[0] = logf.T[:N_HEADS, :]
    cparts = jnp.concatenate(_split3(c), axis=1)
    cc = _dot(cparts, sel_ref[...]) + ones_ref[...]
    cq_ref[0] = cc[:, :LANES].astype(BF16)
    ck_ref[0] = cc[:, LANES:].astype(BF16)
    rw = p + (p_prev - p) * mu_ref[...]
    r_ref[0] = rw[:, :WIDTH]
    kr_ref[0] = rw[:, WIDTH:2 * WIDTH]
    vr_ref[0] = rw[:, 2 * WIDTH:3 * WIDTH]
    z = rw[:, 3 * WIDTH:3 * WIDTH + LANES]
    gl = rw[:, 3 * WIDTH + LANES:]
    w_log = -_softplus(-(w0_ref[...] + _dot(jnp.tanh(z).astype(BF16), w2_ref[...]))) - 0.5
    lw_ref[0] = -jnp.exp(w_log)
    a_ref[0] = _sigmoid(a0_ref[...] + _dot(z.astype(BF16), a2_ref[...]))
    g_ref[0] = _dot(_sigmoid(gl).astype(BF16), g2_ref[...])


def _inproj_long_kernel(x_ref, prow_ref, c0_ref, ln_ref, wqkv_ref, wf_ref, bf_ref, wrw_ref,
                        mu_ref, sel_ref, ones_ref, w2_ref, a2_ref, g2_ref, w0_ref, a0_ref,
                        q_ref, k_ref, kb_ref, v_ref, vb_ref, lf_ref, cq_ref, ck_ref,
                        r_ref, lw_ref, kr_ref, vr_ref, a_ref, g_ref, xl_ref, pl_ref, cl_ref,
                        pcar_ref, ccar_ref):
    @pl.when(pl.program_id(1) == 0)
    def _():
        pcar_ref[...] = prow_ref[0]
        ccar_ref[...] = c0_ref[0]

    rows = x_ref.shape[1]
    xn = _rms(x_ref[0], ln_ref[...])
    xb = xn.astype(BF16)
    qkv = _dot(xb, wqkv_ref[...])
    logf = -_softplus(-(_dot(xb, wf_ref[...]) + bf_ref[...]))
    ri = lax.broadcasted_iota(jnp.int32, (rows, rows), 0)
    ci = lax.broadcasted_iota(jnp.int32, (rows, rows), 1)
    c = _select_sum((ri >= ci).astype(F32), logf) + ccar_ref[...]
    ccar_ref[...] = c[rows - 1:rows, :]
    p = _dot(xb, wrw_ref[...])
    first = lax.broadcasted_iota(jnp.int32, p.shape, 0) == 0
    p_prev = jnp.where(first, pcar_ref[...], pltpu.roll(p, 1, 0))
    pcar_ref[...] = p[rows - 1:rows, :]
    xl_ref[0] = xn[rows - 1:rows, :]
    pl_ref[0] = p[rows - 1:rows, :]
    cl_ref[0] = c[rows - 1:rows, :]
    _inproj_outputs(xn, qkv, logf, c, p, p_prev,
                    (mu_ref, sel_ref, ones_ref, w2_ref, a2_ref, g2_ref, w0_ref, a0_ref),
                    (q_ref, k_ref, kb_ref, v_ref, vb_ref, lf_ref, cq_ref, ck_ref,
                     r_ref, lw_ref, kr_ref, vr_ref, a_ref, g_ref))


def _inproj_packed_kernel(x_ref, xprev_ref, ln_ref, wqkv_ref, wf_ref, bf_ref, wrw_ref,
                          mu_ref, sel_ref, ones_ref, w2_ref, a2_ref, g2_ref, w0_ref, a0_ref,
                          q_ref, k_ref, kb_ref, v_ref, vb_ref, lf_ref, cq_ref, ck_ref,
                          r_ref, lw_ref, kr_ref, vr_ref, a_ref, g_ref, xn_ref, *, seq_len):
    rows = x_ref.shape[1]
    shift = seq_len.bit_length() - 1
    xn = _rms(x_ref[0], ln_ref[...])
    xn_ref[0] = xn
    xb = xn.astype(BF16)
    qkv = _dot(xb, wqkv_ref[...])
    logf = -_softplus(-(_dot(xb, wf_ref[...]) + bf_ref[...]))
    ri = lax.broadcasted_iota(jnp.int32, (rows, rows), 0)
    ci = lax.broadcasted_iota(jnp.int32, (rows, rows), 1)
    same_seq = (ri >> shift) == (ci >> shift)
    c = _select_sum(((ri >= ci) & same_seq).astype(F32), logf)
    p = _dot(xb, wrw_ref[...])
    p_first = _dot(xprev_ref[0].astype(BF16), wrw_ref[...])
    first = (lax.broadcasted_iota(jnp.int32, p.shape, 0) & (seq_len - 1)) == 0
    p_prev = jnp.where(first, p_first, pltpu.roll(p, 1, 0))
    _inproj_outputs(xn, qkv, logf, c, p, p_prev,
                    (mu_ref, sel_ref, ones_ref, w2_ref, a2_ref, g2_ref, w0_ref, a0_ref),
                    (q_ref, k_ref, kb_ref, v_ref, vb_ref, lf_ref, cq_ref, ck_ref,
                     r_ref, lw_ref, kr_ref, vr_ref, a_ref, g_ref))


def _const_spec(arr):
    return pl.BlockSpec(arr.shape, lambda *_: (0,) * arr.ndim)


def _inproj_out_shapes(batch, rows_total):
    def s(width, dtype):
        return jax.ShapeDtypeStruct((batch, rows_total, width), dtype)
    logf_t = jax.ShapeDtypeStruct((batch, N_HEADS, rows_total), F32)
    return [s(WIDTH, BF16), s(WIDTH, F32), s(WIDTH, BF16), s(WIDTH, F32), s(WIDTH, BF16),
            logf_t, s(LANES, BF16), s(LANES, BF16)] + [s(WIDTH, F32)] * 6


def _inproj_out_specs(tile):
    def s(width):
        return pl.BlockSpec((1, tile, width), lambda b, i: (b, i, 0))
    logf_t = pl.BlockSpec((1, N_HEADS, tile), lambda b, i: (b, 0, i))
    return [s(WIDTH)] * 5 + [logf_t, s(LANES), s(LANES)] + [s(WIDTH)] * 6


def _inproj_long(x, prow, c0, weights, tile):
    batch, seq, _ = x.shape
    n_tiles = seq // tile
    row = lambda width: pl.BlockSpec((1, 1, width), lambda b, i: (b, 0, 0))
    shapes = _inproj_out_shapes(batch, seq) + [
        jax.ShapeDtypeStruct((batch, 1, D_MODEL), F32),
        jax.ShapeDtypeStruct((batch, 1, RWKV_COLS), F32),
        jax.ShapeDtypeStruct((batch, 1, LANES), F32)]
    specs = _inproj_out_specs(tile) + [row(D_MODEL), row(RWKV_COLS), row(LANES)]
    return pl.pallas_call(
        _inproj_long_kernel,
        grid=(batch, n_tiles),
        in_specs=[pl.BlockSpec((1, tile, D_MODEL), lambda b, i: (b, i, 0)),
                  _const_spec(prow), _const_spec(c0)] + [_const_spec(w) for w in weights],
        out_specs=specs,
        out_shape=shapes,
        scratch_shapes=[pltpu.VMEM((1, RWKV_COLS), F32), pltpu.VMEM((1, LANES), F32)],
        compiler_params=pltpu.CompilerParams(
            dimension_semantics=("arbitrary", "arbitrary"), vmem_limit_bytes=VMEM_LIMIT),
        name="inproj_long",
    )(x, prow, c0, *weights)


def _inproj_packed(x, xprev, weights, seq_len):
    _, rows, _ = x.shape
    full = pl.BlockSpec((1, rows, D_MODEL), lambda b, i: (0, 0, 0))
    return pl.pallas_call(
        functools.partial(_inproj_packed_kernel, seq_len=seq_len),
        grid=(1, 1),
        in_specs=[full, full] + [_const_spec(w) for w in weights],
        out_specs=_inproj_out_specs(rows) + [full],
        out_shape=_inproj_out_shapes(1, rows) + [jax.ShapeDtypeStruct((1, rows, D_MODEL), F32)],
        compiler_params=pltpu.CompilerParams(
            dimension_semantics=("arbitrary", "arbitrary"), vmem_limit_bytes=VMEM_LIMIT),
        name="inproj_packed",
    )(x, xprev, *weights)


def _fox_prompt_kernel(q_ref, cq_ref, km_ref, ckm_ref, vm_ref, k_ref, ck_ref, v_ref, o_ref):
    pair = pl.program_id(1)
    qi = pl.program_id(2)
    tile = q_ref.shape[1]
    lane = lax.broadcasted_iota(jnp.int32, (1, LANES), 1)
    q2 = q_ref[0]
    cq = cq_ref[0]
    n_sub = tile // ATT_SUB
    lhs = []
    for hh in range(2):
        head_lanes = (lane >= hh * HEAD_DIM) & (lane < (hh + 1) * HEAD_DIM)
        g0 = (2 * pair + hh) * BIAS_GROUP
        group_lanes = (lane >= g0) & (lane < g0 + BIAS_GROUP)
        full = jnp.concatenate([q2 * _mask_bf16(head_lanes), cq * _mask_bf16(group_lanes)], axis=1)
        lhs.append([full[r * ATT_SUB:(r + 1) * ATT_SUB, :] for r in range(n_sub)])
    query_minus_key = (lax.broadcasted_iota(jnp.int32, (ATT_K_TILE, ATT_SUB), 1)
                       - lax.broadcasted_iota(jnp.int32, (ATT_K_TILE, ATT_SUB), 0))

    def block(carry, kk, vv, first_col):
        out = list(carry)
        live = [(hh, r) for hh in range(2) for r in range(n_sub)
                if first_col is None or first_col <= (r + 1) * ATT_SUB - 1]
        for g0 in range(0, len(live), ATT_GROUP):
            group = live[g0:g0 + ATT_GROUP]
            scores, m_news = [], []
            for hh, r in group:
                s = _dot_nt(kk, lhs[hh][r])
                if first_col is not None and first_col + kk.shape[0] - 1 > r * ATT_SUB:
                    s = jnp.where(query_minus_key >= first_col - r * ATT_SUB, s, NEG_BIG)
                scores.append(s)
            for (hh, r), s in zip(group, scores):
                m_news.append(jnp.maximum(out[hh * n_sub + r][0], jnp.max(s, axis=0, keepdims=True)))
            for (hh, r), s, m_new in zip(group, scores, m_news):
                m, l, acc = out[hh * n_sub + r]
                alpha = jnp.exp(m - m_new)
                pe = jnp.exp(s - m_new)
                l_new = alpha * l + jnp.sum(pe, axis=0, keepdims=True)
                acc_new = alpha * acc + _dot_tn(vv, pe.astype(BF16))
                out[hh * n_sub + r] = (m_new, l_new, acc_new)
        return tuple(out)

    init = tuple((jnp.full((1, ATT_SUB), NEG_BIG, F32), jnp.zeros((1, ATT_SUB), F32),
                  jnp.zeros((LANES, ATT_SUB), F32)) for _ in range(2 * n_sub))
    carry = block(init, jnp.concatenate([km_ref[0], ckm_ref[0]], axis=1), vm_ref[0], None)

    def keys(j):
        start = pl.multiple_of(j * ATT_K_TILE, ATT_K_TILE)
        rows = pl.ds(start, ATT_K_TILE)
        return jnp.concatenate([k_ref[0, rows, :], ck_ref[0, rows, :]], axis=1), v_ref[0, rows, :]

    per_q = tile // ATT_K_TILE
    carry = lax.fori_loop(0, qi * per_q, lambda j, c: block(c, *keys(j), None), carry)
    for d in range(per_q):
        carry = block(carry, *keys(qi * per_q + d), d * ATT_K_TILE)
    first_head = lax.broadcasted_iota(jnp.int32, (LANES, ATT_SUB), 0) < HEAD_DIM
    for r in range(n_sub):
        (_, l0, acc0), (_, l1, acc1) = carry[r], carry[n_sub + r]
        o_t = jnp.where(first_head, acc0 / l0, acc1 / l1)
        o_ref[0, r * ATT_SUB:(r + 1) * ATT_SUB, :] = o_t.T.astype(BF16)


def _fox_prompt(q, cq, k_meta, ck_meta, v_meta, k, ck, v):
    batch, seq, _ = q.shape
    n_pairs = N_HEADS // 2
    return pl.pallas_call(
        _fox_prompt_kernel,
        grid=(batch, n_pairs, seq // ATT_TILE),
        in_specs=[
            pl.BlockSpec((1, ATT_TILE, LANES), lambda b, p, i: (b, i, p)),
            pl.BlockSpec((1, ATT_TILE, LANES), lambda b, p, i: (b, i, 0)),
            pl.BlockSpec((1, N_META, LANES), lambda b, p, i: (0, 0, p)),
            pl.BlockSpec((1, N_META, LANES), lambda b, p, i: (0, 0, 0)),
            pl.BlockSpec((1, N_META, LANES), lambda b, p, i: (0, 0, p)),
            pl.BlockSpec((1, seq, LANES), lambda b, p, i: (b, 0, p)),
            pl.BlockSpec((1, seq, LANES), lambda b, p, i: (b, 0, 0)),
            pl.BlockSpec((1, seq, LANES), lambda b, p, i: (b, 0, p)),
        ],
        out_specs=pl.BlockSpec((1, ATT_TILE, LANES), lambda b, p, i: (b, i, p)),
        out_shape=jax.ShapeDtypeStruct((batch, seq, WIDTH), BF16),
        compiler_params=pltpu.CompilerParams(
            dimension_semantics=("arbitrary", "arbitrary", "arbitrary"),
            vmem_limit_bytes=VMEM_LIMIT),
        name="fox_prompt",
    )(q, cq, k_meta, ck_meta, v_meta, k, ck, v)


def _fox_sample_kernel(pt_ref, q_ref, cq_ref, kn_ref, ckn_ref, vn_ref, *rest, n_new):
    del pt_ref
    g_pages = PAGES_PER_STEP
    k_refs, v_refs, lf_refs = rest[:g_pages], rest[g_pages:2 * g_pages], rest[2 * g_pages:3 * g_pages]
    o_ref, q_sc, cqm_sc, roff_sc, carry_sc, m_sc, l_sc, acc_sc = rest[3 * g_pages:]
    step = pl.program_id(1)
    rows = N_HEADS * n_new
    row_head = lax.broadcasted_iota(jnp.int32, (rows, 1), 0) & (N_HEADS - 1)

    def per_head_rows(x):
        return jnp.concatenate([jnp.broadcast_to(x[i:i + 1, :], (N_HEADS, x.shape[1]))
                                for i in range(n_new)], axis=0)

    @pl.when(step == 0)
    def _():
        q = per_head_rows(q_ref[0].astype(F32))
        lane_head = lax.broadcasted_iota(jnp.int32, (1, WIDTH), 1) >> HEAD_SHIFT
        q_sc[...] = jnp.where(row_head == lane_head, q, 0.0).astype(BF16)
        cq = per_head_rows(cq_ref[0].astype(F32))
        lane = lax.broadcasted_iota(jnp.int32, (1, LANES), 1)
        cqm = jnp.where((lane >> GROUP_SHIFT) == row_head, cq, 0.0)
        cqm_sc[...] = cqm.astype(BF16)
        slot = lax.broadcasted_iota(jnp.int32, cqm.shape, 1) & (BIAS_GROUP - 1)
        roff_sc[...] = jnp.sum(jnp.where(slot < BIAS_PARTS, cqm, 0.0), axis=-1, keepdims=True)
        carry_sc[...] = jnp.zeros(carry_sc.shape, F32)
        m_sc[...] = jnp.full(m_sc.shape, NEG_BIG, F32)
        l_sc[...] = jnp.zeros(l_sc.shape, F32)
        acc_sc[...] = jnp.zeros(acc_sc.shape, F32)

    q_bd = q_sc[...]
    ti = lax.broadcasted_iota(jnp.int32, (PAGE_SIZE, PAGE_SIZE), 0)
    tj = lax.broadcasted_iota(jnp.int32, (PAGE_SIZE, PAGE_SIZE), 1)
    later_and_all = jnp.concatenate([(ti > tj).astype(F32), jnp.ones((PAGE_SIZE, PAGE_SIZE), F32)],
                                    axis=1)
    lf_all = jnp.concatenate([lf_refs[g][0] for g in range(g_pages)], axis=0)
    sums = _sum_select(lf_all, later_and_all)

    carry = carry_sc[...]
    roff = roff_sc[...]
    def pages_t(refs, g0):
        return jnp.concatenate([refs[g][0].reshape(WIDTH, PAGE_SIZE) for g in (g0, g0 + 1)],
                               axis=1).astype(BF16)

    scores = []
    for g0 in range(0, g_pages, 2):
        bias = []
        for g in (g0, g0 + 1):
            page_sums = sums[g * N_HEADS:(g + 1) * N_HEADS, :]
            bias.append(jnp.concatenate([page_sums[:, :PAGE_SIZE] + carry] * n_new, axis=0))
            carry = carry + page_sums[:, PAGE_SIZE:]
        scores.append(_dot(q_bd, pages_t(k_refs, g0)) + jnp.concatenate(bias, axis=1) + roff)
    carry_sc[...] = carry

    m = m_sc[...]
    m_new = m
    for s in scores:
        m_new = jnp.maximum(m_new, jnp.max(s, axis=-1, keepdims=True))
    alpha = jnp.exp(m - m_new)
    l_new = alpha * l_sc[...]
    acc = alpha * acc_sc[...]
    for i, s in enumerate(scores):
        pe = jnp.exp(s - m_new)
        l_new = l_new + jnp.sum(pe, axis=-1, keepdims=True)
        acc = acc + _dot_nt(pe.astype(BF16), pages_t(v_refs, 2 * i))
    m_sc[...] = m_new
    l_sc[...] = l_new
    acc_sc[...] = acc

    @pl.when(step == pl.num_programs(1) - 1)
    def _():
        s = _dot_nt(q_bd, kn_ref[0]) + _dot_nt(cqm_sc[...], ckn_ref[0])
        row_tok = lax.broadcasted_iota(jnp.int32, (rows, 1), 0) >> (N_HEADS.bit_length() - 1)
        s = jnp.where(lax.broadcasted_iota(jnp.int32, (1, n_new), 1) <= row_tok, s, NEG_BIG)
        m_fin = jnp.maximum(m_new, jnp.max(s, axis=-1, keepdims=True))
        a_fin = jnp.exp(m_new - m_fin)
        pe = jnp.exp(s - m_fin)
        l_fin = a_fin * l_new + jnp.sum(pe, axis=-1, keepdims=True)
        out = (a_fin * acc + _dot(pe.astype(BF16), vn_ref[0])) / l_fin
        own = (lax.broadcasted_iota(jnp.int32, (N_HEADS, WIDTH), 0)
               == lax.broadcasted_iota(jnp.int32, (N_HEADS, WIDTH), 1) >> HEAD_SHIFT)
        o_ref[0] = jnp.concatenate(
            [jnp.sum(jnp.where(own, out[i * N_HEADS:(i + 1) * N_HEADS, :], 0.0), axis=0, keepdims=True)
             for i in range(n_new)], axis=0).astype(BF16)


def _fox_sample(page_table, q, cq, k_new, ck_new, v_new, cache_k, cache_v, cache_logf):
    n_seq, n_new, _ = q.shape
    n_pages = page_table.shape[1]
    g_pages = PAGES_PER_STEP
    rows = N_HEADS * n_new

    def per_seq(width):
        return pl.BlockSpec((1, n_new, width), lambda b, j, pt: (b, 0, 0))

    def page_spec(g, arr):
        block = (1,) + arr.shape[1:]
        return pl.BlockSpec(block, lambda b, j, pt: (pt[b * n_pages + j * g_pages + g],)
                            + (0,) * (arr.ndim - 1))

    visit_order = page_table[:, ::-1].reshape(-1)

    grid_spec = pltpu.PrefetchScalarGridSpec(
        num_scalar_prefetch=1,
        grid=(n_seq, n_pages // g_pages),
        in_specs=([per_seq(WIDTH), per_seq(LANES), per_seq(WIDTH), per_seq(LANES), per_seq(WIDTH)]
                  + [page_spec(g, cache_k) for g in range(g_pages)]
                  + [page_spec(g, cache_v) for g in range(g_pages)]
                  + [page_spec(g, cache_logf) for g in range(g_pages)]),
        out_specs=pl.BlockSpec((1, n_new, WIDTH), lambda b, j, pt: (b, 0, 0)),
        scratch_shapes=[pltpu.VMEM((rows, WIDTH), BF16), pltpu.VMEM((rows, LANES), BF16),
                        pltpu.VMEM((rows, 1), F32), pltpu.VMEM((N_HEADS, PAGE_SIZE), F32),
                        pltpu.VMEM((rows, 1), F32), pltpu.VMEM((rows, 1), F32),
                        pltpu.VMEM((rows, WIDTH), F32)],
    )
    return pl.pallas_call(
        functools.partial(_fox_sample_kernel, n_new=n_new),
        grid_spec=grid_spec,
        out_shape=jax.ShapeDtypeStruct((n_seq, n_new, WIDTH), BF16),
        compiler_params=pltpu.CompilerParams(
            dimension_semantics=("arbitrary", "arbitrary"), vmem_limit_bytes=VMEM_LIMIT),
        name="fox_sample",
    )(visit_order, q, cq, k_new, ck_new, v_new,
      *([cache_k] * g_pages), *([cache_v] * g_pages), *([cache_logf] * g_pages))


def _wkv_kernel(r_ref, lw_ref, kr_ref, vr_ref, a_ref, g_ref, s0_ref,
                kk_ref, ka_ref, rk_ref, gg_ref, gb_ref, o_ref, so_ref, s_sc):
    nb, C, _ = r_ref.shape
    n_pairs = N_HEADS // 2
    units = [(bb, pair) for bb in range(nb) for pair in range(n_pairs)]
    lo_state = ((lax.broadcasted_iota(jnp.int32, (LANES, LANES), 0) < HEAD_DIM)
                == (lax.broadcasted_iota(jnp.int32, (LANES, LANES), 1) < HEAD_DIM))

    @pl.when(pl.program_id(1) == 0)
    def _():
        zeros = jnp.zeros((HEAD_DIM, HEAD_DIM), F32)
        for bb, pair in units:
            sb = bb if s0_ref.shape[0] == nb else 0
            top = jnp.concatenate([s0_ref[sb, 2 * pair], zeros], axis=1)
            bottom = jnp.concatenate([zeros, s0_ref[sb, 2 * pair + 1]], axis=1)
            s_sc[bb, pair] = jnp.concatenate([top, bottom], axis=0)

    ri = lax.broadcasted_iota(jnp.int32, (C, C), 0)
    ci = lax.broadcasted_iota(jnp.int32, (C, C), 1)
    incl = ri >= ci
    strict = ri > ci
    eye = (ri == ci).astype(F32)
    lo = lax.broadcasted_iota(jnp.int32, (C, LANES), 1) < HEAD_DIM
    lo2 = lax.broadcasted_iota(jnp.int32, (2 * C, LANES), 1) < HEAD_DIM
    levels = max(1, (C - 1).bit_length())
    lcum = [_select_sum(incl.astype(F32), lw_ref[bb]) for bb in range(nb)]

    def per_head(x):
        first = jnp.sum(jnp.where(lo, x, 0.0), axis=-1, keepdims=True)
        second = jnp.sum(jnp.where(lo, 0.0, x), axis=-1, keepdims=True)
        return jnp.where(lo, first, second)

    xs, x_all, bts, kts, bgs, kgs, vbs, kps, e_ends = [], [], [], [], [], [], [], [], []
    for bb, pair in units:
        lanes = slice(pair * LANES, (pair + 1) * LANES)
        r2, lw2, kr2, a2 = (ref[bb, :, lanes] for ref in (r_ref, lw_ref, kr_ref, a_ref))
        lc = lcum[bb][:, lanes]
        kkr = kr2 * kk_ref[:, lanes]
        kk = kkr / jnp.maximum(jnp.sqrt(per_head(kkr * kkr)), 1e-12)
        kp = kr2 * (1.0 + (a2 - 1.0) * ka_ref[:, lanes])
        bv = kk * a2
        e_neg = jnp.exp(-lc)
        l_end = lc[C - 1:C, :]
        e_rem = jnp.exp(l_end - lc)
        x2 = jnp.concatenate([-kk * jnp.exp(lc - lw2), r2 * jnp.exp(lc)], axis=0)
        x_all.append(x2.astype(BF16))
        xs.append((jnp.where(lo2, x2, 0.0).astype(BF16), jnp.where(lo2, 0.0, x2).astype(BF16)))
        bts.append((bv * e_neg).astype(BF16))
        kts.append((kp * e_neg).astype(BF16))
        bgs.append((bv * e_rem).astype(BF16))
        kgs.append((kp * e_rem).astype(BF16))
        vbs.append(vr_ref[bb, :, lanes].astype(BF16))
        kps.append(kp)
        e_ends.append(jnp.exp(l_end))

    chains = [(u, hh) for u in range(len(units)) for hh in range(2)]
    sc_b = [_dot_nt(xs[u][hh], bts[u]) for u, hh in chains]
    sc_k = [_dot_nt(xs[u][hh], kts[u]) for u, hh in chains]
    n_ch = range(len(chains))
    a_ab = [jnp.where(strict, sc_b[i][:C], 0.0) for i in n_ch]
    a_ak = [jnp.where(strict, sc_k[i][:C], 0.0).astype(BF16) for i in n_ch]
    a_rb = [jnp.where(incl, sc_b[i][C:], 0.0).astype(BF16) for i in n_ch]
    a_rk = [jnp.where(incl, sc_k[i][C:], 0.0).astype(BF16) for i in n_ch]

    tinv = [eye + a_ab[i] for i in n_ch]
    if levels > 1:
        lb = [a_ab[i].astype(BF16) for i in n_ch]
        lpow = [_dot(lb[i], lb[i]) for i in n_ch]
        for level in range(1, levels):
            qb = [lpow[i].astype(BF16) for i in n_ch]
            if level == levels - 1:
                tinv = [tinv[i] + _dot(tinv[i].astype(BF16), qb[i]) for i in n_ch]
            else:
                both = [_dot(jnp.concatenate([tinv[i], lpow[i]], axis=0).astype(BF16), qb[i])
                        for i in n_ch]
                tinv = [tinv[i] + both[i][:C] for i in n_ch]
                lpow = [both[i][C:] for i in n_ch]
    tb = [tinv[i].astype(BF16) for i in n_ch]

    def both_heads(u, f):
        return jnp.where(lo, f(2 * u), f(2 * u + 1))

    n_u = range(len(units))
    s_old = [s_sc[bb, pair] for bb, pair in units]
    ps = [_dot_nt(x_all[u], s_old[u].astype(BF16)) for u in n_u]
    w1 = [(ps[u][:C] + both_heads(u, lambda i, u=u: _dot(a_ak[i], vbs[u]))).astype(BF16) for u in n_u]
    ub = [both_heads(u, lambda i, u=u: _dot(tb[i], w1[u])).astype(BF16) for u in n_u]
    ys = [ps[u][C:] + both_heads(u, lambda i, u=u: _dot(a_rb[i], ub[u]) + _dot(a_rk[i], vbs[u]))
          for u in n_u]
    for u, (bb, pair) in enumerate(units):
        cross = _dot_tn(ub[u], bgs[u]) + _dot_tn(vbs[u], kgs[u])
        s_sc[bb, pair] = s_old[u] * e_ends[u] + jnp.where(lo_state, cross, 0.0)

    for u, (bb, pair) in enumerate(units):
        lanes = slice(pair * LANES, (pair + 1) * LANES)
        y = ys[u]
        yc = y - per_head(y) * (1.0 / HEAD_DIM)
        var = per_head(yc * yc) * (1.0 / HEAD_DIM)
        yn = yc * lax.rsqrt(var + GN_EPS) * gg_ref[:, lanes] + gb_ref[:, lanes]
        bonus = per_head(r_ref[bb, :, lanes] * kps[u] * rk_ref[:, lanes]) * vr_ref[bb, :, lanes]
        o_ref[bb, :, lanes] = ((yn + bonus) * g_ref[bb, :, lanes]).astype(BF16)

    @pl.when(pl.program_id(1) == pl.num_programs(1) - 1)
    def _():
        for bb, pair in units:
            s2 = s_sc[bb, pair]
            so_ref[bb, 2 * pair] = s2[:HEAD_DIM, :HEAD_DIM]
            so_ref[bb, 2 * pair + 1] = pltpu.roll(s2[HEAD_DIM:, :], HEAD_DIM, 1)[:, :HEAD_DIM]


def _wkv(r, lw, kr, vr, a, g, s0, params, chunk, batch_tile):
    batch, seq, _ = r.shape
    assert chunk & (chunk - 1) == 0 and seq % chunk == 0 and batch % batch_tile == 0
    shared_state = s0.shape[0] == 1
    tok = pl.BlockSpec((batch_tile, chunk, WIDTH), lambda b, c: (b, c, 0))
    state_out = pl.BlockSpec((batch_tile, N_HEADS, HEAD_DIM, HEAD_DIM), lambda b, c: (b, 0, 0, 0))
    state_in = (pl.BlockSpec((1, N_HEADS, HEAD_DIM, HEAD_DIM), lambda b, c: (0, 0, 0, 0))
                if shared_state else state_out)
    return pl.pallas_call(
        _wkv_kernel,
        grid=(batch // batch_tile, seq // chunk),
        in_specs=[tok] * 6 + [state_in] + [_const_spec(p) for p in params],
        out_specs=[tok, state_out],
        out_shape=[jax.ShapeDtypeStruct((batch, seq, WIDTH), BF16),
                   jax.ShapeDtypeStruct((batch, N_HEADS, HEAD_DIM, HEAD_DIM), F32)],
        scratch_shapes=[pltpu.VMEM((batch_tile, N_HEADS // 2, LANES, LANES), F32)],
        compiler_params=pltpu.CompilerParams(
            dimension_semantics=("arbitrary", "arbitrary"), vmem_limit_bytes=VMEM_LIMIT),
        name="wkv",
    )(r, lw, kr, vr, a, g, s0, *params)


def _out_ffn_kernel(x_ref, att_ref, rw_ref, woa_ref, wor_ref, ln2_ref, wg_ref, wu_ref, wd_ref,
                    lnf_ref, y_ref):
    h = x_ref[...] + _dot(att_ref[...], woa_ref[...]) + _dot(rw_ref[...], wor_ref[...])
    hb = _rms(h, ln2_ref[...]).astype(BF16)
    ffn = jnp.zeros(h.shape, F32)
    d_ff = wg_ref.shape[1]
    for c0 in range(0, d_ff, FF_CHUNK):
        gate = _dot(hb, wg_ref[:, c0:c0 + FF_CHUNK])
        up = _dot(hb, wu_ref[:, c0:c0 + FF_CHUNK])
        act = gate * _sigmoid(gate) * up
        ffn = ffn + _dot(act.astype(BF16), wd_ref[c0:c0 + FF_CHUNK, :])
    y_ref[...] = _rms(h + ffn, lnf_ref[...])


def _out_ffn(x, att, rw, weights):
    rows = x.shape[0]
    tile = min(FFN_ROW_TILE, rows)
    tok = lambda width: pl.BlockSpec((tile, width), lambda i: (i, 0))
    resident = lambda w: pl.BlockSpec(w.shape, lambda i: (0,) * w.ndim, pipeline_mode=pl.Buffered(1))
    return pl.pallas_call(
        _out_ffn_kernel,
        grid=(rows // tile,),
        in_specs=[tok(D_MODEL), tok(WIDTH), tok(WIDTH)] + [resident(w) for w in weights],
        out_specs=tok(D_MODEL),
        out_shape=jax.ShapeDtypeStruct((rows, D_MODEL), F32),
        compiler_params=pltpu.CompilerParams(
            dimension_semantics=("arbitrary",), vmem_limit_bytes=VMEM_LIMIT),
        name="out_ffn",
    )(x, att, rw, *weights)


def _bias_selectors():
    part = jnp.arange(BIAS_PARTS)[:, None, None]
    src = jnp.arange(LANES)[None, :, None]
    dst = jnp.arange(LANES)[None, None, :]
    is_head = src < N_HEADS
    q_sel = (is_head & (dst == src * BIAS_GROUP + part)).astype(F32)
    k_sel = -(is_head & (dst == src * BIAS_GROUP + BIAS_PARTS + part)).astype(F32)
    sel = jnp.concatenate([q_sel, k_sel], axis=2).reshape(BIAS_PARTS * LANES, 2 * LANES)
    slot = jnp.arange(LANES) % BIAS_GROUP
    ones = jnp.concatenate([(slot >= BIAS_PARTS) & (slot < 2 * BIAS_PARTS),
                            slot < BIAS_PARTS]).astype(F32)[None, :]
    return sel.astype(BF16), ones


def kernel(x_prompt, x_sample, cache_k, cache_v, cache_logf, state_wkv, state_shift, page_table,
           meta_tokens, ln1_g, w_in, b_f, mu_shift, w0, w2, a0, a2, g2, k_k, k_a, r_k,
           gn_g, gn_b, w_out, ln2_g, w_gate, w_up, w_down, lnf_g):
    batch, seq, _ = x_prompt.shape
    n_seq, n_new, _ = x_sample.shape
    fox_cols = 3 * WIDTH + N_HEADS

    w_in0 = w_in[0]
    wqkv = w_in0[:, :3 * WIDTH].astype(BF16)
    wf = jnp.pad(w_in0[:, 3 * WIDTH:fox_cols], ((0, 0), (0, LANES - N_HEADS))).astype(BF16)
    bf = jnp.pad(b_f[0], (0, LANES - N_HEADS))[None, :]
    wrw = w_in0[:, fox_cols:].astype(BF16)
    sel, ones = _bias_selectors()
    w2p = jnp.concatenate([w2[0], jnp.zeros((ICLR_LORA, WIDTH), F32)], axis=0).astype(BF16)
    a2p = jnp.concatenate([jnp.zeros((DECAY_LORA, WIDTH), F32), a2[0]], axis=0).astype(BF16)
    inproj_w = (ln1_g[0][None, :], wqkv, wf, bf, wrw, mu_shift[0][None, :], sel, ones,
                w2p, a2p, g2[0].astype(BF16), w0[0][None, :], a0[0][None, :])
    wkv_p = tuple(t[0].reshape(1, WIDTH) for t in (k_k, k_a, r_k, gn_g, gn_b))
    ffn_w = (w_out[0][:WIDTH].astype(BF16), w_out[0][WIDTH:].astype(BF16), ln2_g[0][None, :],
             w_gate[0].astype(BF16), w_up[0].astype(BF16), w_down[0].astype(BF16), lnf_g[None, :])

    zrow = jnp.zeros((1, 1, RWKV_COLS), F32)
    zc = jnp.zeros((1, 1, LANES), F32)
    mo = _inproj_long(meta_tokens[None], zrow, zc, inproj_w, N_META)
    (_, mk, mkb, mv, mvb, mlf, _, mck, mr, mlw, mkr, mvr, ma, mg, _, mprow, mclast) = mo
    zstate = jnp.zeros((1, N_HEADS, HEAD_DIM, HEAD_DIM), F32)
    _, s_meta = _wkv(mr, mlw, mkr, mvr, ma, mg, zstate, wkv_p, N_META, 1)

    po = _inproj_long(x_prompt, mprow, mclast, inproj_w, ROW_TILE)
    (pq, pk, pkb, pv, pvb, plf, pcq, pck, pr, plw, pkr, pvr, pa, pg, pxl, _, _) = po
    att_p = _fox_prompt(pq, pcq, mkb, mck, mvb, pkb, pck, pvb)
    rw_p, wkv_prompt = _wkv(pr, plw, pkr, pvr, pa, pg, s_meta, wkv_p, WKV_CHUNK, WKV_PROMPT_SEQS)
    y_prompt = _out_ffn(x_prompt.reshape(batch * seq, D_MODEL), att_p.reshape(batch * seq, WIDTH),
                        rw_p.reshape(batch * seq, WIDTH), ffn_w).reshape(batch, seq, D_MODEL)

    rows_s = n_seq * n_new
    xprev = jnp.repeat(state_shift[0], n_new, axis=0)[None]
    so = _inproj_packed(x_sample.reshape(1, rows_s, D_MODEL), xprev, inproj_w, n_new)
    (sq, sk, skb, sv, svb, slf, scq, sck, sr, slw, skr, svr, sa, sg, sxn) = so
    seqs = lambda t: t.reshape(n_seq, n_new, t.shape[-1])
    att_s = _fox_sample(page_table, seqs(sq), seqs(scq), seqs(skb), seqs(sck), seqs(svb),
                        jnp.transpose(cache_k[0], (0, 2, 3, 1)), jnp.transpose(cache_v[0], (0, 2, 3, 1)),
                        jnp.transpose(cache_logf[0], (0, 2, 1)))
    rw_s, wkv_sample = _wkv(seqs(sr), seqs(slw), seqs(skr), seqs(svr), seqs(sa), seqs(sg),
                            state_wkv[0], wkv_p, n_new, WKV_SAMPLE_SEQS)
    y_sample = _out_ffn(x_sample.reshape(rows_s, D_MODEL), att_s.reshape(rows_s, WIDTH),
                        rw_s.reshape(rows_s, WIDTH), ffn_w).reshape(n_seq, n_new, D_MODEL)

    def with_meta(meta, main):
        return jnp.concatenate([jnp.broadcast_to(meta, (batch,) + meta.shape[1:]), main], axis=1)

    heads = lambda t: t.reshape(t.shape[:-1] + (N_HEADS, HEAD_DIM))
    k_prompt = heads(with_meta(mk, pk))[None]
    v_prompt = heads(with_meta(mv, pv))[None]
    logf_prompt = jnp.concatenate([jnp.broadcast_to(mlf, (batch,) + mlf.shape[1:]), plf], axis=2)
    logf_prompt = jnp.transpose(logf_prompt, (0, 2, 1))[None]
    shift_prompt = pxl.reshape(1, batch, D_MODEL)
    k_sample = heads(seqs(sk))[None]
    v_sample = heads(seqs(sv))[None]
    logf_sample = jnp.transpose(slf.reshape(N_HEADS, n_seq, n_new), (1, 2, 0))[None]
    shift_sample = seqs(sxn)[:, -1][None]
    return (y_prompt, y_sample, k_prompt, v_prompt, logf_prompt, wkv_prompt[None], shift_prompt,
            k_sample, v_sample, logf_sample, wkv_sample[None], shift_sample)
```

```python
import functools

import jax
import jax.numpy as jnp
from jax import lax
from jax.experimental import pallas as pl
from jax.experimental.pallas import tpu as pltpu

D_MODEL = 1024
N_META = 16
HEAD_DIM = 64
N_HEADS = 8
WIDTH = N_HEADS * HEAD_DIM
PAGE_SIZE = 128
DECAY_LORA = 64
ICLR_LORA = 64
GATE_LORA = 128
RWKV_COLS = 3 * WIDTH + DECAY_LORA + ICLR_LORA + GATE_LORA
RMS_EPS = 1e-6
GN_EPS = 64e-5
NEG_BIG = -1e30

LANES = 128
BIAS_GROUP = 16
BIAS_PARTS = 3
HEAD_SHIFT = HEAD_DIM.bit_length() - 1
GROUP_SHIFT = BIAS_GROUP.bit_length() - 1
VMEM_LIMIT = 56 * 1024 * 1024

ROW_TILE = 512
FFN_ROW_TILE = 512
ATT_TILE = 2048
ATT_K_TILE = 512
ATT_SUB = 256
ATT_GROUP = 16
WKV_CHUNK = 64
WKV_PROMPT_SEQS = 8
WKV_SAMPLE_SEQS = 4
PAGES_PER_STEP = 32
FF_CHUNK = 256

BF16 = jnp.bfloat16
F32 = jnp.float32


def _dot(a, b):
    return jnp.dot(a, b, preferred_element_type=F32)


def _dot_nt(a, b):
    return lax.dot_general(a, b, (((1,), (1,)), ((), ())), preferred_element_type=F32)


def _dot_tn(a, b):
    return lax.dot_general(a, b, (((0,), (0,)), ((), ())), preferred_element_type=F32)


def _rms(x, g):
    return x * lax.rsqrt(jnp.mean(x * x, axis=-1, keepdims=True) + RMS_EPS) * g


def _softplus(z):
    return jnp.maximum(z, 0.0) + jnp.log(1.0 + jnp.exp(-jnp.abs(z)))


def _sigmoid(z):
    return 1.0 / (1.0 + jnp.exp(-z))


def _mask_bf16(cond):
    return cond.astype(F32).astype(BF16)


def _split3(c):
    hi = c.astype(BF16)
    r1 = c - hi.astype(F32)
    mid = r1.astype(BF16)
    lo = (r1 - mid.astype(F32)).astype(BF16)
    return hi, mid, lo


def _select_sum(sel, x):
    n = x.shape[1]
    out = _dot(sel.astype(BF16), jnp.concatenate(_split3(x), axis=1))
    return out[:, :n] + out[:, n:2 * n] + out[:, 2 * n:]


def _sum_select(x, sel):
    m = x.shape[0]
    out = _dot(jnp.concatenate(_split3(x), axis=0), sel.astype(BF16))
    return out[:m] + out[m:2 * m] + out[2 * m:]


def _inproj_outputs(xn, qkv, logf, c, p, p_prev, w, outs):
    (mu_ref, sel_ref, ones_ref, w2_ref, a2_ref, g2_ref, w0_ref, a0_ref) = w
    (q_ref, k_ref, kb_ref, v_ref, vb_ref, lf_ref, cq_ref, ck_ref,
     r_ref, lw_ref, kr_ref, vr_ref, a_ref, g_ref) = outs
    q_ref[0] = (qkv[:, :WIDTH] * (HEAD_DIM ** -0.5)).astype(BF16)
    k = qkv[:, WIDTH:2 * WIDTH]
    k_ref[0] = k
    kb_ref[0] = k.astype(BF16)
    v = qkv[:, 2 * WIDTH:]
    v_ref[0] = v
    vb_ref[0] = v.astype(BF16)
    lf_ref[0] = logf.T[:N_HEADS, :]
    cparts = jnp.concatenate(_split3(c), axis=1)
    cc = _dot(cparts, sel_ref[...]) + ones_ref[...]
    cq_ref[0] = cc[:, :LANES].astype(BF16)
    ck_ref[0] = cc[:, LANES:].astype(BF16)
    rw = p + (p_prev - p) * mu_ref[...]
    r_ref[0] = rw[:, :WIDTH]
    kr_ref[0] = rw[:, WIDTH:2 * WIDTH]
    vr_ref[0] = rw[:, 2 * WIDTH:3 * WIDTH]
    z = rw[:, 3 * WIDTH:3 * WIDTH + LANES]
    gl = rw[:, 3 * WIDTH + LANES:]
    w_log = -_softplus(-(w0_ref[...] + _dot(jnp.tanh(z).astype(BF16), w2_ref[...]))) - 0.5
    lw_ref[0] = -jnp.exp(w_log)
    a_ref[0] = _sigmoid(a0_ref[...] + _dot(z.astype(BF16), a2_ref[...]))
    g_ref[0] = _dot(_sigmoid(gl).astype(BF16), g2_ref[...])


def _inproj_long_kernel(x_ref, prow_ref, c0_ref, ln_ref, wqkv_ref, wf_ref, bf_ref, wrw_ref,
                        mu_ref, sel_ref, ones_ref, w2_ref, a2_ref, g2_ref, w0_ref, a0_ref,
                        q_ref, k_ref, kb_ref, v_ref, vb_ref, lf_ref, cq_ref, ck_ref,
                        r_ref, lw_ref, kr_ref, vr_ref, a_ref, g_ref, xl_ref, pl_ref, cl_ref,
                        pcar_ref, ccar_ref):
    @pl.when(pl.program_id(1) == 0)
    def _():
        pcar_ref[...] = prow_ref[0]
        ccar_ref[...] = c0_ref[0]

    rows = x_ref.shape[1]
    xn = _rms(x_ref[0], ln_ref[...])
    xb = xn.astype(BF16)
    qkv = _dot(xb, wqkv_ref[...])
    logf = -_softplus(-(_dot(xb, wf_ref[...]) + bf_ref[...]))
    ri = lax.broadcasted_iota(jnp.int32, (rows, rows), 0)
    ci = lax.broadcasted_iota(jnp.int32, (rows, rows), 1)
    c = _select_sum((ri >= ci).astype(F32), logf) + ccar_ref[...]
    ccar_ref[...] = c[rows - 1:rows, :]
    p = _dot(xb, wrw_ref[...])
    first = lax.broadcasted_iota(jnp.int32, p.shape, 0) == 0
    p_prev = jnp.where(first, pcar_ref[...], pltpu.roll(p, 1, 0))
    pcar_ref[...] = p[rows - 1:rows, :]
    xl_ref[0] = xn[rows - 1:rows, :]
    pl_ref[0] = p[rows - 1:rows, :]
    cl_ref[0] = c[rows - 1:rows, :]
    _inproj_outputs(xn, qkv, logf, c, p, p_prev,
                    (mu_ref, sel_ref, ones_ref, w2_ref, a2_ref, g2_ref, w0_ref, a0_ref),
                    (q_ref, k_ref, kb_ref, v_ref, vb_ref, lf_ref, cq_ref, ck_ref,
                     r_ref, lw_ref, kr_ref, vr_ref, a_ref, g_ref))


def _inproj_packed_kernel(x_ref, xprev_ref, ln_ref, wqkv_ref, wf_ref, bf_ref, wrw_ref,
                          mu_ref, sel_ref, ones_ref, w2_ref, a2_ref, g2_ref, w0_ref, a0_ref,
                          q_ref, k_ref, kb_ref, v_ref, vb_ref, lf_ref, cq_ref, ck_ref,
                          r_ref, lw_ref, kr_ref, vr_ref, a_ref, g_ref, xn_ref, *, seq_len):
    rows = x_ref.shape[1]
    shift = seq_len.bit_length() - 1
    xn = _rms(x_ref[0], ln_ref[...])
    xn_ref[0] = xn
    xb = xn.astype(BF16)
    qkv = _dot(xb, wqkv_ref[...])
    logf = -_softplus(-(_dot(xb, wf_ref[...]) + bf_ref[...]))
    ri = lax.broadcasted_iota(jnp.int32, (rows, rows), 0)
    ci = lax.broadcasted_iota(jnp.int32, (rows, rows), 1)
    same_seq = (ri >> shift) == (ci >> shift)
    c = _select_sum(((ri >= ci) & same_seq).astype(F32), logf)
    p = _dot(xb, wrw_ref[...])
    p_first = _dot(xprev_ref[0].astype(BF16), wrw_ref[...])
    first = (lax.broadcasted_iota(jnp.int32, p.shape, 0) & (seq_len - 1)) == 0
    p_prev = jnp.where(first, p_first, pltpu.roll(p, 1, 0))
    _inproj_outputs(xn, qkv, logf, c, p, p_prev,
                    (mu_ref, sel_ref, ones_ref, w2_ref, a2_ref, g2_ref, w0_ref, a0_ref),
                    (q_ref, k_ref, kb_ref, v_ref, vb_ref, lf_ref, cq_ref, ck_ref,
                     r_ref, lw_ref, kr_ref, vr_ref, a_ref, g_ref))


def _const_spec(arr):
    return pl.BlockSpec(arr.shape, lambda *_: (0,) * arr.ndim)


def _inproj_out_shapes(batch, rows_total):
    def s(width, dtype):
        return jax.ShapeDtypeStruct((batch, rows_total, width), dtype)
    logf_t = jax.ShapeDtypeStruct((batch, N_HEADS, rows_total), F32)
    return [s(WIDTH, BF16), s(WIDTH, F32), s(WIDTH, BF16), s(WIDTH, F32), s(WIDTH, BF16),
            logf_t, s(LANES, BF16), s(LANES, BF16)] + [s(WIDTH, F32)] * 6


def _inproj_out_specs(tile):
    def s(width):
        return pl.BlockSpec((1, tile, width), lambda b, i: (b, i, 0))
    logf_t = pl.BlockSpec((1, N_HEADS, tile), lambda b, i: (b, 0, i))
    return [s(WIDTH)] * 5 + [logf_t, s(LANES), s(LANES)] + [s(WIDTH)] * 6


def _inproj_long(x, prow, c0, weights, tile):
    batch, seq, _ = x.shape
    n_tiles = seq // tile
    row = lambda width: pl.BlockSpec((1, 1, width), lambda b, i: (b, 0, 0))
    shapes = _inproj_out_shapes(batch, seq) + [
        jax.ShapeDtypeStruct((batch, 1, D_MODEL), F32),
        jax.ShapeDtypeStruct((batch, 1, RWKV_COLS), F32),
        jax.ShapeDtypeStruct((batch, 1, LANES), F32)]
    specs = _inproj_out_specs(tile) + [row(D_MODEL), row(RWKV_COLS), row(LANES)]
    return pl.pallas_call(
        _inproj_long_kernel,
        grid=(batch, n_tiles),
        in_specs=[pl.BlockSpec((1, tile, D_MODEL), lambda b, i: (b, i, 0)),
                  _const_spec(prow), _const_spec(c0)] + [_const_spec(w) for w in weights],
        out_specs=specs,
        out_shape=shapes,
        scratch_shapes=[pltpu.VMEM((1, RWKV_COLS), F32), pltpu.VMEM((1, LANES), F32)],
        compiler_params=pltpu.CompilerParams(
            dimension_semantics=("arbitrary", "arbitrary"), vmem_limit_bytes=VMEM_LIMIT),
        name="inproj_long",
    )(x, prow, c0, *weights)


def _inproj_packed(x, xprev, weights, seq_len):
    _, rows, _ = x.shape
    full = pl.BlockSpec((1, rows, D_MODEL), lambda b, i: (0, 0, 0))
    return pl.pallas_call(
        functools.partial(_inproj_packed_kernel, seq_len=seq_len),
        grid=(1, 1),
        in_specs=[full, full] + [_const_spec(w) for w in weights],
        out_specs=_inproj_out_specs(rows) + [full],
        out_shape=_inproj_out_shapes(1, rows) + [jax.ShapeDtypeStruct((1, rows, D_MODEL), F32)],
        compiler_params=pltpu.CompilerParams(
            dimension_semantics=("arbitrary", "arbitrary"), vmem_limit_bytes=VMEM_LIMIT),
        name="inproj_packed",
    )(x, xprev, *weights)


def _fox_prompt_kernel(q_ref, cq_ref, km_ref, ckm_ref, vm_ref, k_ref, ck_ref, v_ref, o_ref):
    pair = pl.program_id(1)
    qi = pl.program_id(2)
    tile = q_ref.shape[1]
    lane = lax.broadcasted_iota(jnp.int32, (1, LANES), 1)
    q2 = q_ref[0]
    cq = cq_ref[0]
    n_sub = tile // ATT_SUB
    lhs = []
    for hh in range(2):
        head_lanes = (lane >= hh * HEAD_DIM) & (lane < (hh + 1) * HEAD_DIM)
        g0 = (2 * pair + hh) * BIAS_GROUP
        group_lanes = (lane >= g0) & (lane < g0 + BIAS_GROUP)
        full = jnp.concatenate([q2 * _mask_bf16(head_lanes), cq * _mask_bf16(group_lanes)], axis=1)
        lhs.append([full[r * ATT_SUB:(r + 1) * ATT_SUB, :] for r in range(n_sub)])
    query_minus_key = (lax.broadcasted_iota(jnp.int32, (ATT_K_TILE, ATT_SUB), 1)
                       - lax.broadcasted_iota(jnp.int32, (ATT_K_TILE, ATT_SUB), 0))

    def block(carry, kk, vv, first_col):
        out = list(carry)
        live = [(hh, r) for hh in range(2) for r in range(n_sub)
                if first_col is None or first_col <= (r + 1) * ATT_SUB - 1]
        for g0 in range(0, len(live), ATT_GROUP):
            group = live[g0:g0 + ATT_GROUP]
            scores, m_news = [], []
            for hh, r in group:
                s = _dot_nt(kk, lhs[hh][r])
                if first_col is not None and first_col + kk.shape[0] - 1 > r * ATT_SUB:
                    s = jnp.where(query_minus_key >= first_col - r * ATT_SUB, s, NEG_BIG)
                scores.append(s)
            for (hh, r), s in zip(group, scores):
                m_news.append(jnp.maximum(out[hh * n_sub + r][0], jnp.max(s, axis=0, keepdims=True)))
            for (hh, r), s, m_new in zip(group, scores, m_news):
                m, l, acc = out[hh * n_sub + r]
                alpha = jnp.exp(m - m_new)
                pe = jnp.exp(s - m_new)
                l_new = alpha * l + jnp.sum(pe, axis=0, keepdims=True)
                acc_new = alpha * acc + _dot_tn(vv, pe.astype(BF16))
                out[hh * n_sub + r] = (m_new, l_new, acc_new)
        return tuple(out)

    init = tuple((jnp.full((1, ATT_SUB), NEG_BIG, F32), jnp.zeros((1, ATT_SUB), F32),
                  jnp.zeros((LANES, ATT_SUB), F32)) for _ in range(2 * n_sub))
    carry = block(init, jnp.concatenate([km_ref[0], ckm_ref[0]], axis=1), vm_ref[0], None)

    def keys(j):
        start = pl.multiple_of(j * ATT_K_TILE, ATT_K_TILE)
        rows = pl.ds(start, ATT_K_TILE)
        return jnp.concatenate([k_ref[0, rows, :], ck_ref[0, rows, :]], axis=1), v_ref[0, rows, :]

    per_q = tile // ATT_K_TILE
    carry = lax.fori_loop(0, qi * per_q, lambda j, c: block(c, *keys(j), None), carry)
    for d in range(per_q):
        carry = block(carry, *keys(qi * per_q + d), d * ATT_K_TILE)
    first_head = lax.broadcasted_iota(jnp.int32, (LANES, ATT_SUB), 0) < HEAD_DIM
    for r in range(n_sub):
        (_, l0, acc0), (_, l1, acc1) = carry[r], carry[n_sub + r]
        o_t = jnp.where(first_head, acc0 / l0, acc1 / l1)
        o_ref[0, r * ATT_SUB:(r + 1) * ATT_SUB, :] = o_t.T.astype(BF16)


def _fox_prompt(q, cq, k_meta, ck_meta, v_meta, k, ck, v):
    batch, seq, _ = q.shape
    n_pairs = N_HEADS // 2
    return pl.pallas_call(
        _fox_prompt_kernel,
        grid=(batch, n_pairs, seq // ATT_TILE),
        in_specs=[
            pl.BlockSpec((1, ATT_TILE, LANES), lambda b, p, i: (b, i, p)),
            pl.BlockSpec((1, ATT_TILE, LANES), lambda b, p, i: (b, i, 0)),
            pl.BlockSpec((1, N_META, LANES), lambda b, p, i: (0, 0, p)),
            pl.BlockSpec((1, N_META, LANES), lambda b, p, i: (0, 0, 0)),
            pl.BlockSpec((1, N_META, LANES), lambda b, p, i: (0, 0, p)),
            pl.BlockSpec((1, seq, LANES), lambda b, p, i: (b, 0, p)),
            pl.BlockSpec((1, seq, LANES), lambda b, p, i: (b, 0, 0)),
            pl.BlockSpec((1, seq, LANES), lambda b, p, i: (b, 0, p)),
        ],
        out_specs=pl.BlockSpec((1, ATT_TILE, LANES), lambda b, p, i: (b, i, p)),
        out_shape=jax.ShapeDtypeStruct((batch, seq, WIDTH), BF16),
        compiler_params=pltpu.CompilerParams(
            dimension_semantics=("arbitrary", "arbitrary", "arbitrary"),
            vmem_limit_bytes=VMEM_LIMIT),
        name="fox_prompt",
    )(q, cq, k_meta, ck_meta, v_meta, k, ck, v)


def _fox_sample_kernel(pt_ref, q_ref, cq_ref, kn_ref, ckn_ref, vn_ref, *rest, n_new):
    del pt_ref
    g_pages = PAGES_PER_STEP
    k_refs, v_refs, lf_refs = rest[:g_pages], rest[g_pages:2 * g_pages], rest[2 * g_pages:3 * g_pages]
    o_ref, q_sc, cqm_sc, roff_sc, carry_sc, m_sc, l_sc, acc_sc = rest[3 * g_pages:]
    step = pl.program_id(1)
    rows = N_HEADS * n_new
    row_head = lax.broadcasted_iota(jnp.int32, (rows, 1), 0) & (N_HEADS - 1)

    def per_head_rows(x):
        return jnp.concatenate([jnp.broadcast_to(x[i:i + 1, :], (N_HEADS, x.shape[1]))
                                for i in range(n_new)], axis=0)

    @pl.when(step == 0)
    def _():
        q = per_head_rows(q_ref[0].astype(F32))
        lane_head = lax.broadcasted_iota(jnp.int32, (1, WIDTH), 1) >> HEAD_SHIFT
        q_sc[...] = jnp.where(row_head == lane_head, q, 0.0).astype(BF16)
        cq = per_head_rows(cq_ref[0].astype(F32))
        lane = lax.broadcasted_iota(jnp.int32, (1, LANES), 1)
        cqm = jnp.where((lane >> GROUP_SHIFT) == row_head, cq, 0.0)
        cqm_sc[...] = cqm.astype(BF16)
        slot = lax.broadcasted_iota(jnp.int32, cqm.shape, 1) & (BIAS_GROUP - 1)
        roff_sc[...] = jnp.sum(jnp.where(slot < BIAS_PARTS, cqm, 0.0), axis=-1, keepdims=True)
        carry_sc[...] = jnp.zeros(carry_sc.shape, F32)
        m_sc[...] = jnp.full(m_sc.shape, NEG_BIG, F32)
        l_sc[...] = jnp.zeros(l_sc.shape, F32)
        acc_sc[...] = jnp.zeros(acc_sc.shape, F32)

    q_bd = q_sc[...]
    ti = lax.broadcasted_iota(jnp.int32, (PAGE_SIZE, PAGE_SIZE), 0)
    tj = lax.broadcasted_iota(jnp.int32, (PAGE_SIZE, PAGE_SIZE), 1)
    later_and_all = jnp.concatenate([(ti > tj).astype(F32), jnp.ones((PAGE_SIZE, PAGE_SIZE), F32)],
                                    axis=1)
    lf_all = jnp.concatenate([lf_refs[g][0] for g in range(g_pages)], axis=0)
    sums = _sum_select(lf_all, later_and_all)

    carry = carry_sc[...]
    roff = roff_sc[...]
    def pages_t(refs, g0):
        return jnp.concatenate([refs[g][0].reshape(WIDTH, PAGE_SIZE) for g in (g0, g0 + 1)],
                               axis=1).astype(BF16)

    scores = []
    for g0 in range(0, g_pages, 2):
        bias = []
        for g in (g0, g0 + 1):
            page_sums = sums[g * N_HEADS:(g + 1) * N_HEADS, :]
            bias.append(jnp.concatenate([page_sums[:, :PAGE_SIZE] + carry] * n_new, axis=0))
            carry = carry + page_sums[:, PAGE_SIZE:]
        scores.append(_dot(q_bd, pages_t(k_refs, g0)) + jnp.concatenate(bias, axis=1) + roff)
    carry_sc[...] = carry

    m = m_sc[...]
    m_new = m
    for s in scores:
        m_new = jnp.maximum(m_new, jnp.max(s, axis=-1, keepdims=True))
    alpha = jnp.exp(m - m_new)
    l_new = alpha * l_sc[...]
    acc = alpha * acc_sc[...]
    for i, s in enumerate(scores):
        pe = jnp.exp(s - m_new)
        l_new = l_new + jnp.sum(pe, axis=-1, keepdims=True)
        acc = acc + _dot_nt(pe.astype(BF16), pages_t(v_refs, 2 * i))
    m_sc[...] = m_new
    l_sc[...] = l_new
    acc_sc[...] = acc

    @pl.when(step == pl.num_programs(1) - 1)
    def _():
        s = _dot_nt(q_bd, kn_ref[0]) + _dot_nt(cqm_sc[...], ckn_ref[0])
        row_tok = lax.broadcasted_iota(jnp.int32, (rows, 1), 0) >> (N_HEADS.bit_length() - 1)
        s = jnp.where(lax.broadcasted_iota(jnp.int32, (1, n_new), 1) <= row_tok, s, NEG_BIG)
        m_fin = jnp.maximum(m_new, jnp.max(s, axis=-1, keepdims=True))
        a_fin = jnp.exp(m_new - m_fin)
        pe = jnp.exp(s - m_fin)
        l_fin = a_fin * l_new + jnp.sum(pe, axis=-1, keepdims=True)
        out = (a_fin * acc + _dot(pe.astype(BF16), vn_ref[0])) / l_fin
        own = (lax.broadcasted_iota(jnp.int32, (N_HEADS, WIDTH), 0)
               == lax.broadcasted_iota(jnp.int32, (N_HEADS, WIDTH), 1) >> HEAD_SHIFT)
        o_ref[0] = jnp.concatenate(
            [jnp.sum(jnp.where(own, out[i * N_HEADS:(i + 1) * N_HEADS, :], 0.0), axis=0, keepdims=True)
             for i in range(n_new)], axis=0).astype(BF16)


def _fox_sample(page_table, q, cq, k_new, ck_new, v_new, cache_k, cache_v, cache_logf):
    n_seq, n_new, _ = q.shape
    n_pages = page_table.shape[1]
    g_pages = PAGES_PER_STEP
    rows = N_HEADS * n_new

    def per_seq(width):
        return pl.BlockSpec((1, n_new, width), lambda b, j, pt: (b, 0, 0))

    def page_spec(g, arr):
        block = (1,) + arr.shape[1:]
        return pl.BlockSpec(block, lambda b, j, pt: (pt[b * n_pages + j * g_pages + g],)
                            + (0,) * (arr.ndim - 1))

    visit_order = page_table[:, ::-1].reshape(-1)

    grid_spec = pltpu.PrefetchScalarGridSpec(
        num_scalar_prefetch=1,
        grid=(n_seq, n_pages // g_pages),
        in_specs=([per_seq(WIDTH), per_seq(LANES), per_seq(WIDTH), per_seq(LANES), per_seq(WIDTH)]
                  + [page_spec(g, cache_k) for g in range(g_pages)]
                  + [page_spec(g, cache_v) for g in range(g_pages)]
                  + [page_spec(g, cache_logf) for g in range(g_pages)]),
        out_specs=pl.BlockSpec((1, n_new, WIDTH), lambda b, j, pt: (b, 0, 0)),
        scratch_shapes=[pltpu.VMEM((rows, WIDTH), BF16), pltpu.VMEM((rows, LANES), BF16),
                        pltpu.VMEM((rows, 1), F32), pltpu.VMEM((N_HEADS, PAGE_SIZE), F32),
                        pltpu.VMEM((rows, 1), F32), pltpu.VMEM((rows, 1), F32),
                        pltpu.VMEM((rows, WIDTH), F32)],
    )
    return pl.pallas_call(
        functools.partial(_fox_sample_kernel, n_new=n_new),
        grid_spec=grid_spec,
        out_shape=jax.ShapeDtypeStruct((n_seq, n_new, WIDTH), BF16),
        compiler_params=pltpu.CompilerParams(
            dimension_semantics=("arbitrary", "arbitrary"), vmem_limit_bytes=VMEM_LIMIT),
        name="fox_sample",
    )(visit_order, q, cq, k_new, ck_new, v_new,
      *([cache_k] * g_pages), *([cache_v] * g_pages), *([cache_logf] * g_pages))


def _wkv_kernel(r_ref, lw_ref, kr_ref, vr_ref, a_ref, g_ref, s0_ref,
                kk_ref, ka_ref, rk_ref, gg_ref, gb_ref, o_ref, so_ref, s_sc):
    nb, C, _ = r_ref.shape
    n_pairs = N_HEADS // 2
    units = [(bb, pair) for bb in range(nb) for pair in range(n_pairs)]
    lo_state = ((lax.broadcasted_iota(jnp.int32, (LANES, LANES), 0) < HEAD_DIM)
                == (lax.broadcasted_iota(jnp.int32, (LANES, LANES), 1) < HEAD_DIM))

    @pl.when(pl.program_id(1) == 0)
    def _():
        zeros = jnp.zeros((HEAD_DIM, HEAD_DIM), F32)
        for bb, pair in units:
            sb = bb if s0_ref.shape[0] == nb else 0
            top = jnp.concatenate([s0_ref[sb, 2 * pair], zeros], axis=1)
            bottom = jnp.concatenate([zeros, s0_ref[sb, 2 * pair + 1]], axis=1)
            s_sc[bb, pair] = jnp.concatenate([top, bottom], axis=0)

    ri = lax.broadcasted_iota(jnp.int32, (C, C), 0)
    ci = lax.broadcasted_iota(jnp.int32, (C, C), 1)
    incl = ri >= ci
    strict = ri > ci
    eye = (ri == ci).astype(F32)
    lo = lax.broadcasted_iota(jnp.int32, (C, LANES), 1) < HEAD_DIM
    lo2 = lax.broadcasted_iota(jnp.int32, (2 * C, LANES), 1) < HEAD_DIM
    levels = max(1, (C - 1).bit_length())
    lcum = [_select_sum(incl.astype(F32), lw_ref[bb]) for bb in range(nb)]

    def per_head(x):
        first = jnp.sum(jnp.where(lo, x, 0.0), axis=-1, keepdims=True)
        second = jnp.sum(jnp.where(lo, 0.0, x), axis=-1, keepdims=True)
        return jnp.where(lo, first, second)

    xs, x_all, bts, kts, bgs, kgs, vbs, kps, e_ends = [], [], [], [], [], [], [], [], []
    for bb, pair in units:
        lanes = slice(pair * LANES, (pair + 1) * LANES)
        r2, lw2, kr2, a2 = (ref[bb, :, lanes] for ref in (r_ref, lw_ref, kr_ref, a_ref))
        lc = lcum[bb][:, lanes]
        kkr = kr2 * kk_ref[:, lanes]
        kk = kkr / jnp.maximum(jnp.sqrt(per_head(kkr * kkr)), 1e-12)
        kp = kr2 * (1.0 + (a2 - 1.0) * ka_ref[:, lanes])
        bv = kk * a2
        e_neg = jnp.exp(-lc)
        l_end = lc[C - 1:C, :]
        e_rem = jnp.exp(l_end - lc)
        x2 = jnp.concatenate([-kk * jnp.exp(lc - lw2), r2 * jnp.exp(lc)], axis=0)
        x_all.append(x2.astype(BF16))
        xs.append((jnp.where(lo2, x2, 0.0).astype(BF16), jnp.where(lo2, 0.0, x2).astype(BF16)))
        bts.append((bv * e_neg).astype(BF16))
        kts.append((kp * e_neg).astype(BF16))
        bgs.append((bv * e_rem).astype(BF16))
        kgs.append((kp * e_rem).astype(BF16))
        vbs.append(vr_ref[bb, :, lanes].astype(BF16))
        kps.append(kp)
        e_ends.append(jnp.exp(l_end))

    chains = [(u, hh) for u in range(len(units)) for hh in range(2)]
    sc_b = [_dot_nt(xs[u][hh], bts[u]) for u, hh in chains]
    sc_k = [_dot_nt(xs[u][hh], kts[u]) for u, hh in chains]
    n_ch = range(len(chains))
    a_ab = [jnp.where(strict, sc_b[i][:C], 0.0) for i in n_ch]
    a_ak = [jnp.where(strict, sc_k[i][:C], 0.0).astype(BF16) for i in n_ch]
    a_rb = [jnp.where(incl, sc_b[i][C:], 0.0).astype(BF16) for i in n_ch]
    a_rk = [jnp.where(incl, sc_k[i][C:], 0.0).astype(BF16) for i in n_ch]

    tinv = [eye + a_ab[i] for i in n_ch]
    if levels > 1:
        lb = [a_ab[i].astype(BF16) for i in n_ch]
        lpow = [_dot(lb[i], lb[i]) for i in n_ch]
        for level in range(1, levels):
            qb = [lpow[i].astype(BF16) for i in n_ch]
            if level == levels - 1:
                tinv = [tinv[i] + _dot(tinv[i].astype(BF16), qb[i]) for i in n_ch]
            else:
                both = [_dot(jnp.concatenate([tinv[i], lpow[i]], axis=0).astype(BF16), qb[i])
                        for i in n_ch]
                tinv = [tinv[i] + both[i][:C] for i in n_ch]
                lpow = [both[i][C:] for i in n_ch]
    tb = [tinv[i].astype(BF16) for i in n_ch]

    def both_heads(u, f):
        return jnp.where(lo, f(2 * u), f(2 * u + 1))

    n_u = range(len(units))
    s_old = [s_sc[bb, pair] for bb, pair in units]
    ps = [_dot_nt(x_all[u], s_old[u].astype(BF16)) for u in n_u]
    w1 = [(ps[u][:C] + both_heads(u, lambda i, u=u: _dot(a_ak[i], vbs[u]))).astype(BF16) for u in n_u]
    ub = [both_heads(u, lambda i, u=u: _dot(tb[i], w1[u])).astype(BF16) for u in n_u]
    ys = [ps[u][C:] + both_heads(u, lambda i, u=u: _dot(a_rb[i], ub[u]) + _dot(a_rk[i], vbs[u]))
          for u in n_u]
    for u, (bb, pair) in enumerate(units):
        cross = _dot_tn(ub[u], bgs[u]) + _dot_tn(vbs[u], kgs[u])
        s_sc[bb, pair] = s_old[u] * e_ends[u] + jnp.where(lo_state, cross, 0.0)

    for u, (bb, pair) in enumerate(units):
        lanes = slice(pair * LANES, (pair + 1) * LANES)
        y = ys[u]
        yc = y - per_head(y) * (1.0 / HEAD_DIM)
        var = per_head(yc * yc) * (1.0 / HEAD_DIM)
        yn = yc * lax.rsqrt(var + GN_EPS) * gg_ref[:, lanes] + gb_ref[:, lanes]
        bonus = per_head(r_ref[bb, :, lanes] * kps[u] * rk_ref[:, lanes]) * vr_ref[bb, :, lanes]
        o_ref[bb, :, lanes] = ((yn + bonus) * g_ref[bb, :, lanes]).astype(BF16)

    @pl.when(pl.program_id(1) == pl.num_programs(1) - 1)
    def _():
        for bb, pair in units:
            s2 = s_sc[bb, pair]
            so_ref[bb, 2 * pair] = s2[:HEAD_DIM, :HEAD_DIM]
            so_ref[bb, 2 * pair + 1] = pltpu.roll(s2[HEAD_DIM:, :], HEAD_DIM, 1)[:, :HEAD_DIM]


def _wkv(r, lw, kr, vr, a, g, s0, params, chunk, batch_tile):
    batch, seq, _ = r.shape
    assert chunk & (chunk - 1) == 0 and seq % chunk == 0 and batch % batch_tile == 0
    shared_state = s0.shape[0] == 1
    tok = pl.BlockSpec((batch_tile, chunk, WIDTH), lambda b, c: (b, c, 0))
    state_out = pl.BlockSpec((batch_tile, N_HEADS, HEAD_DIM, HEAD_DIM), lambda b, c: (b, 0, 0, 0))
    state_in = (pl.BlockSpec((1, N_HEADS, HEAD_DIM, HEAD_DIM), lambda b, c: (0, 0, 0, 0))
                if shared_state else state_out)
    return pl.pallas_call(
        _wkv_kernel,
        grid=(batch // batch_tile, seq // chunk),
        in_specs=[tok] * 6 + [state_in] + [_const_spec(p) for p in params],
        out_specs=[tok, state_out],
        out_shape=[jax.ShapeDtypeStruct((batch, seq, WIDTH), BF16),
                   jax.ShapeDtypeStruct((batch, N_HEADS, HEAD_DIM, HEAD_DIM), F32)],
        scratch_shapes=[pltpu.VMEM((batch_tile, N_HEADS // 2, LANES, LANES), F32)],
        compiler_params=pltpu.CompilerParams(
            dimension_semantics=("arbitrary", "arbitrary"), vmem_limit_bytes=VMEM_LIMIT),
        name="wkv",
    )(r, lw, kr, vr, a, g, s0, *params)


def _out_ffn_kernel(x_ref, att_ref, rw_ref, woa_ref, wor_ref, ln2_ref, wg_ref, wu_ref, wd_ref,
                    lnf_ref, y_ref):
    h = x_ref[...] + _dot(att_ref[...], woa_ref[...]) + _dot(rw_ref[...], wor_ref[...])
    hb = _rms(h, ln2_ref[...]).astype(BF16)
    ffn = jnp.zeros(h.shape, F32)
    d_ff = wg_ref.shape[1]
    for c0 in range(0, d_ff, FF_CHUNK):
        gate = _dot(hb, wg_ref[:, c0:c0 + FF_CHUNK])
        up = _dot(hb, wu_ref[:, c0:c0 + FF_CHUNK])
        act = gate * _sigmoid(gate) * up
        ffn = ffn + _dot(act.astype(BF16), wd_ref[c0:c0 + FF_CHUNK, :])
    y_ref[...] = _rms(h + ffn, lnf_ref[...])


def _out_ffn(x, att, rw, weights):
    rows = x.shape[0]
    tile = min(FFN_ROW_TILE, rows)
    tok = lambda width: pl.BlockSpec((tile, width), lambda i: (i, 0))
    resident = lambda w: pl.BlockSpec(w.shape, lambda i: (0,) * w.ndim, pipeline_mode=pl.Buffered(1))
    return pl.pallas_call(
        _out_ffn_kernel,
        grid=(rows // tile,),
        in_specs=[tok(D_MODEL), tok(WIDTH), tok(WIDTH)] + [resident(w) for w in weights],
        out_specs=tok(D_MODEL),
        out_shape=jax.ShapeDtypeStruct((rows, D_MODEL), F32),
        compiler_params=pltpu.CompilerParams(
            dimension_semantics=("arbitrary",), vmem_limit_bytes=VMEM_LIMIT),
        name="out_ffn",
    )(x, att, rw, *weights)


def _bias_selectors():
    part = jnp.arange(BIAS_PARTS)[:, None, None]
    src = jnp.arange(LANES)[None, :, None]
    dst = jnp.arange(LANES)[None, None, :]
    is_head = src < N_HEADS
    q_sel = (is_head & (dst == src * BIAS_GROUP + part)).astype(F32)
    k_sel = -(is_head & (dst == src * BIAS_GROUP + BIAS_PARTS + part)).astype(F32)
    sel = jnp.concatenate([q_sel, k_sel], axis=2).reshape(BIAS_PARTS * LANES, 2 * LANES)
    slot = jnp.arange(LANES) % BIAS_GROUP
    ones = jnp.concatenate([(slot >= BIAS_PARTS) & (slot < 2 * BIAS_PARTS),
                            slot < BIAS_PARTS]).astype(F32)[None, :]
    return sel.astype(BF16), ones


def kernel(x_prompt, x_sample, cache_k, cache_v, cache_logf, state_wkv, state_shift, page_table,
           meta_tokens, ln1_g, w_in, b_f, mu_shift, w0, w2, a0, a2, g2, k_k, k_a, r_k,
           gn_g, gn_b, w_out, ln2_g, w_gate, w_up, w_down, lnf_g):
    batch, seq, _ = x_prompt.shape
    n_seq, n_new, _ = x_sample.shape
    fox_cols = 3 * WIDTH + N_HEADS

    w_in0 = w_in[0]
    wqkv = w_in0[:, :3 * WIDTH].astype(BF16)
    wf = jnp.pad(w_in0[:, 3 * WIDTH:fox_cols], ((0, 0), (0, LANES - N_HEADS))).astype(BF16)
    bf = jnp.pad(b_f[0], (0, LANES - N_HEADS))[None, :]
    wrw = w_in0[:, fox_cols:].astype(BF16)
    sel, ones = _bias_selectors()
    w2p = jnp.concatenate([w2[0], jnp.zeros((ICLR_LORA, WIDTH), F32)], axis=0).astype(BF16)
    a2p = jnp.concatenate([jnp.zeros((DECAY_LORA, WIDTH), F32), a2[0]], axis=0).astype(BF16)
    inproj_w = (ln1_g[0][None, :], wqkv, wf, bf, wrw, mu_shift[0][None, :], sel, ones,
                w2p, a2p, g2[0].astype(BF16), w0[0][None, :], a0[0][None, :])
    wkv_p = tuple(t[0].reshape(1, WIDTH) for t in (k_k, k_a, r_k, gn_g, gn_b))
    ffn_w = (w_out[0][:WIDTH].astype(BF16), w_out[0][WIDTH:].astype(BF16), ln2_g[0][None, :],
             w_gate[0].astype(BF16), w_up[0].astype(BF16), w_down[0].astype(BF16), lnf_g[None, :])

    zrow = jnp.zeros((1, 1, RWKV_COLS), F32)
    zc = jnp.zeros((1, 1, LANES), F32)
    mo = _inproj_long(meta_tokens[None], zrow, zc, inproj_w, N_META)
    (_, mk, mkb, mv, mvb, mlf, _, mck, mr, mlw, mkr, mvr, ma, mg, _, mprow, mclast) = mo
    zstate = jnp.zeros((1, N_HEADS, HEAD_DIM, HEAD_DIM), F32)
    _, s_meta = _wkv(mr, mlw, mkr, mvr, ma, mg, zstate, wkv_p, N_META, 1)

    po = _inproj_long(x_prompt, mprow, mclast, inproj_w, ROW_TILE)
    (pq, pk, pkb, pv, pvb, plf, pcq, pck, pr, plw, pkr, pvr, pa, pg, pxl, _, _) = po
    att_p = _fox_prompt(pq, pcq, mkb, mck, mvb, pkb, pck, pvb)
    rw_p, wkv_prompt = _wkv(pr, plw, pkr, pvr, pa, pg, s_meta, wkv_p, WKV_CHUNK, WKV_PROMPT_SEQS)
    y_prompt = _out_ffn(x_prompt.reshape(batch * seq, D_MODEL), att_p.reshape(batch * seq, WIDTH),
                        rw_p.reshape(batch * seq, WIDTH), ffn_w).reshape(batch, seq, D_MODEL)

    rows_s = n_seq * n_new
    xprev = jnp.repeat(state_shift[0], n_new, axis=0)[None]
    so = _inproj_packed(x_sample.reshape(1, rows_s, D_MODEL), xprev, inproj_w, n_new)
    (sq, sk, skb, sv, svb, slf, scq, sck, sr, slw, skr, svr, sa, sg, sxn) = so
    seqs = lambda t: t.reshape(n_seq, n_new, t.shape[-1])
    att_s = _fox_sample(page_table, seqs(sq), seqs(scq), seqs(skb), seqs(sck), seqs(svb),
                        jnp.transpose(cache_k[0], (0, 2, 3, 1)), jnp.transpose(cache_v[0], (0, 2, 3, 1)),
                        jnp.transpose(cache_logf[0], (0, 2, 1)))
    rw_s, wkv_sample = _wkv(seqs(sr), seqs(slw), seqs(skr), seqs(svr), seqs(sa), seqs(sg),
                            state_wkv[0], wkv_p, n_new, WKV_SAMPLE_SEQS)
    y_sample = _out_ffn(x_sample.reshape(rows_s, D_MODEL), att_s.reshape(rows_s, WIDTH),
                        rw_s.reshape(rows_s, WIDTH), ffn_w).reshape(n_seq, n_new, D_MODEL)

    def with_meta(meta, main):
        return jnp.concatenate([jnp.broadcast_to(meta, (batch,) + meta.shape[1:]), main], axis=1)

    heads = lambda t: t.reshape(t.shape[:-1] + (N_HEADS, HEAD_DIM))
    k_prompt = heads(with_meta(mk, pk))[None]
    v_prompt = heads(with_meta(mv, pv))[None]
    logf_prompt = jnp.concatenate([jnp.broadcast_to(mlf, (batch,) + mlf.shape[1:]), plf], axis=2)
    logf_prompt = jnp.transpose(logf_prompt, (0, 2, 1))[None]
    shift_prompt = pxl.reshape(1, batch, D_MODEL)
    k_sample = heads(seqs(sk))[None]
    v_sample = heads(seqs(sv))[None]
    logf_sample = jnp.transpose(slf.reshape(N_HEADS, n_seq, n_new), (1, 2, 0))[None]
    shift_sample = seqs(sxn)[:, -1][None]
    return (y_prompt, y_sample, k_prompt, v_prompt, logf_prompt, wkv_prompt[None], shift_prompt,
            k_sample, v_sample, logf_sample, wkv_sample[None], shift_sample)
```

```python
import functools

import jax
import jax.numpy as jnp
from jax import lax
from jax.experimental import pallas as pl
from jax.experimental.pallas import tpu as pltpu

D_MODEL = 1024
N_META = 16
HEAD_DIM = 64
N_HEADS = 8
WIDTH = N_HEADS * HEAD_DIM
PAGE_SIZE = 128
DECAY_LORA = 64
ICLR_LORA = 64
GATE_LORA = 128
RWKV_COLS = 3 * WIDTH + DECAY_LORA + ICLR_LORA + GATE_LORA
RMS_EPS = 1e-6
GN_EPS = 64e-5
NEG_BIG = -1e30

LANES = 128
BIAS_GROUP = 16
BIAS_PARTS = 3
HEAD_SHIFT = HEAD_DIM.bit_length() - 1
GROUP_SHIFT = BIAS_GROUP.bit_length() - 1
VMEM_LIMIT = 56 * 1024 * 1024

ROW_TILE = 512
FFN_ROW_TILE = 512
ATT_TILE = 2048
ATT_K_TILE = 512
ATT_SUB = 256
ATT_GROUP = 16
WKV_CHUNK = 64
WKV_PROMPT_SEQS = 8
WKV_SAMPLE_SEQS = 4
PAGES_PER_STEP = 32
FF_CHUNK = 256

BF16 = jnp.bfloat16
F32 = jnp.float32


def _dot(a, b):
    return jnp.dot(a, b, preferred_element_type=F32)


def _dot_nt(a, b):
    return lax.dot_general(a, b, (((1,), (1,)), ((), ())), preferred_element_type=F32)


def _dot_tn(a, b):
    return lax.dot_general(a, b, (((0,), (0,)), ((), ())), preferred_element_type=F32)


def _rms(x, g):
    return x * lax.rsqrt(jnp.mean(x * x, axis=-1, keepdims=True) + RMS_EPS) * g


def _softplus(z):
    return jnp.maximum(z, 0.0) + jnp.log(1.0 + jnp.exp(-jnp.abs(z)))


def _sigmoid(z):
    return 1.0 / (1.0 + jnp.exp(-z))


def _mask_bf16(cond):
    return cond.astype(F32).astype(BF16)


def _split3(c):
    hi = c.astype(BF16)
    r1 = c - hi.astype(F32)
    mid = r1.astype(BF16)
    lo = (r1 - mid.astype(F32)).astype(BF16)
    return hi, mid, lo


def _select_sum(sel, x):
    n = x.shape[1]
    out = _dot(sel.astype(BF16), jnp.concatenate(_split3(x), axis=1))
    return out[:, :n] + out[:, n:2 * n] + out[:, 2 * n:]


def _sum_select(x, sel):
    m = x.shape[0]
    out = _dot(jnp.concatenate(_split3(x), axis=0), sel.astype(BF16))
    return out[:m] + out[m:2 * m] + out[2 * m:]


def _inproj_outputs(xn, qkv, logf, c, p, p_prev, w, outs):
    (mu_ref, sel_ref, ones_ref, w2_ref, a2_ref, g2_ref, w0_ref, a0_ref) = w
    (q_ref, k_ref, kb_ref, v_ref, vb_ref, lf_ref, cq_ref, ck_ref,
     r_ref, lw_ref, kr_ref, vr_ref, a_ref, g_ref) = outs
    q_ref[0] = (qkv[:, :WIDTH] * (HEAD_DIM ** -0.5)).astype(BF16)
    k = qkv[:, WIDTH:2 * WIDTH]
    k_ref[0] = k.T
    kb_ref[0] = k.astype(BF16)
    v = qkv[:, 2 * WIDTH:]
    v_ref[0] = v.T
    vb_ref[0] = v.astype(BF16)
    lf_ref[0] = logf.T[:N_HEADS, :]
    cparts = jnp.concatenate(_split3(c), axis=1)
    cc = _dot(cparts, sel_ref[...]) + ones_ref[...]
    cq_ref[0] = cc[:, :LANES].astype(BF16)
    ck_ref[0] = cc[:, LANES:].astype(BF16)
    rw = p + (p_prev - p) * mu_ref[...]
    r_ref[0] = rw[:, :WIDTH]
    kr_ref[0] = rw[:, WIDTH:2 * WIDTH]
    vr_ref[0] = rw[:, 2 * WIDTH:3 * WIDTH]
    z = rw[:, 3 * WIDTH:3 * WIDTH + LANES]
    gl = rw[:, 3 * WIDTH + LANES:]
    w_log = -_softplus(-(w0_ref[...] + _dot(jnp.tanh(z).astype(BF16), w2_ref[...]))) - 0.5
    lw_ref[0] = -jnp.exp(w_log)
    a_ref[0] = _sigmoid(a0_ref[...] + _dot(z.astype(BF16), a2_ref[...]))
    g_ref[0] = _dot(_sigmoid(gl).astype(BF16), g2_ref[...])


def _inproj_long_kernel(x_ref, prow_ref, c0_ref, ln_ref, wqkv_ref, wf_ref, bf_ref, wrw_ref,
                        mu_ref, sel_ref, ones_ref, w2_ref, a2_ref, g2_ref, w0_ref, a0_ref,
                        q_ref, k_ref, kb_ref, v_ref, vb_ref, lf_ref, cq_ref, ck_ref,
                        r_ref, lw_ref, kr_ref, vr_ref, a_ref, g_ref, xl_ref, pl_ref, cl_ref,
                        pcar_ref, ccar_ref):
    @pl.when(pl.program_id(1) == 0)
    def _():
        pcar_ref[...] = prow_ref[0]
        ccar_ref[...] = c0_ref[0]

    rows = x_ref.shape[1]
    xn = _rms(x_ref[0], ln_ref[...])
    xb = xn.astype(BF16)
    qkv = _dot(xb, wqkv_ref[...])
    logf = -_softplus(-(_dot(xb, wf_ref[...]) + bf_ref[...]))
    ri = lax.broadcasted_iota(jnp.int32, (rows, rows), 0)
    ci = lax.broadcasted_iota(jnp.int32, (rows, rows), 1)
    c = _select_sum((ri >= ci).astype(F32), logf) + ccar_ref[...]
    ccar_ref[...] = c[rows - 1:rows, :]
    p = _dot(xb, wrw_ref[...])
    first = lax.broadcasted_iota(jnp.int32, p.shape, 0) == 0
    p_prev = jnp.where(first, pcar_ref[...], pltpu.roll(p, 1, 0))
    pcar_ref[...] = p[rows - 1:rows, :]
    xl_ref[0] = xn[rows - 1:rows, :]
    pl_ref[0] = p[rows - 1:rows, :]
    cl_ref[0] = c[rows - 1:rows, :]
    _inproj_outputs(xn, qkv, logf, c, p, p_prev,
                    (mu_ref, sel_ref, ones_ref, w2_ref, a2_ref, g2_ref, w0_ref, a0_ref),
                    (q_ref, k_ref, kb_ref, v_ref, vb_ref, lf_ref, cq_ref, ck_ref,
                     r_ref, lw_ref, kr_ref, vr_ref, a_ref, g_ref))


def _inproj_packed_kernel(x_ref, xprev_ref, ln_ref, wqkv_ref, wf_ref, bf_ref, wrw_ref,
                          mu_ref, sel_ref, ones_ref, w2_ref, a2_ref, g2_ref, w0_ref, a0_ref,
                          q_ref, k_ref, kb_ref, v_ref, vb_ref, lf_ref, cq_ref, ck_ref,
                          r_ref, lw_ref, kr_ref, vr_ref, a_ref, g_ref, xn_ref, *, seq_len):
    rows = x_ref.shape[1]
    shift = seq_len.bit_length() - 1
    xn = _rms(x_ref[0], ln_ref[...])
    xn_ref[0] = xn
    xb = xn.astype(BF16)
    qkv = _dot(xb, wqkv_ref[...])
    logf = -_softplus(-(_dot(xb, wf_ref[...]) + bf_ref[...]))
    ri = lax.broadcasted_iota(jnp.int32, (rows, rows), 0)
    ci = lax.broadcasted_iota(jnp.int32, (rows, rows), 1)
    same_seq = (ri >> shift) == (ci >> shift)
    c = _select_sum(((ri >= ci) & same_seq).astype(F32), logf)
    p = _dot(xb, wrw_ref[...])
    p_first = _dot(xprev_ref[0].astype(BF16), wrw_ref[...])
    first = (lax.broadcasted_iota(jnp.int32, p.shape, 0) & (seq_len - 1)) == 0
    p_prev = jnp.where(first, p_first, pltpu.roll(p, 1, 0))
    _inproj_outputs(xn, qkv, logf, c, p, p_prev,
                    (mu_ref, sel_ref, ones_ref, w2_ref, a2_ref, g2_ref, w0_ref, a0_ref),
                    (q_ref, k_ref, kb_ref, v_ref, vb_ref, lf_ref, cq_ref, ck_ref,
                     r_ref, lw_ref, kr_ref, vr_ref, a_ref, g_ref))


def _const_spec(arr):
    return pl.BlockSpec(arr.shape, lambda *_: (0,) * arr.ndim)


def _inproj_out_shapes(batch, rows_total):
    def s(width, dtype):
        return jax.ShapeDtypeStruct((batch, rows_total, width), dtype)
    def t(height):
        return jax.ShapeDtypeStruct((batch, height, rows_total), F32)
    return [s(WIDTH, BF16), t(WIDTH), s(WIDTH, BF16), t(WIDTH), s(WIDTH, BF16),
            t(N_HEADS), s(LANES, BF16), s(LANES, BF16)] + [s(WIDTH, F32)] * 6


def _inproj_out_specs(tile):
    def s(width):
        return pl.BlockSpec((1, tile, width), lambda b, i: (b, i, 0))

    def t(height):
        return pl.BlockSpec((1, height, tile), lambda b, i: (b, 0, i))
    return [s(WIDTH), t(WIDTH), s(WIDTH), t(WIDTH), s(WIDTH),
            t(N_HEADS), s(LANES), s(LANES)] + [s(WIDTH)] * 6


def _inproj_long(x, prow, c0, weights, tile):
    batch, seq, _ = x.shape
    n_tiles = seq // tile
    row = lambda width: pl.BlockSpec((1, 1, width), lambda b, i: (b, 0, 0))
    shapes = _inproj_out_shapes(batch, seq) + [
        jax.ShapeDtypeStruct((batch, 1, D_MODEL), F32),
        jax.ShapeDtypeStruct((batch, 1, RWKV_COLS), F32),
        jax.ShapeDtypeStruct((batch, 1, LANES), F32)]
    specs = _inproj_out_specs(tile) + [row(D_MODEL), row(RWKV_COLS), row(LANES)]
    return pl.pallas_call(
        _inproj_long_kernel,
        grid=(batch, n_tiles),
        in_specs=[pl.BlockSpec((1, tile, D_MODEL), lambda b, i: (b, i, 0)),
                  _const_spec(prow), _const_spec(c0)] + [_const_spec(w) for w in weights],
        out_specs=specs,
        out_shape=shapes,
        scratch_shapes=[pltpu.VMEM((1, RWKV_COLS), F32), pltpu.VMEM((1, LANES), F32)],
        compiler_params=pltpu.CompilerParams(
            dimension_semantics=("arbitrary", "arbitrary"), vmem_limit_bytes=VMEM_LIMIT),
        name="inproj_long",
    )(x, prow, c0, *weights)


def _inproj_packed(x, xprev, weights, seq_len):
    _, rows, _ = x.shape
    full = pl.BlockSpec((1, rows, D_MODEL), lambda b, i: (0, 0, 0))
    return pl.pallas_call(
        functools.partial(_inproj_packed_kernel, seq_len=seq_len),
        grid=(1, 1),
        in_specs=[full, full] + [_const_spec(w) for w in weights],
        out_specs=_inproj_out_specs(rows) + [full],
        out_shape=_inproj_out_shapes(1, rows) + [jax.ShapeDtypeStruct((1, rows, D_MODEL), F32)],
        compiler_params=pltpu.CompilerParams(
            dimension_semantics=("arbitrary", "arbitrary"), vmem_limit_bytes=VMEM_LIMIT),
        name="inproj_packed",
    )(x, xprev, *weights)


def _fox_prompt_kernel(q_ref, cq_ref, km_ref, ckm_ref, vm_ref, k_ref, ck_ref, v_ref, o_ref):
    pair = pl.program_id(1)
    qi = pl.program_id(2)
    tile = q_ref.shape[1]
    lane = lax.broadcasted_iota(jnp.int32, (1, LANES), 1)
    q2 = q_ref[0]
    cq = cq_ref[0]
    n_sub = tile // ATT_SUB
    lhs = []
    for hh in range(2):
        head_lanes = (lane >= hh * HEAD_DIM) & (lane < (hh + 1) * HEAD_DIM)
        g0 = (2 * pair + hh) * BIAS_GROUP
        group_lanes = (lane >= g0) & (lane < g0 + BIAS_GROUP)
        full = jnp.concatenate([q2 * _mask_bf16(head_lanes), cq * _mask_bf16(group_lanes)], axis=1)
        lhs.append([full[r * ATT_SUB:(r + 1) * ATT_SUB, :] for r in range(n_sub)])
    query_minus_key = (lax.broadcasted_iota(jnp.int32, (ATT_K_TILE, ATT_SUB), 1)
                       - lax.broadcasted_iota(jnp.int32, (ATT_K_TILE, ATT_SUB), 0))

    def block(carry, kk, vv, first_col):
        out = list(carry)
        live = [(hh, r) for hh in range(2) for r in range(n_sub)
                if first_col is None or first_col <= (r + 1) * ATT_SUB - 1]
        for g0 in range(0, len(live), ATT_GROUP):
            group = live[g0:g0 + ATT_GROUP]
            scores, m_news = [], []
            for hh, r in group:
                s = _dot_nt(kk, lhs[hh][r])
                if first_col is not None and first_col + kk.shape[0] - 1 > r * ATT_SUB:
                    s = jnp.where(query_minus_key >= first_col - r * ATT_SUB, s, NEG_BIG)
                scores.append(s)
            for (hh, r), s in zip(group, scores):
                m_news.append(jnp.maximum(out[hh * n_sub + r][0], jnp.max(s, axis=0, keepdims=True)))
            for (hh, r), s, m_new in zip(group, scores, m_news):
                m, l, acc = out[hh * n_sub + r]
                alpha = jnp.exp(m - m_new)
                pe = jnp.exp(s - m_new)
                l_new = alpha * l + jnp.sum(pe, axis=0, keepdims=True)
                acc_new = alpha * acc + _dot_tn(vv, pe.astype(BF16))
                out[hh * n_sub + r] = (m_new, l_new, acc_new)
        return tuple(out)

    init = tuple((jnp.full((1, ATT_SUB), NEG_BIG, F32), jnp.zeros((1, ATT_SUB), F32),
                  jnp.zeros((LANES, ATT_SUB), F32)) for _ in range(2 * n_sub))
    carry = block(init, jnp.concatenate([km_ref[0], ckm_ref[0]], axis=1), vm_ref[0], None)

    def keys(j):
        start = pl.multiple_of(j * ATT_K_TILE, ATT_K_TILE)
        rows = pl.ds(start, ATT_K_TILE)
        return jnp.concatenate([k_ref[0, rows, :], ck_ref[0, rows, :]], axis=1), v_ref[0, rows, :]

    per_q = tile // ATT_K_TILE
    carry = lax.fori_loop(0, qi * per_q, lambda j, c: block(c, *keys(j), None), carry)
    for d in range(per_q):
        carry = block(carry, *keys(qi * per_q + d), d * ATT_K_TILE)
    first_head = lax.broadcasted_iota(jnp.int32, (LANES, ATT_SUB), 0) < HEAD_DIM
    for r in range(n_sub):
        (_, l0, acc0), (_, l1, acc1) = carry[r], carry[n_sub + r]
        o_t = jnp.where(first_head, acc0 / l0, acc1 / l1)
        o_ref[0, r * ATT_SUB:(r + 1) * ATT_SUB, :] = o_t.T.astype(BF16)


def _fox_prompt(q, cq, k_meta, ck_meta, v_meta, k, ck, v):
    batch, seq, _ = q.shape
    n_pairs = N_HEADS // 2
    return pl.pallas_call(
        _fox_prompt_kernel,
        grid=(batch, n_pairs, seq // ATT_TILE),
        in_specs=[
            pl.BlockSpec((1, ATT_TILE, LANES), lambda b, p, i: (b, i, p)),
            pl.BlockSpec((1, ATT_TILE, LANES), lambda b, p, i: (b, i, 0)),
            pl.BlockSpec((1, N_META, LANES), lambda b, p, i: (0, 0, p)),
            pl.BlockSpec((1, N_META, LANES), lambda b, p, i: (0, 0, 0)),
            pl.BlockSpec((1, N_META, LANES), lambda b, p, i: (0, 0, p)),
            pl.BlockSpec((1, seq, LANES), lambda b, p, i: (b, 0, p)),
            pl.BlockSpec((1, seq, LANES), lambda b, p, i: (b, 0, 0)),
            pl.BlockSpec((1, seq, LANES), lambda b, p, i: (b, 0, p)),
        ],
        out_specs=pl.BlockSpec((1, ATT_TILE, LANES), lambda b, p, i: (b, i, p)),
        out_shape=jax.ShapeDtypeStruct((batch, seq, WIDTH), BF16),
        compiler_params=pltpu.CompilerParams(
            dimension_semantics=("arbitrary", "arbitrary", "arbitrary"),
            vmem_limit_bytes=VMEM_LIMIT),
        name="fox_prompt",
    )(q, cq, k_meta, ck_meta, v_meta, k, ck, v)


def _fox_sample_kernel(pt_ref, q_ref, cq_ref, kn_ref, ckn_ref, vn_ref, *rest, n_new):
    del pt_ref
    g_pages = PAGES_PER_STEP
    k_refs, v_refs, lf_refs = rest[:g_pages], rest[g_pages:2 * g_pages], rest[2 * g_pages:3 * g_pages]
    o_ref, q_sc, cqm_sc, roff_sc, carry_sc, m_sc, l_sc, acc_sc = rest[3 * g_pages:]
    step = pl.program_id(1)
    rows = N_HEADS * n_new
    row_head = lax.broadcasted_iota(jnp.int32, (rows, 1), 0) & (N_HEADS - 1)

    def per_head_rows(x):
        return jnp.concatenate([jnp.broadcast_to(x[i:i + 1, :], (N_HEADS, x.shape[1]))
                                for i in range(n_new)], axis=0)

    @pl.when(step == 0)
    def _():
        q = per_head_rows(q_ref[0].astype(F32))
        lane_head = lax.broadcasted_iota(jnp.int32, (1, WIDTH), 1) >> HEAD_SHIFT
        q_sc[...] = jnp.where(row_head == lane_head, q, 0.0).astype(BF16)
        cq = per_head_rows(cq_ref[0].astype(F32))
        lane = lax.broadcasted_iota(jnp.int32, (1, LANES), 1)
        cqm = jnp.where((lane >> GROUP_SHIFT) == row_head, cq, 0.0)
        cqm_sc[...] = cqm.astype(BF16)
        slot = lax.broadcasted_iota(jnp.int32, cqm.shape, 1) & (BIAS_GROUP - 1)
        roff_sc[...] = jnp.sum(jnp.where(slot < BIAS_PARTS, cqm, 0.0), axis=-1, keepdims=True)
        carry_sc[...] = jnp.zeros(carry_sc.shape, F32)
        m_sc[...] = jnp.full(m_sc.shape, NEG_BIG, F32)
        l_sc[...] = jnp.zeros(l_sc.shape, F32)
        acc_sc[...] = jnp.zeros(acc_sc.shape, F32)

    q_bd = q_sc[...]
    ti = lax.broadcasted_iota(jnp.int32, (PAGE_SIZE, PAGE_SIZE), 0)
    tj = lax.broadcasted_iota(jnp.int32, (PAGE_SIZE, PAGE_SIZE), 1)
    later_and_all = jnp.concatenate([(ti > tj).astype(F32), jnp.ones((PAGE_SIZE, PAGE_SIZE), F32)],
                                    axis=1)
    lf_all = jnp.concatenate([lf_refs[g][0] for g in range(g_pages)], axis=0)
    sums = _sum_select(lf_all, later_and_all)

    carry = carry_sc[...]
    roff = roff_sc[...]
    def pages_t(refs, g0):
        return jnp.concatenate([refs[g][0].reshape(WIDTH, PAGE_SIZE) for g in (g0, g0 + 1)],
                               axis=1).astype(BF16)

    scores = []
    for g0 in range(0, g_pages, 2):
        bias = []
        for g in (g0, g0 + 1):
            page_sums = sums[g * N_HEADS:(g + 1) * N_HEADS, :]
            bias.append(jnp.concatenate([page_sums[:, :PAGE_SIZE] + carry] * n_new, axis=0))
            carry = carry + page_sums[:, PAGE_SIZE:]
        scores.append(_dot(q_bd, pages_t(k_refs, g0)) + jnp.concatenate(bias, axis=1) + roff)
    carry_sc[...] = carry

    m = m_sc[...]
    m_new = m
    for s in scores:
        m_new = jnp.maximum(m_new, jnp.max(s, axis=-1, keepdims=True))
    alpha = jnp.exp(m - m_new)
    l_new = alpha * l_sc[...]
    acc = alpha * acc_sc[...]
    for i, s in enumerate(scores):
        pe = jnp.exp(s - m_new)
        l_new = l_new + jnp.sum(pe, axis=-1, keepdims=True)
        acc = acc + _dot_nt(pe.astype(BF16), pages_t(v_refs, 2 * i))
    m_sc[...] = m_new
    l_sc[...] = l_new
    acc_sc[...] = acc

    @pl.when(step == pl.num_programs(1) - 1)
    def _():
        s = _dot_nt(q_bd, kn_ref[0]) + _dot_nt(cqm_sc[...], ckn_ref[0])
        row_tok = lax.broadcasted_iota(jnp.int32, (rows, 1), 0) >> (N_HEADS.bit_length() - 1)
        s = jnp.where(lax.broadcasted_iota(jnp.int32, (1, n_new), 1) <= row_tok, s, NEG_BIG)
        m_fin = jnp.maximum(m_new, jnp.max(s, axis=-1, keepdims=True))
        a_fin = jnp.exp(m_new - m_fin)
        pe = jnp.exp(s - m_fin)
        l_fin = a_fin * l_new + jnp.sum(pe, axis=-1, keepdims=True)
        out = (a_fin * acc + _dot(pe.astype(BF16), vn_ref[0])) / l_fin
        own = (lax.broadcasted_iota(jnp.int32, (N_HEADS, WIDTH), 0)
               == lax.broadcasted_iota(jnp.int32, (N_HEADS, WIDTH), 1) >> HEAD_SHIFT)
        o_ref[0] = jnp.concatenate(
            [jnp.sum(jnp.where(own, out[i * N_HEADS:(i + 1) * N_HEADS, :], 0.0), axis=0, keepdims=True)
             for i in range(n_new)], axis=0).astype(BF16)


def _fox_sample(page_table, q, cq, k_new, ck_new, v_new, cache_k, cache_v, cache_logf):
    n_seq, n_new, _ = q.shape
    n_pages = page_table.shape[1]
    g_pages = PAGES_PER_STEP
    rows = N_HEADS * n_new

    def per_seq(width):
        return pl.BlockSpec((1, n_new, width), lambda b, j, pt: (b, 0, 0))

    def page_spec(g, arr):
        block = (1,) + arr.shape[1:]
        return pl.BlockSpec(block, lambda b, j, pt: (pt[b * n_pages + j * g_pages + g],)
                            + (0,) * (arr.ndim - 1))

    visit_order = page_table[:, ::-1].reshape(-1)

    grid_spec = pltpu.PrefetchScalarGridSpec(
        num_scalar_prefetch=1,
        grid=(n_seq, n_pages // g_pages),
        in_specs=([per_seq(WIDTH), per_seq(LANES), per_seq(WIDTH), per_seq(LANES), per_seq(WIDTH)]
                  + [page_spec(g, cache_k) for g in range(g_pages)]
                  + [page_spec(g, cache_v) for g in range(g_pages)]
                  + [page_spec(g, cache_logf) for g in range(g_pages)]),
        out_specs=pl.BlockSpec((1, n_new, WIDTH), lambda b, j, pt: (b, 0, 0)),
        scratch_shapes=[pltpu.VMEM((rows, WIDTH), BF16), pltpu.VMEM((rows, LANES), BF16),
                        pltpu.VMEM((rows, 1), F32), pltpu.VMEM((N_HEADS, PAGE_SIZE), F32),
                        pltpu.VMEM((rows, 1), F32), pltpu.VMEM((rows, 1), F32),
                        pltpu.VMEM((rows, WIDTH), F32)],
    )
    return pl.pallas_call(
        functools.partial(_fox_sample_kernel, n_new=n_new),
        grid_spec=grid_spec,
        out_shape=jax.ShapeDtypeStruct((n_seq, n_new, WIDTH), BF16),
        compiler_params=pltpu.CompilerParams(
            dimension_semantics=("arbitrary", "arbitrary"), vmem_limit_bytes=VMEM_LIMIT),
        name="fox_sample",
    )(visit_order, q, cq, k_new, ck_new, v_new,
      *([cache_k] * g_pages), *([cache_v] * g_pages), *([cache_logf] * g_pages))


def _wkv_kernel(r_ref, lw_ref, kr_ref, vr_ref, a_ref, g_ref, s0_ref,
                kk_ref, ka_ref, rk_ref, gg_ref, gb_ref, o_ref, so_ref, s_sc):
    nb, C, _ = r_ref.shape
    n_pairs = N_HEADS // 2
    units = [(bb, pair) for bb in range(nb) for pair in range(n_pairs)]
    lo_state = ((lax.broadcasted_iota(jnp.int32, (LANES, LANES), 0) < HEAD_DIM)
                == (lax.broadcasted_iota(jnp.int32, (LANES, LANES), 1) < HEAD_DIM))

    @pl.when(pl.program_id(1) == 0)
    def _():
        zeros = jnp.zeros((HEAD_DIM, HEAD_DIM), F32)
        for bb, pair in units:
            sb = bb if s0_ref.shape[0] == nb else 0
            top = jnp.concatenate([s0_ref[sb, 2 * pair], zeros], axis=1)
            bottom = jnp.concatenate([zeros, s0_ref[sb, 2 * pair + 1]], axis=1)
            s_sc[bb, pair] = jnp.concatenate([top, bottom], axis=0)

    ri = lax.broadcasted_iota(jnp.int32, (C, C), 0)
    ci = lax.broadcasted_iota(jnp.int32, (C, C), 1)
    incl = ri >= ci
    strict = ri > ci
    eye = (ri == ci).astype(F32)
    lo = lax.broadcasted_iota(jnp.int32, (C, LANES), 1) < HEAD_DIM
    lo2 = lax.broadcasted_iota(jnp.int32, (2 * C, LANES), 1) < HEAD_DIM
    levels = max(1, (C - 1).bit_length())
    lcum = [_select_sum(incl.astype(F32), lw_ref[bb]) for bb in range(nb)]

    def per_head(x):
        first = jnp.sum(jnp.where(lo, x, 0.0), axis=-1, keepdims=True)
        second = jnp.sum(jnp.where(lo, 0.0, x), axis=-1, keepdims=True)
        return jnp.where(lo, first, second)

    xs, x_all, bts, kts, bgs, kgs, vbs, kps, e_ends = [], [], [], [], [], [], [], [], []
    for bb, pair in units:
        lanes = slice(pair * LANES, (pair + 1) * LANES)
        r2, lw2, kr2, a2 = (ref[bb, :, lanes] for ref in (r_ref, lw_ref, kr_ref, a_ref))
        lc = lcum[bb][:, lanes]
        kkr = kr2 * kk_ref[:, lanes]
        kk = kkr / jnp.maximum(jnp.sqrt(per_head(kkr * kkr)), 1e-12)
        kp = kr2 * (1.0 + (a2 - 1.0) * ka_ref[:, lanes])
        bv = kk * a2
        e_neg = jnp.exp(-lc)
        l_end = lc[C - 1:C, :]
        e_rem = jnp.exp(l_end - lc)
        x2 = jnp.concatenate([-kk * jnp.exp(lc - lw2), r2 * jnp.exp(lc)], axis=0)
        x_all.append(x2.astype(BF16))
        xs.append((jnp.where(lo2, x2, 0.0).astype(BF16), jnp.where(lo2, 0.0, x2).astype(BF16)))
        bts.append((bv * e_neg).astype(BF16))
        kts.append((kp * e_neg).astype(BF16))
        bgs.append((bv * e_rem).astype(BF16))
        kgs.append((kp * e_rem).astype(BF16))
        vbs.append(vr_ref[bb, :, lanes].astype(BF16))
        kps.append(kp)
        e_ends.append(jnp.exp(l_end))

    chains = [(u, hh) for u in range(len(units)) for hh in range(2)]
    sc_b = [_dot_nt(xs[u][hh], bts[u]) for u, hh in chains]
    sc_k = [_dot_nt(xs[u][hh], kts[u]) for u, hh in chains]
    n_ch = range(len(chains))
    a_ab = [jnp.where(strict, sc_b[i][:C], 0.0) for i in n_ch]
    a_ak = [jnp.where(strict, sc_k[i][:C], 0.0).astype(BF16) for i in n_ch]
    a_rb = [jnp.where(incl, sc_b[i][C:], 0.0).astype(BF16) for i in n_ch]
    a_rk = [jnp.where(incl, sc_k[i][C:], 0.0).astype(BF16) for i in n_ch]

    tinv = [eye + a_ab[i] for i in n_ch]
    if levels > 1:
        lb = [a_ab[i].astype(BF16) for i in n_ch]
        lpow = [_dot(lb[i], lb[i]) for i in n_ch]
        for level in range(1, levels):
            qb = [lpow[i].astype(BF16) for i in n_ch]
            if level == levels - 1:
                tinv = [tinv[i] + _dot(tinv[i].astype(BF16), qb[i]) for i in n_ch]
            else:
                both = [_dot(jnp.concatenate([tinv[i], lpow[i]], axis=0).astype(BF16), qb[i])
                        for i in n_ch]
                tinv = [tinv[i] + both[i][:C] for i in n_ch]
                lpow = [both[i][C:] for i in n_ch]
    tb = [tinv[i].astype(BF16) for i in n_ch]

    def both_heads(u, f):
        return jnp.where(lo, f(2 * u), f(2 * u + 1))

    n_u = range(len(units))
    s_old = [s_sc[bb, pair] for bb, pair in units]
    ps = [_dot_nt(x_all[u], s_old[u].astype(BF16)) for u in n_u]
    w1 = [(ps[u][:C] + both_heads(u, lambda i, u=u: _dot(a_ak[i], vbs[u]))).astype(BF16) for u in n_u]
    ub = [both_heads(u, lambda i, u=u: _dot(tb[i], w1[u])).astype(BF16) for u in n_u]
    ys = [ps[u][C:] + both_heads(u, lambda i, u=u: _dot(a_rb[i], ub[u]) + _dot(a_rk[i], vbs[u]))
          for u in n_u]
    for u, (bb, pair) in enumerate(units):
        cross = _dot_tn(ub[u], bgs[u]) + _dot_tn(vbs[u], kgs[u])
        s_sc[bb, pair] = s_old[u] * e_ends[u] + jnp.where(lo_state, cross, 0.0)

    for u, (bb, pair) in enumerate(units):
        lanes = slice(pair * LANES, (pair + 1) * LANES)
        y = ys[u]
        yc = y - per_head(y) * (1.0 / HEAD_DIM)
        var = per_head(yc * yc) * (1.0 / HEAD_DIM)
        yn = yc * lax.rsqrt(var + GN_EPS) * gg_ref[:, lanes] + gb_ref[:, lanes]
        bonus = per_head(r_ref[bb, :, lanes] * kps[u] * rk_ref[:, lanes]) * vr_ref[bb, :, lanes]
        o_ref[bb, :, lanes] = ((yn + bonus) * g_ref[bb, :, lanes]).astype(BF16)

    @pl.when(pl.program_id(1) == pl.num_programs(1) - 1)
    def _():
        for bb, pair in units:
            s2 = s_sc[bb, pair]
            so_ref[bb, 2 * pair] = s2[:HEAD_DIM, :HEAD_DIM]
            so_ref[bb, 2 * pair + 1] = pltpu.roll(s2[HEAD_DIM:, :], HEAD_DIM, 1)[:, :HEAD_DIM]


def _wkv(r, lw, kr, vr, a, g, s0, params, chunk, batch_tile):
    batch, seq, _ = r.shape
    assert chunk & (chunk - 1) == 0 and seq % chunk == 0 and batch % batch_tile == 0
    shared_state = s0.shape[0] == 1
    tok = pl.BlockSpec((batch_tile, chunk, WIDTH), lambda b, c: (b, c, 0))
    state_out = pl.BlockSpec((batch_tile, N_HEADS, HEAD_DIM, HEAD_DIM), lambda b, c: (b, 0, 0, 0))
    state_in = (pl.BlockSpec((1, N_HEADS, HEAD_DIM, HEAD_DIM), lambda b, c: (0, 0, 0, 0))
                if shared_state else state_out)
    return pl.pallas_call(
        _wkv_kernel,
        grid=(batch // batch_tile, seq // chunk),
        in_specs=[tok] * 6 + [state_in] + [_const_spec(p) for p in params],
        out_specs=[tok, state_out],
        out_shape=[jax.ShapeDtypeStruct((batch, seq, WIDTH), BF16),
                   jax.ShapeDtypeStruct((batch, N_HEADS, HEAD_DIM, HEAD_DIM), F32)],
        scratch_shapes=[pltpu.VMEM((batch_tile, N_HEADS // 2, LANES, LANES), F32)],
        compiler_params=pltpu.CompilerParams(
            dimension_semantics=("arbitrary", "arbitrary"), vmem_limit_bytes=VMEM_LIMIT),
        name="wkv",
    )(r, lw, kr, vr, a, g, s0, *params)


def _out_ffn_kernel(x_ref, att_ref, rw_ref, woa_ref, wor_ref, ln2_ref, wg_ref, wu_ref, wd_ref,
                    lnf_ref, y_ref):
    h = x_ref[...] + _dot(att_ref[...], woa_ref[...]) + _dot(rw_ref[...], wor_ref[...])
    hb = _rms(h, ln2_ref[...]).astype(BF16)
    ffn = jnp.zeros(h.shape, F32)
    d_ff = wg_ref.shape[1]
    for c0 in range(0, d_ff, FF_CHUNK):
        gate = _dot(hb, wg_ref[:, c0:c0 + FF_CHUNK])
        up = _dot(hb, wu_ref[:, c0:c0 + FF_CHUNK])
        act = gate * _sigmoid(gate) * up
        ffn = ffn + _dot(act.astype(BF16), wd_ref[c0:c0 + FF_CHUNK, :])
    y_ref[...] = _rms(h + ffn, lnf_ref[...])


def _out_ffn(x, att, rw, weights):
    rows = x.shape[0]
    tile = min(FFN_ROW_TILE, rows)
    tok = lambda width: pl.BlockSpec((tile, width), lambda i: (i, 0))
    resident = lambda w: pl.BlockSpec(w.shape, lambda i: (0,) * w.ndim, pipeline_mode=pl.Buffered(1))
    return pl.pallas_call(
        _out_ffn_kernel,
        grid=(rows // tile,),
        in_specs=[tok(D_MODEL), tok(WIDTH), tok(WIDTH)] + [resident(w) for w in weights],
        out_specs=tok(D_MODEL),
        out_shape=jax.ShapeDtypeStruct((rows, D_MODEL), F32),
        compiler_params=pltpu.CompilerParams(
            dimension_semantics=("arbitrary",), vmem_limit_bytes=VMEM_LIMIT),
        name="out_ffn",
    )(x, att, rw, *weights)


def _bias_selectors():
    part = jnp.arange(BIAS_PARTS)[:, None, None]
    src = jnp.arange(LANES)[None, :, None]
    dst = jnp.arange(LANES)[None, None, :]
    is_head = src < N_HEADS
    q_sel = (is_head & (dst == src * BIAS_GROUP + part)).astype(F32)
    k_sel = -(is_head & (dst == src * BIAS_GROUP + BIAS_PARTS + part)).astype(F32)
    sel = jnp.concatenate([q_sel, k_sel], axis=2).reshape(BIAS_PARTS * LANES, 2 * LANES)
    slot = jnp.arange(LANES) % BIAS_GROUP
    ones = jnp.concatenate([(slot >= BIAS_PARTS) & (slot < 2 * BIAS_PARTS),
                            slot < BIAS_PARTS]).astype(F32)[None, :]
    return sel.astype(BF16), ones


def kernel(x_prompt, x_sample, cache_k, cache_v, cache_logf, state_wkv, state_shift, page_table,
           meta_tokens, ln1_g, w_in, b_f, mu_shift, w0, w2, a0, a2, g2, k_k, k_a, r_k,
           gn_g, gn_b, w_out, ln2_g, w_gate, w_up, w_down, lnf_g):
    batch, seq, _ = x_prompt.shape
    n_seq, n_new, _ = x_sample.shape
    fox_cols = 3 * WIDTH + N_HEADS

    w_in0 = w_in[0]
    wqkv = w_in0[:, :3 * WIDTH].astype(BF16)
    wf = jnp.pad(w_in0[:, 3 * WIDTH:fox_cols], ((0, 0), (0, LANES - N_HEADS))).astype(BF16)
    bf = jnp.pad(b_f[0], (0, LANES - N_HEADS))[None, :]
    wrw = w_in0[:, fox_cols:].astype(BF16)
    sel, ones = _bias_selectors()
    w2p = jnp.concatenate([w2[0], jnp.zeros((ICLR_LORA, WIDTH), F32)], axis=0).astype(BF16)
    a2p = jnp.concatenate([jnp.zeros((DECAY_LORA, WIDTH), F32), a2[0]], axis=0).astype(BF16)
    inproj_w = (ln1_g[0][None, :], wqkv, wf, bf, wrw, mu_shift[0][None, :], sel, ones,
                w2p, a2p, g2[0].astype(BF16), w0[0][None, :], a0[0][None, :])
    wkv_p = tuple(t[0].reshape(1, WIDTH) for t in (k_k, k_a, r_k, gn_g, gn_b))
    ffn_w = (w_out[0][:WIDTH].astype(BF16), w_out[0][WIDTH:].astype(BF16), ln2_g[0][None, :],
             w_gate[0].astype(BF16), w_up[0].astype(BF16), w_down[0].astype(BF16), lnf_g[None, :])

    zrow = jnp.zeros((1, 1, RWKV_COLS), F32)
    zc = jnp.zeros((1, 1, LANES), F32)
    mo = _inproj_long(meta_tokens[None], zrow, zc, inproj_w, N_META)
    (_, mk, mkb, mv, mvb, mlf, _, mck, mr, mlw, mkr, mvr, ma, mg, _, mprow, mclast) = mo
    zstate = jnp.zeros((1, N_HEADS, HEAD_DIM, HEAD_DIM), F32)
    _, s_meta = _wkv(mr, mlw, mkr, mvr, ma, mg, zstate, wkv_p, N_META, 1)

    po = _inproj_long(x_prompt, mprow, mclast, inproj_w, ROW_TILE)
    (pq, pk, pkb, pv, pvb, plf, pcq, pck, pr, plw, pkr, pvr, pa, pg, pxl, _, _) = po
    att_p = _fox_prompt(pq, pcq, mkb, mck, mvb, pkb, pck, pvb)
    rw_p, wkv_prompt = _wkv(pr, plw, pkr, pvr, pa, pg, s_meta, wkv_p, WKV_CHUNK, WKV_PROMPT_SEQS)
    y_prompt = _out_ffn(x_prompt.reshape(batch * seq, D_MODEL), att_p.reshape(batch * seq, WIDTH),
                        rw_p.reshape(batch * seq, WIDTH), ffn_w).reshape(batch, seq, D_MODEL)

    rows_s = n_seq * n_new
    xprev = jnp.repeat(state_shift[0], n_new, axis=0)[None]
    so = _inproj_packed(x_sample.reshape(1, rows_s, D_MODEL), xprev, inproj_w, n_new)
    (sq, sk, skb, sv, svb, slf, scq, sck, sr, slw, skr, svr, sa, sg, sxn) = so
    seqs = lambda t: t.reshape(n_seq, n_new, t.shape[-1])
    att_s = _fox_sample(page_table, seqs(sq), seqs(scq), seqs(skb), seqs(sck), seqs(svb),
                        jnp.transpose(cache_k[0], (0, 2, 3, 1)), jnp.transpose(cache_v[0], (0, 2, 3, 1)),
                        jnp.transpose(cache_logf[0], (0, 2, 1)))
    rw_s, wkv_sample = _wkv(seqs(sr), seqs(slw), seqs(skr), seqs(svr), seqs(sa), seqs(sg),
                            state_wkv[0], wkv_p, n_new, WKV_SAMPLE_SEQS)
    y_sample = _out_ffn(x_sample.reshape(rows_s, D_MODEL), att_s.reshape(rows_s, WIDTH),
                        rw_s.reshape(rows_s, WIDTH), ffn_w).reshape(n_seq, n_new, D_MODEL)

    def with_meta(meta, main):
        return jnp.concatenate([jnp.broadcast_to(meta, (batch,) + meta.shape[1:]), main], axis=2)

    def prompt_heads(t):
        return jnp.transpose(t.reshape(batch, N_HEADS, HEAD_DIM, t.shape[-1]), (0, 3, 1, 2))[None]

    def sample_heads(t):
        return jnp.transpose(t.reshape(N_HEADS, HEAD_DIM, n_seq, n_new), (2, 3, 0, 1))[None]

    k_prompt = prompt_heads(with_meta(mk, pk))
    v_prompt = prompt_heads(with_meta(mv, pv))
    logf_prompt = jnp.transpose(with_meta(mlf, plf), (0, 2, 1))[None]
    shift_prompt = pxl.reshape(1, batch, D_MODEL)
    k_sample = sample_heads(sk)
    v_sample = sample_heads(sv)
    logf_sample = jnp.transpose(slf.reshape(N_HEADS, n_seq, n_new), (1, 2, 0))[None]
    shift_sample = seqs(sxn)[:, -1][None]
    return (y_prompt, y_sample, k_prompt, v_prompt, logf_prompt, wkv_prompt[None], shift_prompt,
            k_sample, v_sample, logf_sample, wkv_sample[None], shift_sample)
```

```python
import functools

import jax
import jax.numpy as jnp
from jax import lax
from jax.experimental import pallas as pl
from jax.experimental.pallas import tpu as pltpu

D_MODEL = 1024
N_META = 16
HEAD_DIM = 64
N_HEADS = 8
WIDTH = N_HEADS * HEAD_DIM
PAGE_SIZE = 128
DECAY_LORA = 64
ICLR_LORA = 64
GATE_LORA = 128
RWKV_COLS = 3 * WIDTH + DECAY_LORA + ICLR_LORA + GATE_LORA
RMS_EPS = 1e-6
GN_EPS = 64e-5
NEG_BIG = -1e30

LANES = 128
BIAS_GROUP = 16
BIAS_PARTS = 3
HEAD_SHIFT = HEAD_DIM.bit_length() - 1
GROUP_SHIFT = BIAS_GROUP.bit_length() - 1
VMEM_LIMIT = 56 * 1024 * 1024

ROW_TILE = 512
FFN_ROW_TILE = 512
ATT_TILE = 2048
ATT_K_TILE = 512
ATT_SUB = 256
ATT_GROUP = 16
WKV_CHUNK = 64
WKV_PROMPT_SEQS = 8
WKV_SAMPLE_SEQS = 4
PAGES_PER_STEP = 32
FF_CHUNK = 256

BF16 = jnp.bfloat16
F32 = jnp.float32


def _dot(a, b):
    return jnp.dot(a, b, preferred_element_type=F32)


def _dot_nt(a, b):
    return lax.dot_general(a, b, (((1,), (1,)), ((), ())), preferred_element_type=F32)


def _dot_tn(a, b):
    return lax.dot_general(a, b, (((0,), (0,)), ((), ())), preferred_element_type=F32)


def _rms(x, g):
    return x * lax.rsqrt(jnp.mean(x * x, axis=-1, keepdims=True) + RMS_EPS) * g


def _softplus(z):
    return jnp.maximum(z, 0.0) + jnp.log(1.0 + jnp.exp(-jnp.abs(z)))


def _sigmoid(z):
    return 1.0 / (1.0 + jnp.exp(-z))


def _mask_bf16(cond):
    return cond.astype(F32).astype(BF16)


def _split3(c):
    hi = c.astype(BF16)
    r1 = c - hi.astype(F32)
    mid = r1.astype(BF16)
    lo = (r1 - mid.astype(F32)).astype(BF16)
    return hi, mid, lo


def _select_sum(sel, x):
    n = x.shape[1]
    out = _dot(sel.astype(BF16), jnp.concatenate(_split3(x), axis=1))
    return out[:, :n] + out[:, n:2 * n] + out[:, 2 * n:]


def _sum_select(x, sel):
    m = x.shape[0]
    out = _dot(jnp.concatenate(_split3(x), axis=0), sel.astype(BF16))
    return out[:m] + out[m:2 * m] + out[2 * m:]


def _inproj_outputs(xn, qkv, logf, c, p, p_prev, w, outs):
    (mu_ref, sel_ref, ones_ref, w2_ref, a2_ref, g2_ref, w0_ref, a0_ref) = w
    (q_ref, k_ref, kb_ref, v_ref, vb_ref, lf_ref, cq_ref, ck_ref,
     r_ref, lw_ref, kr_ref, vr_ref, a_ref, g_ref) = outs
    q_ref[0] = (qkv[:, :WIDTH] * (HEAD_DIM ** -0.5)).astype(BF16)
    k = qkv[:, WIDTH:2 * WIDTH]
    k_ref[0] = k.T
    kb_ref[0] = k.astype(BF16)
    v = qkv[:, 2 * WIDTH:]
    v_ref[0] = v.T
    vb_ref[0] = v.astype(BF16)
    lf_ref[0] = logf.T[:N_HEADS, :]
    cparts = jnp.concatenate(_split3(c), axis=1)
    cc = _dot(cparts, sel_ref[...]) + ones_ref[...]
    cq_ref[0] = cc[:, :LANES].astype(BF16)
    ck_ref[0] = cc[:, LANES:].astype(BF16)
    rw = p + (p_prev - p) * mu_ref[...]
    r_ref[0] = rw[:, :WIDTH]
    kr_ref[0] = rw[:, WIDTH:2 * WIDTH]
    vr_ref[0] = rw[:, 2 * WIDTH:3 * WIDTH]
    z = rw[:, 3 * WIDTH:3 * WIDTH + LANES]
    gl = rw[:, 3 * WIDTH + LANES:]
    w_log = -_softplus(-(w0_ref[...] + _dot(jnp.tanh(z).astype(BF16), w2_ref[...]))) - 0.5
    lw_ref[0] = -jnp.exp(w_log)
    a_ref[0] = _sigmoid(a0_ref[...] + _dot(z.astype(BF16), a2_ref[...]))
    g_ref[0] = _dot(_sigmoid(gl).astype(BF16), g2_ref[...])


def _inproj_long_kernel(x_ref, prow_ref, c0_ref, ln_ref, wqkv_ref, wf_ref, bf_ref, wrw_ref,
                        mu_ref, sel_ref, ones_ref, w2_ref, a2_ref, g2_ref, w0_ref, a0_ref,
                        q_ref, k_ref, kb_ref, v_ref, vb_ref, lf_ref, cq_ref, ck_ref,
                        r_ref, lw_ref, kr_ref, vr_ref, a_ref, g_ref, xl_ref, pl_ref, cl_ref,
                        pcar_ref, ccar_ref):
    @pl.when(pl.program_id(1) == 0)
    def _():
        pcar_ref[...] = prow_ref[0]
        ccar_ref[...] = c0_ref[0]

    rows = x_ref.shape[1]
    xn = _rms(x_ref[0], ln_ref[...])
    xb = xn.astype(BF16)
    qkv = _dot(xb, wqkv_ref[...])
    logf = -_softplus(-(_dot(xb, wf_ref[...]) + bf_ref[...]))
    ri = lax.broadcasted_iota(jnp.int32, (rows, rows), 0)
    ci = lax.broadcasted_iota(jnp.int32, (rows, rows), 1)
    c = _select_sum((ri >= ci).astype(F32), logf) + ccar_ref[...]
    ccar_ref[...] = c[rows - 1:rows, :]
    p = _dot(xb, wrw_ref[...])
    first = lax.broadcasted_iota(jnp.int32, p.shape, 0) == 0
    p_prev = jnp.where(first, pcar_ref[...], pltpu.roll(p, 1, 0))
    pcar_ref[...] = p[rows - 1:rows, :]
    xl_ref[0] = xn[rows - 1:rows, :]
    pl_ref[0] = p[rows - 1:rows, :]
    cl_ref[0] = c[rows - 1:rows, :]
    _inproj_outputs(xn, qkv, logf, c, p, p_prev,
                    (mu_ref, sel_ref, ones_ref, w2_ref, a2_ref, g2_ref, w0_ref, a0_ref),
                    (q_ref, k_ref, kb_ref, v_ref, vb_ref, lf_ref, cq_ref, ck_ref,
                     r_ref, lw_ref, kr_ref, vr_ref, a_ref, g_ref))


def _inproj_packed_kernel(x_ref, xprev_ref, ln_ref, wqkv_ref, wf_ref, bf_ref, wrw_ref,
                          mu_ref, sel_ref, ones_ref, w2_ref, a2_ref, g2_ref, w0_ref, a0_ref,
                          q_ref, k_ref, kb_ref, v_ref, vb_ref, lf_ref, cq_ref, ck_ref,
                          r_ref, lw_ref, kr_ref, vr_ref, a_ref, g_ref, xn_ref, *, seq_len):
    rows = x_ref.shape[1]
    shift = seq_len.bit_length() - 1
    xn = _rms(x_ref[0], ln_ref[...])
    xn_ref[0] = xn
    xb = xn.astype(BF16)
    qkv = _dot(xb, wqkv_ref[...])
    logf = -_softplus(-(_dot(xb, wf_ref[...]) + bf_ref[...]))
    ri = lax.broadcasted_iota(jnp.int32, (rows, rows), 0)
    ci = lax.broadcasted_iota(jnp.int32, (rows, rows), 1)
    same_seq = (ri >> shift) == (ci >> shift)
    c = _select_sum(((ri >= ci) & same_seq).astype(F32), logf)
    p = _dot(xb, wrw_ref[...])
    p_first = _dot(xprev_ref[0].astype(BF16), wrw_ref[...])
    first = (lax.broadcasted_iota(jnp.int32, p.shape, 0) & (seq_len - 1)) == 0
    p_prev = jnp.where(first, p_first, pltpu.roll(p, 1, 0))
    _inproj_outputs(xn, qkv, logf, c, p, p_prev,
                    (mu_ref, sel_ref, ones_ref, w2_ref, a2_ref, g2_ref, w0_ref, a0_ref),
                    (q_ref, k_ref, kb_ref, v_ref, vb_ref, lf_ref, cq_ref, ck_ref,
                     r_ref, lw_ref, kr_ref, vr_ref, a_ref, g_ref))


def _const_spec(arr):
    return pl.BlockSpec(arr.shape, lambda *_: (0,) * arr.ndim)


def _inproj_out_shapes(batch, rows_total):
    def s(width, dtype):
        return jax.ShapeDtypeStruct((batch, rows_total, width), dtype)
    def t(height):
        return jax.ShapeDtypeStruct((batch, height, rows_total), F32)
    return [s(WIDTH, BF16), t(WIDTH), s(WIDTH, BF16), t(WIDTH), s(WIDTH, BF16),
            t(N_HEADS), s(LANES, BF16), s(LANES, BF16)] + [s(WIDTH, F32)] * 6


def _inproj_out_specs(tile):
    def s(width):
        return pl.BlockSpec((1, tile, width), lambda b, i: (b, i, 0))

    def t(height):
        return pl.BlockSpec((1, height, tile), lambda b, i: (b, 0, i))
    return [s(WIDTH), t(WIDTH), s(WIDTH), t(WIDTH), s(WIDTH),
            t(N_HEADS), s(LANES), s(LANES)] + [s(WIDTH)] * 6


def _inproj_long(x, prow, c0, weights, tile):
    batch, seq, _ = x.shape
    n_tiles = seq // tile
    row = lambda width: pl.BlockSpec((1, 1, width), lambda b, i: (b, 0, 0))
    shapes = _inproj_out_shapes(batch, seq) + [
        jax.ShapeDtypeStruct((batch, 1, D_MODEL), F32),
        jax.ShapeDtypeStruct((batch, 1, RWKV_COLS), F32),
        jax.ShapeDtypeStruct((batch, 1, LANES), F32)]
    specs = _inproj_out_specs(tile) + [row(D_MODEL), row(RWKV_COLS), row(LANES)]
    return pl.pallas_call(
        _inproj_long_kernel,
        grid=(batch, n_tiles),
        in_specs=[pl.BlockSpec((1, tile, D_MODEL), lambda b, i: (b, i, 0)),
                  _const_spec(prow), _const_spec(c0)] + [_const_spec(w) for w in weights],
        out_specs=specs,
        out_shape=shapes,
        scratch_shapes=[pltpu.VMEM((1, RWKV_COLS), F32), pltpu.VMEM((1, LANES), F32)],
        compiler_params=pltpu.CompilerParams(
            dimension_semantics=("arbitrary", "arbitrary"), vmem_limit_bytes=VMEM_LIMIT),
        name="inproj_long",
    )(x, prow, c0, *weights)


def _inproj_packed(x, xprev, weights, seq_len):
    _, rows, _ = x.shape
    full = pl.BlockSpec((1, rows, D_MODEL), lambda b, i: (0, 0, 0))
    return pl.pallas_call(
        functools.partial(_inproj_packed_kernel, seq_len=seq_len),
        grid=(1, 1),
        in_specs=[full, full] + [_const_spec(w) for w in weights],
        out_specs=_inproj_out_specs(rows) + [full],
        out_shape=_inproj_out_shapes(1, rows) + [jax.ShapeDtypeStruct((1, rows, D_MODEL), F32)],
        compiler_params=pltpu.CompilerParams(
            dimension_semantics=("arbitrary", "arbitrary"), vmem_limit_bytes=VMEM_LIMIT),
        name="inproj_packed",
    )(x, xprev, *weights)


def _fox_prompt_kernel(q_ref, cq_ref, km_ref, ckm_ref, vm_ref, k_ref, ck_ref, v_ref, o_ref):
    pair = pl.program_id(1)
    qi = pl.program_id(2)
    tile = q_ref.shape[1]
    lane = lax.broadcasted_iota(jnp.int32, (1, LANES), 1)
    q2 = q_ref[0]
    cq = cq_ref[0]
    n_sub = tile // ATT_SUB
    lhs = []
    for hh in range(2):
        head_lanes = (lane >= hh * HEAD_DIM) & (lane < (hh + 1) * HEAD_DIM)
        g0 = (2 * pair + hh) * BIAS_GROUP
        group_lanes = (lane >= g0) & (lane < g0 + BIAS_GROUP)
        full = jnp.concatenate([q2 * _mask_bf16(head_lanes), cq * _mask_bf16(group_lanes)], axis=1)
        lhs.append([full[r * ATT_SUB:(r + 1) * ATT_SUB, :] for r in range(n_sub)])
    query_minus_key = (lax.broadcasted_iota(jnp.int32, (ATT_K_TILE, ATT_SUB), 1)
                       - lax.broadcasted_iota(jnp.int32, (ATT_K_TILE, ATT_SUB), 0))

    def block(carry, kk, vv, first_col):
        out = list(carry)
        live = [(hh, r) for hh in range(2) for r in range(n_sub)
                if first_col is None or first_col <= (r + 1) * ATT_SUB - 1]
        for g0 in range(0, len(live), ATT_GROUP):
            group = live[g0:g0 + ATT_GROUP]
            scores, m_news = [], []
            for hh, r in group:
                s = _dot_nt(kk, lhs[hh][r])
                if first_col is not None and first_col + kk.shape[0] - 1 > r * ATT_SUB:
                    s = jnp.where(query_minus_key >= first_col - r * ATT_SUB, s, NEG_BIG)
                scores.append(s)
            for (hh, r), s in zip(group, scores):
                m_news.append(jnp.maximum(out[hh * n_sub + r][0], jnp.max(s, axis=0, keepdims=True)))
            for (hh, r), s, m_new in zip(group, scores, m_news):
                m, l, acc = out[hh * n_sub + r]
                alpha = jnp.exp(m - m_new)
                pe = jnp.exp(s - m_new)
                l_new = alpha * l + jnp.sum(pe, axis=0, keepdims=True)
                acc_new = alpha * acc + _dot_tn(vv, pe.astype(BF16))
                out[hh * n_sub + r] = (m_new, l_new, acc_new)
        return tuple(out)

    init = tuple((jnp.full((1, ATT_SUB), NEG_BIG, F32), jnp.zeros((1, ATT_SUB), F32),
                  jnp.zeros((LANES, ATT_SUB), F32)) for _ in range(2 * n_sub))
    carry = block(init, jnp.concatenate([km_ref[0], ckm_ref[0]], axis=1), vm_ref[0], None)

    def keys(j):
        start = pl.multiple_of(j * ATT_K_TILE, ATT_K_TILE)
        rows = pl.ds(start, ATT_K_TILE)
        return jnp.concatenate([k_ref[0, rows, :], ck_ref[0, rows, :]], axis=1), v_ref[0, rows, :]

    per_q = tile // ATT_K_TILE
    carry = lax.fori_loop(0, qi * per_q, lambda j, c: block(c, *keys(j), None), carry)
    for d in range(per_q):
        carry = block(carry, *keys(qi * per_q + d), d * ATT_K_TILE)
    first_head = lax.broadcasted_iota(jnp.int32, (LANES, ATT_SUB), 0) < HEAD_DIM
    for r in range(n_sub):
        (_, l0, acc0), (_, l1, acc1) = carry[r], carry[n_sub + r]
        o_t = jnp.where(first_head, acc0 / l0, acc1 / l1)
        o_ref[0, r * ATT_SUB:(r + 1) * ATT_SUB, :] = o_t.T.astype(BF16)


def _fox_prompt(q, cq, k_meta, ck_meta, v_meta, k, ck, v):
    batch, seq, _ = q.shape
    n_pairs = N_HEADS // 2
    return pl.pallas_call(
        _fox_prompt_kernel,
        grid=(batch, n_pairs, seq // ATT_TILE),
        in_specs=[
            pl.BlockSpec((1, ATT_TILE, LANES), lambda b, p, i: (b, i, p)),
            pl.BlockSpec((1, ATT_TILE, LANES), lambda b, p, i: (b, i, 0)),
            pl.BlockSpec((1, N_META, LANES), lambda b, p, i: (0, 0, p)),
            pl.BlockSpec((1, N_META, LANES), lambda b, p, i: (0, 0, 0)),
            pl.BlockSpec((1, N_META, LANES), lambda b, p, i: (0, 0, p)),
            pl.BlockSpec((1, seq, LANES), lambda b, p, i: (b, 0, p)),
            pl.BlockSpec((1, seq, LANES), lambda b, p, i: (b, 0, 0)),
            pl.BlockSpec((1, seq, LANES), lambda b, p, i: (b, 0, p)),
        ],
        out_specs=pl.BlockSpec((1, ATT_TILE, LANES), lambda b, p, i: (b, i, p)),
        out_shape=jax.ShapeDtypeStruct((batch, seq, WIDTH), BF16),
        compiler_params=pltpu.CompilerParams(
            dimension_semantics=("arbitrary", "arbitrary", "arbitrary"),
            vmem_limit_bytes=VMEM_LIMIT),
        name="fox_prompt",
    )(q, cq, k_meta, ck_meta, v_meta, k, ck, v)


def _fox_sample_kernel(pt_ref, q_ref, cq_ref, kn_ref, ckn_ref, vn_ref, *rest, n_new):
    g_pages = PAGES_PER_STEP
    k_refs, v_refs, lf_pool_ref = rest[:g_pages], rest[g_pages:2 * g_pages], rest[2 * g_pages]
    o_ref, q_sc, cqm_sc, roff_sc, carry_sc, m_sc, l_sc, acc_sc = rest[2 * g_pages + 1:]
    step = pl.program_id(1)
    rows = N_HEADS * n_new
    row_head = lax.broadcasted_iota(jnp.int32, (rows, 1), 0) & (N_HEADS - 1)

    def per_head_rows(x):
        return jnp.concatenate([jnp.broadcast_to(x[i:i + 1, :], (N_HEADS, x.shape[1]))
                                for i in range(n_new)], axis=0)

    @pl.when(step == 0)
    def _():
        q = per_head_rows(q_ref[0].astype(F32))
        lane_head = lax.broadcasted_iota(jnp.int32, (1, WIDTH), 1) >> HEAD_SHIFT
        q_sc[...] = jnp.where(row_head == lane_head, q, 0.0).astype(BF16)
        cq = per_head_rows(cq_ref[0].astype(F32))
        lane = lax.broadcasted_iota(jnp.int32, (1, LANES), 1)
        cqm = jnp.where((lane >> GROUP_SHIFT) == row_head, cq, 0.0)
        cqm_sc[...] = cqm.astype(BF16)
        slot = lax.broadcasted_iota(jnp.int32, cqm.shape, 1) & (BIAS_GROUP - 1)
        roff_sc[...] = jnp.sum(jnp.where(slot < BIAS_PARTS, cqm, 0.0), axis=-1, keepdims=True)
        carry_sc[...] = jnp.zeros(carry_sc.shape, F32)
        m_sc[...] = jnp.full(m_sc.shape, NEG_BIG, F32)
        l_sc[...] = jnp.zeros(l_sc.shape, F32)
        acc_sc[...] = jnp.zeros(acc_sc.shape, F32)

    q_bd = q_sc[...]
    ti = lax.broadcasted_iota(jnp.int32, (PAGE_SIZE, PAGE_SIZE), 0)
    tj = lax.broadcasted_iota(jnp.int32, (PAGE_SIZE, PAGE_SIZE), 1)
    later_and_all = jnp.concatenate([(ti > tj).astype(F32), jnp.ones((PAGE_SIZE, PAGE_SIZE), F32)],
                                    axis=1)
    first = (pl.program_id(0) * pl.num_programs(1) + step) * g_pages
    lf_all = jnp.concatenate([lf_pool_ref[pt_ref[first + g]] for g in range(g_pages)], axis=0)
    sums = _sum_select(lf_all, later_and_all)

    carry = carry_sc[...]
    roff = roff_sc[...]
    def pages_t(refs, g0):
        return jnp.concatenate([refs[g][0].reshape(WIDTH, PAGE_SIZE) for g in (g0, g0 + 1)],
                               axis=1).astype(BF16)

    scores = []
    for g0 in range(0, g_pages, 2):
        bias = []
        for g in (g0, g0 + 1):
            page_sums = sums[g * N_HEADS:(g + 1) * N_HEADS, :]
            bias.append(jnp.concatenate([page_sums[:, :PAGE_SIZE] + carry] * n_new, axis=0))
            carry = carry + page_sums[:, PAGE_SIZE:]
        scores.append(_dot(q_bd, pages_t(k_refs, g0)) + jnp.concatenate(bias, axis=1) + roff)
    carry_sc[...] = carry

    m = m_sc[...]
    m_new = m
    for s in scores:
        m_new = jnp.maximum(m_new, jnp.max(s, axis=-1, keepdims=True))
    alpha = jnp.exp(m - m_new)
    l_new = alpha * l_sc[...]
    acc = alpha * acc_sc[...]
    for i, s in enumerate(scores):
        pe = jnp.exp(s - m_new)
        l_new = l_new + jnp.sum(pe, axis=-1, keepdims=True)
        acc = acc + _dot_nt(pe.astype(BF16), pages_t(v_refs, 2 * i))
    m_sc[...] = m_new
    l_sc[...] = l_new
    acc_sc[...] = acc

    @pl.when(step == pl.num_programs(1) - 1)
    def _():
        s = _dot_nt(q_bd, kn_ref[0]) + _dot_nt(cqm_sc[...], ckn_ref[0])
        row_tok = lax.broadcasted_iota(jnp.int32, (rows, 1), 0) >> (N_HEADS.bit_length() - 1)
        s = jnp.where(lax.broadcasted_iota(jnp.int32, (1, n_new), 1) <= row_tok, s, NEG_BIG)
        m_fin = jnp.maximum(m_new, jnp.max(s, axis=-1, keepdims=True))
        a_fin = jnp.exp(m_new - m_fin)
        pe = jnp.exp(s - m_fin)
        l_fin = a_fin * l_new + jnp.sum(pe, axis=-1, keepdims=True)
        out = (a_fin * acc + _dot(pe.astype(BF16), vn_ref[0])) / l_fin
        own = (lax.broadcasted_iota(jnp.int32, (N_HEADS, WIDTH), 0)
               == lax.broadcasted_iota(jnp.int32, (N_HEADS, WIDTH), 1) >> HEAD_SHIFT)
        o_ref[0] = jnp.concatenate(
            [jnp.sum(jnp.where(own, out[i * N_HEADS:(i + 1) * N_HEADS, :], 0.0), axis=0, keepdims=True)
             for i in range(n_new)], axis=0).astype(BF16)


def _fox_sample(page_table, q, cq, k_new, ck_new, v_new, cache_k, cache_v, cache_logf):
    n_seq, n_new, _ = q.shape
    n_pages = page_table.shape[1]
    g_pages = PAGES_PER_STEP
    rows = N_HEADS * n_new

    def per_seq(width):
        return pl.BlockSpec((1, n_new, width), lambda b, j, pt: (b, 0, 0))

    def page_spec(g, arr):
        block = (1,) + arr.shape[1:]
        return pl.BlockSpec(block, lambda b, j, pt: (pt[b * n_pages + j * g_pages + g],)
                            + (0,) * (arr.ndim - 1))

    visit_order = page_table[:, ::-1].reshape(-1)

    grid_spec = pltpu.PrefetchScalarGridSpec(
        num_scalar_prefetch=1,
        grid=(n_seq, n_pages // g_pages),
        in_specs=([per_seq(WIDTH), per_seq(LANES), per_seq(WIDTH), per_seq(LANES), per_seq(WIDTH)]
                  + [page_spec(g, cache_k) for g in range(g_pages)]
                  + [page_spec(g, cache_v) for g in range(g_pages)]
                  + [pl.BlockSpec(cache_logf.shape, lambda b, j, pt: (0, 0, 0),
                                  pipeline_mode=pl.Buffered(1))]),
        out_specs=pl.BlockSpec((1, n_new, WIDTH), lambda b, j, pt: (b, 0, 0)),
        scratch_shapes=[pltpu.VMEM((rows, WIDTH), BF16), pltpu.VMEM((rows, LANES), BF16),
                        pltpu.VMEM((rows, 1), F32), pltpu.VMEM((N_HEADS, PAGE_SIZE), F32),
                        pltpu.VMEM((rows, 1), F32), pltpu.VMEM((rows, 1), F32),
                        pltpu.VMEM((rows, WIDTH), F32)],
    )
    return pl.pallas_call(
        functools.partial(_fox_sample_kernel, n_new=n_new),
        grid_spec=grid_spec,
        out_shape=jax.ShapeDtypeStruct((n_seq, n_new, WIDTH), BF16),
        compiler_params=pltpu.CompilerParams(
            dimension_semantics=("arbitrary", "arbitrary"), vmem_limit_bytes=VMEM_LIMIT),
        name="fox_sample",
    )(visit_order, q, cq, k_new, ck_new, v_new,
      *([cache_k] * g_pages), *([cache_v] * g_pages), cache_logf)


def _wkv_kernel(r_ref, lw_ref, kr_ref, vr_ref, a_ref, g_ref, s0_ref,
                kk_ref, ka_ref, rk_ref, gg_ref, gb_ref, o_ref, so_ref, s_sc):
    nb, C, _ = r_ref.shape
    n_pairs = N_HEADS // 2
    units = [(bb, pair) for bb in range(nb) for pair in range(n_pairs)]
    lo_state = ((lax.broadcasted_iota(jnp.int32, (LANES, LANES), 0) < HEAD_DIM)
                == (lax.broadcasted_iota(jnp.int32, (LANES, LANES), 1) < HEAD_DIM))

    @pl.when(pl.program_id(1) == 0)
    def _():
        zeros = jnp.zeros((HEAD_DIM, HEAD_DIM), F32)
        for bb, pair in units:
            sb = bb if s0_ref.shape[0] == nb else 0
            top = jnp.concatenate([s0_ref[sb, 2 * pair], zeros], axis=1)
            bottom = jnp.concatenate([zeros, s0_ref[sb, 2 * pair + 1]], axis=1)
            s_sc[bb, pair] = jnp.concatenate([top, bottom], axis=0)

    ri = lax.broadcasted_iota(jnp.int32, (C, C), 0)
    ci = lax.broadcasted_iota(jnp.int32, (C, C), 1)
    incl = ri >= ci
    strict = ri > ci
    eye = (ri == ci).astype(F32)
    lo = lax.broadcasted_iota(jnp.int32, (C, LANES), 1) < HEAD_DIM
    lo2 = lax.broadcasted_iota(jnp.int32, (2 * C, LANES), 1) < HEAD_DIM
    levels = max(1, (C - 1).bit_length())
    lcum = [_select_sum(incl.astype(F32), lw_ref[bb]) for bb in range(nb)]

    def per_head(x):
        first = jnp.sum(jnp.where(lo, x, 0.0), axis=-1, keepdims=True)
        second = jnp.sum(jnp.where(lo, 0.0, x), axis=-1, keepdims=True)
        return jnp.where(lo, first, second)

    xs, x_all, bts, kts, bgs, kgs, vbs, kps, e_ends = [], [], [], [], [], [], [], [], []
    for bb, pair in units:
        lanes = slice(pair * LANES, (pair + 1) * LANES)
        r2, lw2, kr2, a2 = (ref[bb, :, lanes] for ref in (r_ref, lw_ref, kr_ref, a_ref))
        lc = lcum[bb][:, lanes]
        kkr = kr2 * kk_ref[:, lanes]
        kk = kkr / jnp.maximum(jnp.sqrt(per_head(kkr * kkr)), 1e-12)
        kp = kr2 * (1.0 + (a2 - 1.0) * ka_ref[:, lanes])
        bv = kk * a2
        e_neg = jnp.exp(-lc)
        l_end = lc[C - 1:C, :]
        e_rem = jnp.exp(l_end - lc)
        x2 = jnp.concatenate([-kk * jnp.exp(lc - lw2), r2 * jnp.exp(lc)], axis=0)
        x_all.append(x2.astype(BF16))
        xs.append((jnp.where(lo2, x2, 0.0).astype(BF16), jnp.where(lo2, 0.0, x2).astype(BF16)))
        bts.append((bv * e_neg).astype(BF16))
        kts.append((kp * e_neg).astype(BF16))
        bgs.append((bv * e_rem).astype(BF16))
        kgs.append((kp * e_rem).astype(BF16))
        vbs.append(vr_ref[bb, :, lanes].astype(BF16))
        kps.append(kp)
        e_ends.append(jnp.exp(l_end))

    chains = [(u, hh) for u in range(len(units)) for hh in range(2)]
    sc_b = [_dot_nt(xs[u][hh], bts[u]) for u, hh in chains]
    sc_k = [_dot_nt(xs[u][hh], kts[u]) for u, hh in chains]
    n_ch = range(len(chains))
    a_ab = [jnp.where(strict, sc_b[i][:C], 0.0) for i in n_ch]
    a_ak = [jnp.where(strict, sc_k[i][:C], 0.0).astype(BF16) for i in n_ch]
    a_rb = [jnp.where(incl, sc_b[i][C:], 0.0).astype(BF16) for i in n_ch]
    a_rk = [jnp.where(incl, sc_k[i][C:], 0.0).astype(BF16) for i in n_ch]

    tinv = [eye + a_ab[i] for i in n_ch]
    if levels > 1:
        lb = [a_ab[i].astype(BF16) for i in n_ch]
        lpow = [_dot(lb[i], lb[i]) for i in n_ch]
        for level in range(1, levels):
            qb = [lpow[i].astype(BF16) for i in n_ch]
            if level == levels - 1:
                tinv = [tinv[i] + _dot(tinv[i].astype(BF16), qb[i]) for i in n_ch]
            else:
                both = [_dot(jnp.concatenate([tinv[i], lpow[i]], axis=0).astype(BF16), qb[i])
                        for i in n_ch]
                tinv = [tinv[i] + both[i][:C] for i in n_ch]
                lpow = [both[i][C:] for i in n_ch]
    tb = [tinv[i].astype(BF16) for i in n_ch]

    def both_heads(u, f):
        return jnp.where(lo, f(2 * u), f(2 * u + 1))

    n_u = range(len(units))
    s_old = [s_sc[bb, pair] for bb, pair in units]
    ps = [_dot_nt(x_all[u], s_old[u].astype(BF16)) for u in n_u]
    w1 = [(ps[u][:C] + both_heads(u, lambda i, u=u: _dot(a_ak[i], vbs[u]))).astype(BF16) for u in n_u]
    ub = [both_heads(u, lambda i, u=u: _dot(tb[i], w1[u])).astype(BF16) for u in n_u]
    ys = [ps[u][C:] + both_heads(u, lambda i, u=u: _dot(a_rb[i], ub[u]) + _dot(a_rk[i], vbs[u]))
          for u in n_u]
    for u, (bb, pair) in enumerate(units):
        cross = _dot_tn(ub[u], bgs[u]) + _dot_tn(vbs[u], kgs[u])
        s_sc[bb, pair] = s_old[u] * e_ends[u] + jnp.where(lo_state, cross, 0.0)

    for u, (bb, pair) in enumerate(units):
        lanes = slice(pair * LANES, (pair + 1) * LANES)
        y = ys[u]
        yc = y - per_head(y) * (1.0 / HEAD_DIM)
        var = per_head(yc * yc) * (1.0 / HEAD_DIM)
        yn = yc * lax.rsqrt(var + GN_EPS) * gg_ref[:, lanes] + gb_ref[:, lanes]
        bonus = per_head(r_ref[bb, :, lanes] * kps[u] * rk_ref[:, lanes]) * vr_ref[bb, :, lanes]
        o_ref[bb, :, lanes] = ((yn + bonus) * g_ref[bb, :, lanes]).astype(BF16)

    @pl.when(pl.program_id(1) == pl.num_programs(1) - 1)
    def _():
        for bb, pair in units:
            s2 = s_sc[bb, pair]
            so_ref[bb, 2 * pair] = s2[:HEAD_DIM, :HEAD_DIM]
            so_ref[bb, 2 * pair + 1] = pltpu.roll(s2[HEAD_DIM:, :], HEAD_DIM, 1)[:, :HEAD_DIM]


def _wkv(r, lw, kr, vr, a, g, s0, params, chunk, batch_tile):
    batch, seq, _ = r.shape
    assert chunk & (chunk - 1) == 0 and seq % chunk == 0 and batch % batch_tile == 0
    shared_state = s0.shape[0] == 1
    tok = pl.BlockSpec((batch_tile, chunk, WIDTH), lambda b, c: (b, c, 0))
    state_out = pl.BlockSpec((batch_tile, N_HEADS, HEAD_DIM, HEAD_DIM), lambda b, c: (b, 0, 0, 0))
    state_in = (pl.BlockSpec((1, N_HEADS, HEAD_DIM, HEAD_DIM), lambda b, c: (0, 0, 0, 0))
                if shared_state else state_out)
    return pl.pallas_call(
        _wkv_kernel,
        grid=(batch // batch_tile, seq // chunk),
        in_specs=[tok] * 6 + [state_in] + [_const_spec(p) for p in params],
        out_specs=[tok, state_out],
        out_shape=[jax.ShapeDtypeStruct((batch, seq, WIDTH), BF16),
                   jax.ShapeDtypeStruct((batch, N_HEADS, HEAD_DIM, HEAD_DIM), F32)],
        scratch_shapes=[pltpu.VMEM((batch_tile, N_HEADS // 2, LANES, LANES), F32)],
        compiler_params=pltpu.CompilerParams(
            dimension_semantics=("arbitrary", "arbitrary"), vmem_limit_bytes=VMEM_LIMIT),
        name="wkv",
    )(r, lw, kr, vr, a, g, s0, *params)


def _out_ffn_kernel(x_ref, att_ref, rw_ref, woa_ref, wor_ref, ln2_ref, wg_ref, wu_ref, wd_ref,
                    lnf_ref, y_ref):
    h = x_ref[...] + _dot(att_ref[...], woa_ref[...]) + _dot(rw_ref[...], wor_ref[...])
    hb = _rms(h, ln2_ref[...]).astype(BF16)
    ffn = jnp.zeros(h.shape, F32)
    d_ff = wg_ref.shape[1]
    for c0 in range(0, d_ff, FF_CHUNK):
        gate = _dot(hb, wg_ref[:, c0:c0 + FF_CHUNK])
        up = _dot(hb, wu_ref[:, c0:c0 + FF_CHUNK])
        act = gate * _sigmoid(gate) * up
        ffn = ffn + _dot(act.astype(BF16), wd_ref[c0:c0 + FF_CHUNK, :])
    y_ref[...] = _rms(h + ffn, lnf_ref[...])


def _out_ffn(x, att, rw, weights):
    rows = x.shape[0]
    tile = min(FFN_ROW_TILE, rows)
    tok = lambda width: pl.BlockSpec((tile, width), lambda i: (i, 0))
    resident = lambda w: pl.BlockSpec(w.shape, lambda i: (0,) * w.ndim, pipeline_mode=pl.Buffered(1))
    return pl.pallas_call(
        _out_ffn_kernel,
        grid=(rows // tile,),
        in_specs=[tok(D_MODEL), tok(WIDTH), tok(WIDTH)] + [resident(w) for w in weights],
        out_specs=tok(D_MODEL),
        out_shape=jax.ShapeDtypeStruct((rows, D_MODEL), F32),
        compiler_params=pltpu.CompilerParams(
            dimension_semantics=("arbitrary",), vmem_limit_bytes=VMEM_LIMIT),
        name="out_ffn",
    )(x, att, rw, *weights)


def _bias_selectors():
    part = jnp.arange(BIAS_PARTS)[:, None, None]
    src = jnp.arange(LANES)[None, :, None]
    dst = jnp.arange(LANES)[None, None, :]
    is_head = src < N_HEADS
    q_sel = (is_head & (dst == src * BIAS_GROUP + part)).astype(F32)
    k_sel = -(is_head & (dst == src * BIAS_GROUP + BIAS_PARTS + part)).astype(F32)
    sel = jnp.concatenate([q_sel, k_sel], axis=2).reshape(BIAS_PARTS * LANES, 2 * LANES)
    slot = jnp.arange(LANES) % BIAS_GROUP
    ones = jnp.concatenate([(slot >= BIAS_PARTS) & (slot < 2 * BIAS_PARTS),
                            slot < BIAS_PARTS]).astype(F32)[None, :]
    return sel.astype(BF16), ones


def kernel(x_prompt, x_sample, cache_k, cache_v, cache_logf, state_wkv, state_shift, page_table,
           meta_tokens, ln1_g, w_in, b_f, mu_shift, w0, w2, a0, a2, g2, k_k, k_a, r_k,
           gn_g, gn_b, w_out, ln2_g, w_gate, w_up, w_down, lnf_g):
    batch, seq, _ = x_prompt.shape
    n_seq, n_new, _ = x_sample.shape
    fox_cols = 3 * WIDTH + N_HEADS

    w_in0 = w_in[0]
    wqkv = w_in0[:, :3 * WIDTH].astype(BF16)
    wf = jnp.pad(w_in0[:, 3 * WIDTH:fox_cols], ((0, 0), (0, LANES - N_HEADS))).astype(BF16)
    bf = jnp.pad(b_f[0], (0, LANES - N_HEADS))[None, :]
    wrw = w_in0[:, fox_cols:].astype(BF16)
    sel, ones = _bias_selectors()
    w2p = jnp.concatenate([w2[0], jnp.zeros((ICLR_LORA, WIDTH), F32)], axis=0).astype(BF16)
    a2p = jnp.concatenate([jnp.zeros((DECAY_LORA, WIDTH), F32), a2[0]], axis=0).astype(BF16)
    inproj_w = (ln1_g[0][None, :], wqkv, wf, bf, wrw, mu_shift[0][None, :], sel, ones,
                w2p, a2p, g2[0].astype(BF16), w0[0][None, :], a0[0][None, :])
    wkv_p = tuple(t[0].reshape(1, WIDTH) for t in (k_k, k_a, r_k, gn_g, gn_b))
    ffn_w = (w_out[0][:WIDTH].astype(BF16), w_out[0][WIDTH:].astype(BF16), ln2_g[0][None, :],
             w_gate[0].astype(BF16), w_up[0].astype(BF16), w_down[0].astype(BF16), lnf_g[None, :])

    zrow = jnp.zeros((1, 1, RWKV_COLS), F32)
    zc = jnp.zeros((1, 1, LANES), F32)
    mo = _inproj_long(meta_tokens[None], zrow, zc, inproj_w, N_META)
    (_, mk, mkb, mv, mvb, mlf, _, mck, mr, mlw, mkr, mvr, ma, mg, _, mprow, mclast) = mo
    zstate = jnp.zeros((1, N_HEADS, HEAD_DIM, HEAD_DIM), F32)
    _, s_meta = _wkv(mr, mlw, mkr, mvr, ma, mg, zstate, wkv_p, N_META, 1)

    po = _inproj_long(x_prompt, mprow, mclast, inproj_w, ROW_TILE)
    (pq, pk, pkb, pv, pvb, plf, pcq, pck, pr, plw, pkr, pvr, pa, pg, pxl, _, _) = po
    att_p = _fox_prompt(pq, pcq, mkb, mck, mvb, pkb, pck, pvb)
    rw_p, wkv_prompt = _wkv(pr, plw, pkr, pvr, pa, pg, s_meta, wkv_p, WKV_CHUNK, WKV_PROMPT_SEQS)
    y_prompt = _out_ffn(x_prompt.reshape(batch * seq, D_MODEL), att_p.reshape(batch * seq, WIDTH),
                        rw_p.reshape(batch * seq, WIDTH), ffn_w).reshape(batch, seq, D_MODEL)

    rows_s = n_seq * n_new
    xprev = jnp.repeat(state_shift[0], n_new, axis=0)[None]
    so = _inproj_packed(x_sample.reshape(1, rows_s, D_MODEL), xprev, inproj_w, n_new)
    (sq, sk, skb, sv, svb, slf, scq, sck, sr, slw, skr, svr, sa, sg, sxn) = so
    seqs = lambda t: t.reshape(n_seq, n_new, t.shape[-1])
    att_s = _fox_sample(page_table, seqs(sq), seqs(scq), seqs(skb), seqs(sck), seqs(svb),
                        jnp.transpose(cache_k[0], (0, 2, 3, 1)), jnp.transpose(cache_v[0], (0, 2, 3, 1)),
                        jnp.transpose(cache_logf[0], (0, 2, 1)))
    rw_s, wkv_sample = _wkv(seqs(sr), seqs(slw), seqs(skr), seqs(svr), seqs(sa), seqs(sg),
                            state_wkv[0], wkv_p, n_new, WKV_SAMPLE_SEQS)
    y_sample = _out_ffn(x_sample.reshape(rows_s, D_MODEL), att_s.reshape(rows_s, WIDTH),
                        rw_s.reshape(rows_s, WIDTH), ffn_w).reshape(n_seq, n_new, D_MODEL)

    def with_meta(meta, main):
        return jnp.concatenate([jnp.broadcast_to(meta, (batch,) + meta.shape[1:]), main], axis=2)

    def prompt_heads(t):
        return jnp.transpose(t.reshape(batch, N_HEADS, HEAD_DIM, t.shape[-1]), (0, 3, 1, 2))[None]

    def sample_heads(t):
        return jnp.transpose(t.reshape(N_HEADS, HEAD_DIM, n_seq, n_new), (2, 3, 0, 1))[None]

    k_prompt = prompt_heads(with_meta(mk, pk))
    v_prompt = prompt_heads(with_meta(mv, pv))
    logf_prompt = jnp.transpose(with_meta(mlf, plf), (0, 2, 1))[None]
    shift_prompt = pxl.reshape(1, batch, D_MODEL)
    k_sample = sample_heads(sk)
    v_sample = sample_heads(sv)
    logf_sample = jnp.transpose(slf.reshape(N_HEADS, n_seq, n_new), (1, 2, 0))[None]
    shift_sample = seqs(sxn)[:, -1][None]
    return (y_prompt, y_sample, k_prompt, v_prompt, logf_prompt, wkv_prompt[None], shift_prompt,
            k_sample, v_sample, logf_sample, wkv_sample[None], shift_sample)
```
